```python
import jax, jax.numpy as jnp
from jax import lax
import numpy as np

D_MODEL = 4096
BATCH = 1
SEQ = 8192
DEPTH = 2
DEC_BATCH = 16
DEC_SEQ = 32
PAST_LEN = 1024

CHUNK = 64
HEAD_DIM = 128
Q_BLOCK = 128
NEG = -1e30
H_A = 16
H_B = 16
N_LEFT_CHUNKS = 8
BAND_REACH = N_LEFT_CHUNKS * CHUNK
REL_CLIP = 128
EVEN_SIZES = (H_A * HEAD_DIM,) * 3 + (H_B * HEAD_DIM,) * 3
EVEN_MIX = (H_A + H_B) * HEAD_DIM
H_C = 16
KV_C = 4
H_IDX = 16
D_IDX = 64
TOPK_MAX = 256
H_D = 16
DK_D = 128
DV_D = 128
CONV_W = 4
QKV_D = H_D * (2 * DK_D + DV_D)
ODD_SIZES = (H_C * HEAD_DIM, KV_C * HEAD_DIM, KV_C * HEAD_DIM, H_IDX * D_IDX, D_IDX, H_IDX,
             QKV_D, H_D, H_D, H_D * DV_D)
ODD_MIX = H_C * HEAD_DIM + H_D * DV_D
N_EXPERTS = 16
N_GROUPS = 4
EXPERTS_PER_GROUP = N_EXPERTS // N_GROUPS
TOP_K = 2
D_EXPERT = 1024
ROPE_THETA = 10000.0
LN_EPS = 1e-5
RMS_EPS = 1e-6
DEEPNORM_ALPHA = (2 * DEPTH) ** 0.25
DEEPNORM_BETA = (8 * DEPTH) ** -0.25

kernel_name = 'hybrid_streaming_encoder_step'

F32 = jnp.float32


def split_cols(h, sizes):
    cuts, acc = [], 0
    for s in sizes[:-1]:
        acc += s
        cuts.append(acc)
    return jnp.split(h, cuts, axis=-1)


def layer_norm(x, g, b):
    xf = x.astype(F32)
    mu = jnp.mean(xf, -1, keepdims=True)
    var = jnp.mean(jnp.square(xf - mu), -1, keepdims=True)
    return ((xf - mu) * lax.rsqrt(var + LN_EPS) * g.astype(F32) + b.astype(F32)).astype(x.dtype)


def rope(x, pos):
    half = x.shape[-1] // 2
    inv = ROPE_THETA ** (-jnp.arange(half, dtype=F32) / half)
    ang = pos.astype(F32)[:, None] * inv[None, :]
    cos, sin = jnp.cos(ang)[None, :, None, :], jnp.sin(ang)[None, :, None, :]
    xf = x.astype(F32)
    x1, x2 = xf[..., :half], xf[..., half:]
    return jnp.concatenate([x1 * cos - x2 * sin, x2 * cos + x1 * sin], -1).astype(x.dtype)


def stick_breaking(q, k, v, q_pos, k_pos):
    z = jnp.einsum('bqhd,bkhd->bhqk', q.astype(F32), k.astype(F32)) * (q.shape[-1] ** -0.5)
    earlier = (k_pos[None, :] < q_pos[:, None])[None, None]
    log_1mb = jnp.where(earlier, jax.nn.log_sigmoid(-z), 0.0)
    between = lax.cumsum(log_1mb, axis=3, reverse=True) - log_1mb
    w = jnp.where(earlier, jnp.exp(jax.nn.log_sigmoid(z) + between), 0.0)
    return jnp.einsum('bhqk,bkhd->bqhd', w, v.astype(F32)).astype(v.dtype)


def band_mask(q_pos, k_pos):
    qc = q_pos[:, None] // CHUNK
    kc = k_pos[None, :] // CHUNK
    return (kc <= qc) & (kc >= qc - N_LEFT_CHUNKS) & (k_pos[None, :] >= 0)


def band_bias(rel_bias, q_pos, k_pos):
    rel = jnp.clip(q_pos[:, None] - k_pos[None, :], -REL_CLIP, REL_CLIP) + REL_CLIP
    return rel_bias[:, rel]


def band_core(q, kb, vb, bias, valid):
    s = jnp.einsum('bnqhd,bnkhd->bnhqk', q.astype(F32), kb.astype(F32)) * (q.shape[-1] ** -0.5)
    s = jnp.where(valid[None, :, None], s + bias.astype(F32)[None, None], NEG)
    p = jax.nn.softmax(s, axis=-1)
    return jnp.einsum('bnhqk,bnkhd->bnqhd', p, vb.astype(F32)).astype(q.dtype)


def band_prompt(q, k, v, rel_bias):
    B, T, H, d = q.shape
    nc = T // CHUNK
    nb = N_LEFT_CHUNKS + 1
    pad = jnp.zeros((B, N_LEFT_CHUNKS * CHUNK, H, d), k.dtype)
    kp = jnp.concatenate([pad, k], 1).reshape(B, nc + N_LEFT_CHUNKS, CHUNK, H, d)
    vp = jnp.concatenate([pad, v], 1).reshape(B, nc + N_LEFT_CHUNKS, CHUNK, H, d)
    idx = jnp.arange(nc)[:, None] + jnp.arange(nb)[None, :]
    kband = kp[:, idx].reshape(B, nc, nb * CHUNK, H, d)
    vband = vp[:, idx].reshape(B, nc, nb * CHUNK, H, d)
    q_pos = jnp.arange(T).reshape(nc, CHUNK)
    k_pos = ((idx - N_LEFT_CHUNKS)[:, :, None] * CHUNK + jnp.arange(CHUNK)).reshape(nc, nb * CHUNK)
    valid = jax.vmap(band_mask)(q_pos, k_pos)
    bias = band_bias(rel_bias, q_pos[0], k_pos[0])
    out = band_core(q.reshape(B, nc, CHUNK, H, d), kband, vband, bias, valid)
    return out.reshape(B, T, H, d)


def dsa_attend(qc, qi, wi, q_pos, kc, vc, ki, k_pos, topk):
    B, Q = qc.shape[:2]
    logits = jnp.einsum('bqhe,bse->bqhs', qi.astype(F32), ki.astype(F32))
    score = jnp.einsum('bqh,bqhs->bqs', wi.astype(F32), jax.nn.relu(logits))
    adm = jnp.broadcast_to(((k_pos[None, :] // CHUNK) <= (q_pos[:, None] // CHUNK))[None], score.shape)
    _, sel = lax.top_k(jnp.where(adm, score, NEG), topk)
    sel_ok = jnp.take_along_axis(adm, sel, axis=-1)
    gather = jax.vmap(lambda rows, i: rows[i])
    ks, vs = gather(kc, sel), gather(vc, sel)
    qg = qc.reshape(B, Q, KV_C, H_C // KV_C, HEAD_DIM).astype(F32)
    s = jnp.einsum('bqgrd,bqkgd->bqgrk', qg, ks.astype(F32)) * (HEAD_DIM ** -0.5)
    s = jnp.where(sel_ok[:, :, None, None, :], s, NEG)
    p = jax.nn.softmax(s, axis=-1)
    o = jnp.einsum('bqgrk,bqkgd->bqgrd', p, vs.astype(F32))
    return o.reshape(B, Q, H_C, HEAD_DIM).astype(qc.dtype)


def causal_conv(raw, buf, conv_w):
    xp = jnp.concatenate([buf, raw], 1)
    out = lax.conv_general_dilated(xp, conv_w[:, None, :], window_strides=(1,), padding='VALID',
                                   dimension_numbers=('NWC', 'WIO', 'NWC'),
                                   feature_group_count=raw.shape[-1])
    return jax.nn.silu(out), xp[:, xp.shape[1] - (CONV_W - 1):]


def l2norm(t):
    return t * lax.rsqrt(jnp.sum(t * t, -1, keepdims=True) + RMS_EPS)


def gdn_inputs(qkv_raw, buf, a_raw, b_raw, conv_w, a_log, dt_bias):
    qkv, new_buf = causal_conv(qkv_raw, buf, conv_w)
    B, T, _ = qkv.shape
    q, k, v = split_cols(qkv, (H_D * DK_D, H_D * DK_D, H_D * DV_D))
    q = l2norm(q.reshape(B, T, H_D, DK_D).astype(F32)) * (DK_D ** -0.5)
    k = l2norm(k.reshape(B, T, H_D, DK_D).astype(F32))
    v = v.reshape(B, T, H_D, DV_D).astype(F32)
    g = -jnp.exp(a_log.astype(F32)) * jax.nn.softplus(a_raw.astype(F32) + dt_bias.astype(F32))
    beta = jax.nn.sigmoid(b_raw.astype(F32))
    return q, k, v, g, beta, new_buf


def gdn_chunk(S, q, k, v, g, beta):
    C = q.shape[1]
    qh, kh, vh = q.transpose(0, 2, 1, 3), k.transpose(0, 2, 1, 3), v.transpose(0, 2, 1, 3)
    G = jnp.cumsum(g, axis=1).transpose(0, 2, 1)
    bt = beta.transpose(0, 2, 1)
    incl = jnp.tril(jnp.ones((C, C), dtype=bool))
    strict = jnp.tril(jnp.ones((C, C), dtype=bool), -1)
    decay = jnp.exp(jnp.where(incl, G[..., :, None] - G[..., None, :], -jnp.inf))
    kk = jnp.einsum('bhtd,bhjd->bhtj', kh, kh)
    a_mat = jnp.where(strict, bt[..., :, None] * kk * decay, 0.0) + jnp.eye(C, dtype=F32)
    rhs = jnp.concatenate([vh * bt[..., None], kh * (bt * jnp.exp(G))[..., None]], -1)
    sol = lax.linalg.triangular_solve(a_mat, rhs, left_side=True, lower=True, unit_diagonal=True)
    u = sol[..., :DV_D] - jnp.einsum('bhtk,bhkv->bhtv', sol[..., DV_D:], S)
    qk = jnp.einsum('bhtd,bhjd->bhtj', qh, kh) * decay
    o = jnp.einsum('bhtk,bhkv->bhtv', qh * jnp.exp(G)[..., None], S) + jnp.einsum('bhtj,bhjv->bhtv', qk, u)
    g_last = G[..., -1]
    S_new = S * jnp.exp(g_last)[..., None, None] + jnp.einsum(
        'bhjk,bhjv->bhkv', kh * jnp.exp(g_last[..., None] - G)[..., None], u)
    return S_new, o.transpose(0, 2, 1, 3)


def gdn_output(o, gate, norm_w):
    B, T = o.shape[:2]
    o = o * lax.rsqrt(jnp.mean(o * o, -1, keepdims=True) + RMS_EPS) * norm_w.astype(F32)
    return (o.reshape(B, T, H_D * DV_D) * jax.nn.silu(gate.astype(F32))).astype(gate.dtype)


def even_project(x, w_in):
    B, T, _ = x.shape
    parts = split_cols(x @ w_in, EVEN_SIZES)
    qa, ka, va = (p.reshape(B, T, H_A, HEAD_DIM) for p in parts[:3])
    qb, kb, vb = (p.reshape(B, T, H_B, HEAD_DIM) for p in parts[3:])
    return qa, ka, va, qb, kb, vb


def even_prompt(x, w_in, w_out, rel_bias):
    B, T, _ = x.shape
    qa, ka, va, qb, kb, vb = even_project(x, w_in)
    pos = jnp.arange(T)
    nq = T // Q_BLOCK
    oa = lax.map(lambda a: stick_breaking(a[0], ka, va, a[1], pos),
                 (qa.reshape(B, nq, Q_BLOCK, H_A, HEAD_DIM).swapaxes(0, 1), pos.reshape(nq, Q_BLOCK)))
    oa = oa.swapaxes(0, 1).reshape(B, T, H_A * HEAD_DIM)
    ob = band_prompt(qb, kb, vb, rel_bias).reshape(B, T, H_B * HEAD_DIM)
    y = jnp.concatenate([oa, ob], -1) @ w_out
    keep = min(BAND_REACH, T)
    return y, ka, va, kb[:, T - keep:], vb[:, T - keep:]


def even_sample(x, cache_a_k, cache_a_v, cache_b_k, cache_b_v, w_in, w_out, rel_bias):
    B, n, _ = x.shape
    past, lb = cache_a_k.shape[1], cache_b_k.shape[1]
    qa, ka, va, qb, kb, vb = even_project(x, w_in)
    q_pos = past + jnp.arange(n)
    oa = stick_breaking(qa, jnp.concatenate([cache_a_k, ka], 1), jnp.concatenate([cache_a_v, va], 1),
                        q_pos, jnp.arange(past + n))
    k_pos = jnp.concatenate([jnp.arange(past - lb, past), q_pos])
    kband = jnp.concatenate([cache_b_k, kb], 1)[:, None]
    vband = jnp.concatenate([cache_b_v, vb], 1)[:, None]
    ob = band_core(qb[:, None], kband, vband, band_bias(rel_bias, q_pos, k_pos),
                   band_mask(q_pos, k_pos)[None])[:, 0]
    y = jnp.concatenate([oa.reshape(B, n, -1), ob.reshape(B, n, -1)], -1) @ w_out
    return y, ka, va, kb, vb


def odd_project(x, w_in, pos):
    B, T, _ = x.shape
    qc, kc, vc, qi, ki, wi, qkv_d, a_d, b_d, g_d = split_cols(x @ w_in, ODD_SIZES)
    qc = rope(qc.reshape(B, T, H_C, HEAD_DIM), pos)
    kc = rope(kc.reshape(B, T, KV_C, HEAD_DIM), pos)
    vc = vc.reshape(B, T, KV_C, HEAD_DIM)
    qi = rope(qi.reshape(B, T, H_IDX, D_IDX), pos)
    ki = rope(ki[:, :, None, :], pos)[:, :, 0]
    return qc, kc, vc, qi, ki, wi, qkv_d, a_d, b_d, g_d


def odd_prompt(x, w_in, w_out, conv_w, a_log, dt_bias, norm_w):
    B, T, _ = x.shape
    pos = jnp.arange(T)
    qc, kc, vc, qi, ki, wi, qkv_d, a_d, b_d, g_d = odd_project(x, w_in, pos)
    nq = T // Q_BLOCK
    topk = min(TOPK_MAX, T // 4)
    blk = lambda t: t.reshape((B, nq, Q_BLOCK) + t.shape[2:]).swapaxes(0, 1)
    oc = lax.map(lambda a: dsa_attend(a[0], a[1], a[2], a[3], kc, vc, ki, pos, topk),
                 (blk(qc), blk(qi), blk(wi), pos.reshape(nq, Q_BLOCK)))
    oc = oc.swapaxes(0, 1).reshape(B, T, H_C * HEAD_DIM)
    buf0 = jnp.zeros((B, CONV_W - 1, QKV_D), qkv_d.dtype)
    q, k, v, g, beta, conv_buf = gdn_inputs(qkv_d, buf0, a_d, b_d, conv_w, a_log, dt_bias)
    nc = T // CHUNK
    chk = lambda t: t.reshape((B, nc, CHUNK) + t.shape[2:]).swapaxes(0, 1)
    S0 = jnp.zeros((B, H_D, DK_D, DV_D), F32)
    S, od = lax.scan(lambda S, c: gdn_chunk(S, *c), S0, (chk(q), chk(k), chk(v), chk(g), chk(beta)))
    od = gdn_output(od.swapaxes(0, 1).reshape(B, T, H_D, DV_D), g_d, norm_w)
    y = jnp.concatenate([oc, od], -1) @ w_out
    return y, kc, vc, ki, S.astype(x.dtype), conv_buf


def odd_sample(x, cache_c_k, cache_c_v, cache_c_idx_k, state_d_ssm, state_d_conv,
               w_in, w_out, conv_w, a_log, dt_bias, norm_w):
    B, n, _ = x.shape
    past = cache_c_k.shape[1]
    q_pos = past + jnp.arange(n)
    qc, kc, vc, qi, ki, wi, qkv_d, a_d, b_d, g_d = odd_project(x, w_in, q_pos)
    topk = min(TOPK_MAX, (past + n) // 4)
    oc = dsa_attend(qc, qi, wi, q_pos, jnp.concatenate([cache_c_k, kc], 1),
                    jnp.concatenate([cache_c_v, vc], 1), jnp.concatenate([cache_c_idx_k, ki], 1),
                    jnp.arange(past + n), topk).reshape(B, n, H_C * HEAD_DIM)
    q, k, v, g, beta, conv_buf = gdn_inputs(qkv_d, state_d_conv, a_d, b_d, conv_w, a_log, dt_bias)
    S, od = gdn_chunk(state_d_ssm.astype(F32), q, k, v, g, beta)
    od = gdn_output(od, g_d, norm_w)
    y = jnp.concatenate([oc, od], -1) @ w_out
    return y, kc, vc, ki, S.astype(state_d_ssm.dtype), conv_buf


def moe(x, w_router, router_bias, w_gate, w_up, w_down):
    shape = x.shape
    xt = x.reshape(-1, shape[-1])
    n = xt.shape[0]
    aff = jax.nn.softmax((xt @ w_router).astype(F32), axis=-1)
    sel = aff + router_bias.astype(F32)
    group_score = jnp.max(sel.reshape(n, N_GROUPS, EXPERTS_PER_GROUP), -1)
    best = jnp.argmax(group_score, -1)
    in_group = (jnp.arange(N_EXPERTS) // EXPERTS_PER_GROUP)[None, :] == best[:, None]
    _, idx = lax.top_k(jnp.where(in_group, sel, NEG), TOP_K)
    g = jnp.take_along_axis(aff, idx, -1)
    g = g / jnp.sum(g, -1, keepdims=True)
    gates = jnp.sum(jax.nn.one_hot(idx, N_EXPERTS, dtype=F32) * g[..., None], axis=1)
    y = jnp.zeros(xt.shape, F32)
    for e in range(N_EXPERTS):
        h = jax.nn.silu(xt @ w_gate[e]) * (xt @ w_up[e])
        y = y + gates[:, e:e + 1] * (h @ w_down[e]).astype(F32)
    return y.astype(x.dtype).reshape(shape)


def setup_inputs(seed: int = 0) -> dict:
    key = jax.random.key(seed)
    ks = jax.random.split(key, 32)
    nrm = lambda k, shape, scale: jax.random.normal(k, shape, F32) * scale
    lb = min(BAND_REACH, PAST_LEN)
    dt = jnp.exp(jax.random.uniform(ks[20], (H_D,), F32, np.log(0.001), np.log(0.1)))
    return {
        'x_prompt': nrm(ks[0], (BATCH, SEQ, D_MODEL), 1.0),
        'x_sample': nrm(ks[1], (DEC_BATCH, DEC_SEQ, D_MODEL), 1.0),
        'cache_a_k': nrm(ks[2], (DEC_BATCH, PAST_LEN, H_A, HEAD_DIM), 1.0),
        'cache_a_v': nrm(ks[3], (DEC_BATCH, PAST_LEN, H_A, HEAD_DIM), 1.0),
        'cache_b_k': nrm(ks[4], (DEC_BATCH, lb, H_B, HEAD_DIM), 1.0),
        'cache_b_v': nrm(ks[5], (DEC_BATCH, lb, H_B, HEAD_DIM), 1.0),
        'cache_c_k': nrm(ks[6], (DEC_BATCH, PAST_LEN, KV_C, HEAD_DIM), 1.0),
        'cache_c_v': nrm(ks[7], (DEC_BATCH, PAST_LEN, KV_C, HEAD_DIM), 1.0),
        'cache_c_idx_k': nrm(ks[8], (DEC_BATCH, PAST_LEN, D_IDX), 1.0),
        'state_d_ssm': nrm(ks[9], (DEC_BATCH, H_D, DK_D, DV_D), 0.1),
        'state_d_conv': nrm(ks[10], (DEC_BATCH, CONV_W - 1, QKV_D), 1.0),
        'w_in_even': nrm(ks[11], (D_MODEL, sum(EVEN_SIZES)), D_MODEL ** -0.5),
        'w_out_even': nrm(ks[12], (EVEN_MIX, D_MODEL), EVEN_MIX ** -0.5 * DEEPNORM_BETA),
        'b_rel_bias': nrm(ks[13], (H_B, 2 * REL_CLIP + 1), 0.2),
        'w_in_odd': nrm(ks[14], (D_MODEL, sum(ODD_SIZES)), D_MODEL ** -0.5),
        'w_out_odd': nrm(ks[15], (ODD_MIX, D_MODEL), ODD_MIX ** -0.5 * DEEPNORM_BETA),
        'd_conv_w': nrm(ks[16], (CONV_W, QKV_D), CONV_W ** -0.5),
        'd_a_log': jnp.log(jax.random.uniform(ks[17], (H_D,), F32, 1.0, 16.0)),
        'd_dt_bias': dt + jnp.log(-jnp.expm1(-dt)),
        'd_norm_w': 1.0 + nrm(ks[18], (DV_D,), 0.02),
        'w_router': nrm(ks[19], (D_MODEL, N_EXPERTS), D_MODEL ** -0.5),
        'router_bias': nrm(ks[21], (N_EXPERTS,), 0.01),
        'moe_w_gate': nrm(ks[22], (DEPTH, N_EXPERTS, D_MODEL, D_EXPERT), D_MODEL ** -0.5),
        'moe_w_up': nrm(ks[23], (DEPTH, N_EXPERTS, D_MODEL, D_EXPERT), D_MODEL ** -0.5),
        'moe_w_down': nrm(ks[24], (DEPTH, N_EXPERTS, D_EXPERT, D_MODEL), D_EXPERT ** -0.5 * DEEPNORM_BETA),
        'ln_g': 1.0 + nrm(ks[25], (DEPTH, 2, D_MODEL), 0.02),
        'ln_b': nrm(ks[26], (DEPTH, 2, D_MODEL), 0.02),
    }


def reference(x_prompt, x_sample, cache_a_k, cache_a_v, cache_b_k, cache_b_v, cache_c_k, cache_c_v,
              cache_c_idx_k, state_d_ssm, state_d_conv, w_in_even, w_out_even, b_rel_bias, w_in_odd,
              w_out_odd, d_conv_w, d_a_log, d_dt_bias, d_norm_w, w_router, router_bias, moe_w_gate,
              moe_w_up, moe_w_down, ln_g, ln_b):
    xp, xs = x_prompt, x_sample
    for layer in range(DEPTH):
        if layer % 2 == 0:
            mp, a_k_p, a_v_p, b_k_p, b_v_p = even_prompt(xp, w_in_even, w_out_even, b_rel_bias)
            ms, a_k_s, a_v_s, b_k_s, b_v_s = even_sample(xs, cache_a_k, cache_a_v, cache_b_k, cache_b_v,
                                                         w_in_even, w_out_even, b_rel_bias)
        else:
            mp, c_k_p, c_v_p, c_i_p, d_s_p, d_c_p = odd_prompt(xp, w_in_odd, w_out_odd, d_conv_w,
                                                                d_a_log, d_dt_bias, d_norm_w)
            ms, c_k_s, c_v_s, c_i_s, d_s_s, d_c_s = odd_sample(xs, cache_c_k, cache_c_v, cache_c_idx_k,
                                                                state_d_ssm, state_d_conv, w_in_odd,
                                                                w_out_odd, d_conv_w, d_a_log, d_dt_bias,
                                                                d_norm_w)
        xp = layer_norm(DEEPNORM_ALPHA * xp + mp, ln_g[layer, 0], ln_b[layer, 0])
        xs = layer_norm(DEEPNORM_ALPHA * xs + ms, ln_g[layer, 0], ln_b[layer, 0])
        fp = moe(xp, w_router, router_bias, moe_w_gate[layer], moe_w_up[layer], moe_w_down[layer])
        fs = moe(xs, w_router, router_bias, moe_w_gate[layer], moe_w_up[layer], moe_w_down[layer])
        xp = layer_norm(DEEPNORM_ALPHA * xp + fp, ln_g[layer, 1], ln_b[layer, 1])
        xs = layer_norm(DEEPNORM_ALPHA * xs + fs, ln_g[layer, 1], ln_b[layer, 1])
    return (xp, xs, a_k_p, a_v_p, a_k_s, a_v_s, b_k_p, b_v_p, b_k_s, b_v_s,
            c_k_p, c_v_p, c_i_p, c_k_s, c_v_s, c_i_s, d_s_p, d_c_p, d_s_s, d_c_s)
```

```python
import functools

import jax
import jax.numpy as jnp
import numpy as np
from jax import lax
from jax.experimental import pallas as pl
from jax.experimental.pallas import tpu as pltpu

D_MODEL = 4096
DEPTH = 2
CHUNK = 64
HEAD_DIM = 128
Q_BLOCK = 128
NEG = -1e30
H_A = 16
H_B = 16
N_LEFT_CHUNKS = 8
BAND_REACH = N_LEFT_CHUNKS * CHUNK
REL_CLIP = 128
EVEN_SIZES = (H_A * HEAD_DIM,) * 3 + (H_B * HEAD_DIM,) * 3
H_C = 16
KV_C = 4
H_IDX = 16
D_IDX = 64
TOPK_MAX = 256
H_D = 16
DK_D = 128
DV_D = 128
CONV_W = 4
QKV_D = H_D * (2 * DK_D + DV_D)
ODD_SIZES = (H_C * HEAD_DIM, KV_C * HEAD_DIM, KV_C * HEAD_DIM, H_IDX * D_IDX, D_IDX, H_IDX,
             QKV_D, H_D, H_D, H_D * DV_D)
N_EXPERTS = 16
N_GROUPS = 4
EXPERTS_PER_GROUP = N_EXPERTS // N_GROUPS
TOP_K = 2
ROPE_THETA = 10000.0
LN_EPS = 1e-5
RMS_EPS = 1e-6
DEEPNORM_ALPHA = (2 * DEPTH) ** 0.25

F32 = jnp.float32
BF16 = jnp.bfloat16

VMEM_LIMIT_BYTES = 56 * 1024 * 1024


def _matmul_kernel(x_ref, w_ref, o_ref):
    o_ref[...] = jnp.dot(x_ref[...], w_ref[...], preferred_element_type=F32)


def matmul(x, w, tm=512, tn=1024):
    m, k = x.shape
    n = w.shape[1]
    tm = min(tm, m)
    tn = min(tn, n)
    assert m % tm == 0 and n % tn == 0
    return pl.pallas_call(
        _matmul_kernel,
        grid=(n // tn, m // tm),
        in_specs=[pl.BlockSpec((tm, k), lambda j, i: (i, 0)),
                  pl.BlockSpec((k, tn), lambda j, i: (0, j))],
        out_specs=pl.BlockSpec((tm, tn), lambda j, i: (i, j)),
        out_shape=jax.ShapeDtypeStruct((m, n), F32),
        compiler_params=pltpu.CompilerParams(
            dimension_semantics=("arbitrary", "arbitrary"), vmem_limit_bytes=VMEM_LIMIT_BYTES),
        name="proj_matmul",
    )(x, w)


def project(x, w_bf16):
    b, t, d = x.shape
    n = w_bf16.shape[1]
    n_pad = (-n) % 1024
    if n_pad:
        w_bf16 = jnp.pad(w_bf16, ((0, 0), (0, n_pad)))
    y = matmul(x.reshape(b * t, d).astype(BF16), w_bf16)
    return y[:, :n].reshape(b, t, n)


def split_cols(h, sizes):
    cuts, acc = [], 0
    for s in sizes[:-1]:
        acc += s
        cuts.append(acc)
    return jnp.split(h, cuts, axis=-1)


def layer_norm(x, g, b):
    xf = x.astype(F32)
    mu = jnp.mean(xf, -1, keepdims=True)
    var = jnp.mean(jnp.square(xf - mu), -1, keepdims=True)
    return ((xf - mu) * lax.rsqrt(var + LN_EPS) * g.astype(F32) + b.astype(F32)).astype(x.dtype)


def rope(x, pos):
    half = x.shape[-1] // 2
    inv = ROPE_THETA ** (-jnp.arange(half, dtype=F32) / half)
    ang = pos.astype(F32)[:, None] * inv[None, :]
    cos, sin = jnp.cos(ang)[None, :, None, :], jnp.sin(ang)[None, :, None, :]
    xf = x.astype(F32)
    x1, x2 = xf[..., :half], xf[..., half:]
    return jnp.concatenate([x1 * cos - x2 * sin, x2 * cos + x1 * sin], -1).astype(x.dtype)


def stick_breaking(q, k, v, q_pos, k_pos):
    z = jnp.einsum('bqhd,bkhd->bhqk', q.astype(F32), k.astype(F32)) * (q.shape[-1] ** -0.5)
    earlier = (k_pos[None, :] < q_pos[:, None])[None, None]
    log_1mb = jnp.where(earlier, jax.nn.log_sigmoid(-z), 0.0)
    between = lax.cumsum(log_1mb, axis=3, reverse=True) - log_1mb
    w = jnp.where(earlier, jnp.exp(jax.nn.log_sigmoid(z) + between), 0.0)
    return jnp.einsum('bhqk,bkhd->bqhd', w, v.astype(F32)).astype(v.dtype)


def band_mask(q_pos, k_pos):
    qc = q_pos[:, None] // CHUNK
    kc = k_pos[None, :] // CHUNK
    return (kc <= qc) & (kc >= qc - N_LEFT_CHUNKS) & (k_pos[None, :] >= 0)


def band_bias(rel_bias, q_pos, k_pos):
    rel = jnp.clip(q_pos[:, None] - k_pos[None, :], -REL_CLIP, REL_CLIP) + REL_CLIP
    return rel_bias[:, rel]


def band_core(q, kb, vb, bias, valid):
    s = jnp.einsum('bnqhd,bnkhd->bnhqk', q.astype(F32), kb.astype(F32)) * (q.shape[-1] ** -0.5)
    s = jnp.where(valid[None, :, None], s + bias.astype(F32)[None, None], NEG)
    p = jax.nn.softmax(s, axis=-1)
    return jnp.einsum('bnhqk,bnkhd->bnqhd', p, vb.astype(F32)).astype(q.dtype)


def band_prompt(q, k, v, rel_bias):
    B, T, H, d = q.shape
    nc = T // CHUNK
    nb = N_LEFT_CHUNKS + 1
    pad = jnp.zeros((B, N_LEFT_CHUNKS * CHUNK, H, d), k.dtype)
    kp = jnp.concatenate([pad, k], 1).reshape(B, nc + N_LEFT_CHUNKS, CHUNK, H, d)
    vp = jnp.concatenate([pad, v], 1).reshape(B, nc + N_LEFT_CHUNKS, CHUNK, H, d)
    idx = jnp.arange(nc)[:, None] + jnp.arange(nb)[None, :]
    kband = kp[:, idx].reshape(B, nc, nb * CHUNK, H, d)
    vband = vp[:, idx].reshape(B, nc, nb * CHUNK, H, d)
    q_pos = jnp.arange(T).reshape(nc, CHUNK)
    k_pos = ((idx - N_LEFT_CHUNKS)[:, :, None] * CHUNK + jnp.arange(CHUNK)).reshape(nc, nb * CHUNK)
    valid = jax.vmap(band_mask)(q_pos, k_pos)
    bias = band_bias(rel_bias, q_pos[0], k_pos[0])
    out = band_core(q.reshape(B, nc, CHUNK, H, d), kband, vband, bias, valid)
    return out.reshape(B, T, H, d)


def dsa_attend(qc, qi, wi, q_pos, kc, vc, ki, k_pos, topk):
    B, Q = qc.shape[:2]
    logits = jnp.einsum('bqhe,bse->bqhs', qi.astype(F32), ki.astype(F32))
    score = jnp.einsum('bqh,bqhs->bqs', wi.astype(F32), jax.nn.relu(logits))
    adm = jnp.broadcast_to(((k_pos[None, :] // CHUNK) <= (q_pos[:, None] // CHUNK))[None], score.shape)
    _, sel = lax.top_k(jnp.where(adm, score, NEG), topk)
    sel_ok = jnp.take_along_axis(adm, sel, axis=-1)
    gather = jax.vmap(lambda rows, i: rows[i])
    ks, vs = gather(kc, sel), gather(vc, sel)
    qg = qc.reshape(B, Q, KV_C, H_C // KV_C, HEAD_DIM).astype(F32)
    s = jnp.einsum('bqgrd,bqkgd->bqgrk', qg, ks.astype(F32)) * (HEAD_DIM ** -0.5)
    s = jnp.where(sel_ok[:, :, None, None, :], s, NEG)
    p = jax.nn.softmax(s, axis=-1)
    o = jnp.einsum('bqgrk,bqkgd->bqgrd', p, vs.astype(F32))
    return o.reshape(B, Q, H_C, HEAD_DIM).astype(qc.dtype)


def causal_conv(raw, buf, conv_w):
    xp = jnp.concatenate([buf, raw], 1)
    out = lax.conv_general_dilated(xp, conv_w[:, None, :], window_strides=(1,), padding='VALID',
                                   dimension_numbers=('NWC', 'WIO', 'NWC'),
                                   feature_group_count=raw.shape[-1])
    return jax.nn.silu(out), xp[:, xp.shape[1] - (CONV_W - 1):]


def l2norm(t):
    return t * lax.rsqrt(jnp.sum(t * t, -1, keepdims=True) + RMS_EPS)


def gdn_inputs(qkv_raw, buf, a_raw, b_raw, conv_w, a_log, dt_bias):
    qkv, new_buf = causal_conv(qkv_raw, buf, conv_w)
    B, T, _ = qkv.shape
    q, k, v = split_cols(qkv, (H_D * DK_D, H_D * DK_D, H_D * DV_D))
    q = l2norm(q.reshape(B, T, H_D, DK_D).astype(F32)) * (DK_D ** -0.5)
    k = l2norm(k.reshape(B, T, H_D, DK_D).astype(F32))
    v = v.reshape(B, T, H_D, DV_D).astype(F32)
    g = -jnp.exp(a_log.astype(F32)) * jax.nn.softplus(a_raw.astype(F32) + dt_bias.astype(F32))
    beta = jax.nn.sigmoid(b_raw.astype(F32))
    return q, k, v, g, beta, new_buf


def gdn_chunk(S, q, k, v, g, beta):
    C = q.shape[1]
    qh, kh, vh = q.transpose(0, 2, 1, 3), k.transpose(0, 2, 1, 3), v.transpose(0, 2, 1, 3)
    G = jnp.cumsum(g, axis=1).transpose(0, 2, 1)
    bt = beta.transpose(0, 2, 1)
    incl = jnp.tril(jnp.ones((C, C), dtype=bool))
    strict = jnp.tril(jnp.ones((C, C), dtype=bool), -1)
    decay = jnp.exp(jnp.where(incl, G[..., :, None] - G[..., None, :], -jnp.inf))
    kk = jnp.einsum('bhtd,bhjd->bhtj', kh, kh)
    a_mat = jnp.where(strict, bt[..., :, None] * kk * decay, 0.0) + jnp.eye(C, dtype=F32)
    rhs = jnp.concatenate([vh * bt[..., None], kh * (bt * jnp.exp(G))[..., None]], -1)
    sol = lax.linalg.triangular_solve(a_mat, rhs, left_side=True, lower=True, unit_diagonal=True)
    u = sol[..., :DV_D] - jnp.einsum('bhtk,bhkv->bhtv', sol[..., DV_D:], S)
    qk = jnp.einsum('bhtd,bhjd->bhtj', qh, kh) * decay
    o = jnp.einsum('bhtk,bhkv->bhtv', qh * jnp.exp(G)[..., None], S) + jnp.einsum('bhtj,bhjv->bhtv', qk, u)
    g_last = G[..., -1]
    S_new = S * jnp.exp(g_last)[..., None, None] + jnp.einsum(
        'bhjk,bhjv->bhkv', kh * jnp.exp(g_last[..., None] - G)[..., None], u)
    return S_new, o.transpose(0, 2, 1, 3)


def gdn_output(o, gate, norm_w):
    B, T = o.shape[:2]
    o = o * lax.rsqrt(jnp.mean(o * o, -1, keepdims=True) + RMS_EPS) * norm_w.astype(F32)
    return (o.reshape(B, T, H_D * DV_D) * jax.nn.silu(gate.astype(F32))).astype(gate.dtype)


def even_project(x, w_in):
    B, T, _ = x.shape
    parts = split_cols(project(x, w_in), EVEN_SIZES)
    qa, ka, va = (p.reshape(B, T, H_A, HEAD_DIM) for p in parts[:3])
    qb, kb, vb = (p.reshape(B, T, H_B, HEAD_DIM) for p in parts[3:])
    return qa, ka, va, qb, kb, vb


def even_prompt(x, w_in, w_out, rel_bias):
    B, T, _ = x.shape
    qa, ka, va, qb, kb, vb = even_project(x, w_in)
    pos = jnp.arange(T)
    nq = T // Q_BLOCK
    oa = lax.map(lambda a: stick_breaking(a[0], ka, va, a[1], pos),
                 (qa.reshape(B, nq, Q_BLOCK, H_A, HEAD_DIM).swapaxes(0, 1), pos.reshape(nq, Q_BLOCK)))
    oa = oa.swapaxes(0, 1).reshape(B, T, H_A * HEAD_DIM)
    ob = band_prompt(qb, kb, vb, rel_bias).reshape(B, T, H_B * HEAD_DIM)
    y = project(jnp.concatenate([oa, ob], -1), w_out)
    keep = min(BAND_REACH, T)
    return y, ka, va, kb[:, T - keep:], vb[:, T - keep:]


def even_sample(x, cache_a_k, cache_a_v, cache_b_k, cache_b_v, w_in, w_out, rel_bias):
    B, n, _ = x.shape
    past, lb = cache_a_k.shape[1], cache_b_k.shape[1]
    qa, ka, va, qb, kb, vb = even_project(x, w_in)
    q_pos = past + jnp.arange(n)
    oa = stick_breaking(qa, jnp.concatenate([cache_a_k, ka], 1), jnp.concatenate([cache_a_v, va], 1),
                        q_pos, jnp.arange(past + n))
    k_pos = jnp.concatenate([jnp.arange(past - lb, past), q_pos])
    kband = jnp.concatenate([cache_b_k, kb], 1)[:, None]
    vband = jnp.concatenate([cache_b_v, vb], 1)[:, None]
    ob = band_core(qb[:, None], kband, vband, band_bias(rel_bias, q_pos, k_pos),
                   band_mask(q_pos, k_pos)[None])[:, 0]
    y = project(jnp.concatenate([oa.reshape(B, n, -1), ob.reshape(B, n, -1)], -1), w_out)
    return y, ka, va, kb, vb


def odd_project(x, w_in, pos):
    B, T, _ = x.shape
    qc, kc, vc, qi, ki, wi, qkv_d, a_d, b_d, g_d = split_cols(project(x, w_in), ODD_SIZES)
    qc = rope(qc.reshape(B, T, H_C, HEAD_DIM), pos)
    kc = rope(kc.reshape(B, T, KV_C, HEAD_DIM), pos)
    vc = vc.reshape(B, T, KV_C, HEAD_DIM)
    qi = rope(qi.reshape(B, T, H_IDX, D_IDX), pos)
    ki = rope(ki[:, :, None, :], pos)[:, :, 0]
    return qc, kc, vc, qi, ki, wi, qkv_d, a_d, b_d, g_d


def odd_prompt(x, w_in, w_out, conv_w, a_log, dt_bias, norm_w):
    B, T, _ = x.shape
    pos = jnp.arange(T)
    qc, kc, vc, qi, ki, wi, qkv_d, a_d, b_d, g_d = odd_project(x, w_in, pos)
    nq = T // Q_BLOCK
    topk = min(TOPK_MAX, T // 4)
    blk = lambda t: t.reshape((B, nq, Q_BLOCK) + t.shape[2:]).swapaxes(0, 1)
    oc = lax.map(lambda a: dsa_attend(a[0], a[1], a[2], a[3], kc, vc, ki, pos, topk),
                 (blk(qc), blk(qi), blk(wi), pos.reshape(nq, Q_BLOCK)))
    oc = oc.swapaxes(0, 1).reshape(B, T, H_C * HEAD_DIM)
    buf0 = jnp.zeros((B, CONV_W - 1, QKV_D), qkv_d.dtype)
    q, k, v, g, beta, conv_buf = gdn_inputs(qkv_d, buf0, a_d, b_d, conv_w, a_log, dt_bias)
    nc = T // CHUNK
    chk = lambda t: t.reshape((B, nc, CHUNK) + t.shape[2:]).swapaxes(0, 1)
    S0 = jnp.zeros((B, H_D, DK_D, DV_D), F32)
    S, od = lax.scan(lambda S, c: gdn_chunk(S, *c), S0, (chk(q), chk(k), chk(v), chk(g), chk(beta)))
    od = gdn_output(od.swapaxes(0, 1).reshape(B, T, H_D, DV_D), g_d, norm_w)
    y = project(jnp.concatenate([oc, od], -1), w_out)
    return y, kc, vc, ki, S.astype(x.dtype), conv_buf


def odd_sample(x, cache_c_k, cache_c_v, cache_c_idx_k, state_d_ssm, state_d_conv,
               w_in, w_out, conv_w, a_log, dt_bias, norm_w):
    B, n, _ = x.shape
    past = cache_c_k.shape[1]
    q_pos = past + jnp.arange(n)
    qc, kc, vc, qi, ki, wi, qkv_d, a_d, b_d, g_d = odd_project(x, w_in, q_pos)
    topk = min(TOPK_MAX, (past + n) // 4)
    oc = dsa_attend(qc, qi, wi, q_pos, jnp.concatenate([cache_c_k, kc], 1),
                    jnp.concatenate([cache_c_v, vc], 1), jnp.concatenate([cache_c_idx_k, ki], 1),
                    jnp.arange(past + n), topk).reshape(B, n, H_C * HEAD_DIM)
    q, k, v, g, beta, conv_buf = gdn_inputs(qkv_d, state_d_conv, a_d, b_d, conv_w, a_log, dt_bias)
    S, od = gdn_chunk(state_d_ssm.astype(F32), q, k, v, g, beta)
    od = gdn_output(od, g_d, norm_w)
    y = project(jnp.concatenate([oc, od], -1), w_out)
    return y, kc, vc, ki, S.astype(state_d_ssm.dtype), conv_buf


def moe(x, w_router, router_bias, w_gate, w_up, w_down):
    shape = x.shape
    xt = x.reshape(-1, shape[-1])
    n = xt.shape[0]
    aff = jax.nn.softmax((xt @ w_router).astype(F32), axis=-1)
    sel = aff + router_bias.astype(F32)
    group_score = jnp.max(sel.reshape(n, N_GROUPS, EXPERTS_PER_GROUP), -1)
    best = jnp.argmax(group_score, -1)
    in_group = (jnp.arange(N_EXPERTS) // EXPERTS_PER_GROUP)[None, :] == best[:, None]
    _, idx = lax.top_k(jnp.where(in_group, sel, NEG), TOP_K)
    g = jnp.take_along_axis(aff, idx, -1)
    g = g / jnp.sum(g, -1, keepdims=True)
    gates = jnp.sum(jax.nn.one_hot(idx, N_EXPERTS, dtype=F32) * g[..., None], axis=1)
    y = jnp.zeros(xt.shape, F32)
    for e in range(N_EXPERTS):
        h = jax.nn.silu(xt @ w_gate[e]) * (xt @ w_up[e])
        y = y + gates[:, e:e + 1] * (h @ w_down[e]).astype(F32)
    return y.astype(x.dtype).reshape(shape)


def kernel(x_prompt, x_sample, cache_a_k, cache_a_v, cache_b_k, cache_b_v, cache_c_k, cache_c_v,
           cache_c_idx_k, state_d_ssm, state_d_conv, w_in_even, w_out_even, b_rel_bias, w_in_odd,
           w_out_odd, d_conv_w, d_a_log, d_dt_bias, d_norm_w, w_router, router_bias, moe_w_gate,
           moe_w_up, moe_w_down, ln_g, ln_b):
    xp, xs = x_prompt, x_sample
    w_in_even = w_in_even.astype(BF16)
    w_out_even = w_out_even.astype(BF16)
    w_in_odd = w_in_odd.astype(BF16)
    w_out_odd = w_out_odd.astype(BF16)
    for layer in range(DEPTH):
        if layer % 2 == 0:
            mp, a_k_p, a_v_p, b_k_p, b_v_p = even_prompt(xp, w_in_even, w_out_even, b_rel_bias)
            ms, a_k_s, a_v_s, b_k_s, b_v_s = even_sample(xs, cache_a_k, cache_a_v, cache_b_k, cache_b_v,
                                                         w_in_even, w_out_even, b_rel_bias)
        else:
            mp, c_k_p, c_v_p, c_i_p, d_s_p, d_c_p = odd_prompt(xp, w_in_odd, w_out_odd, d_conv_w,
                                                                d_a_log, d_dt_bias, d_norm_w)
            ms, c_k_s, c_v_s, c_i_s, d_s_s, d_c_s = odd_sample(xs, cache_c_k, cache_c_v, cache_c_idx_k,
                                                                state_d_ssm, state_d_conv, w_in_odd,
                                                                w_out_odd, d_conv_w, d_a_log, d_dt_bias,
                                                                d_norm_w)
        xp = layer_norm(DEEPNORM_ALPHA * xp + mp, ln_g[layer, 0], ln_b[layer, 0])
        xs = layer_norm(DEEPNORM_ALPHA * xs + ms, ln_g[layer, 0], ln_b[layer, 0])
        fp = moe(xp, w_router, router_bias, moe_w_gate[layer], moe_w_up[layer], moe_w_down[layer])
        fs = moe(xs, w_router, router_bias, moe_w_gate[layer], moe_w_up[layer], moe_w_down[layer])
        xp = layer_norm(DEEPNORM_ALPHA * xp + fp, ln_g[layer, 1], ln_b[layer, 1])
        xs = layer_norm(DEEPNORM_ALPHA * xs + fs, ln_g[layer, 1], ln_b[layer, 1])
    return (xp, xs, a_k_p, a_v_p, a_k_s, a_v_s, b_k_p, b_v_p, b_k_s, b_v_s,
            c_k_p, c_v_p, c_i_p, c_k_s, c_v_s, c_i_s, d_s_p, d_c_p, d_s_s, d_c_s)
```

```python
import functools

import jax
import jax.numpy as jnp
import numpy as np
from jax import lax
from jax.experimental import pallas as pl
from jax.experimental.pallas import tpu as pltpu

D_MODEL = 4096
DEPTH = 2
CHUNK = 64
HEAD_DIM = 128
Q_BLOCK = 128
NEG = -1e30
H_A = 16
H_B = 16
N_LEFT_CHUNKS = 8
BAND_REACH = N_LEFT_CHUNKS * CHUNK
REL_CLIP = 128
EVEN_SIZES = (H_A * HEAD_DIM,) * 3 + (H_B * HEAD_DIM,) * 3
H_C = 16
KV_C = 4
H_IDX = 16
D_IDX = 64
TOPK_MAX = 256
H_D = 16
DK_D = 128
DV_D = 128
CONV_W = 4
QKV_D = H_D * (2 * DK_D + DV_D)
ODD_SIZES = (H_C * HEAD_DIM, KV_C * HEAD_DIM, KV_C * HEAD_DIM, H_IDX * D_IDX, D_IDX, H_IDX,
             QKV_D, H_D, H_D, H_D * DV_D)
N_EXPERTS = 16
N_GROUPS = 4
EXPERTS_PER_GROUP = N_EXPERTS // N_GROUPS
TOP_K = 2
D_EXPERT = 1024
ROPE_THETA = 10000.0
LN_EPS = 1e-5
RMS_EPS = 1e-6
DEEPNORM_ALPHA = (2 * DEPTH) ** 0.25

F32 = jnp.float32
BF16 = jnp.bfloat16
I32 = jnp.int32

LANES = 128
VMEM_LIMIT_BYTES = 56 * 1024 * 1024


def _params(*sem):
    return pltpu.CompilerParams(dimension_semantics=sem, vmem_limit_bytes=VMEM_LIMIT_BYTES)


def _cast_rows(src_ref, dst_ref, dst_off=0, chunk=256):
    rows = src_ref.shape[0]
    assert rows % chunk == 0 and dst_off % chunk == 0

    def body(c, carry):
        r = pl.multiple_of(c * chunk, chunk)
        dst_ref[pl.ds(dst_off + r, chunk), :] = src_ref[pl.ds(r, chunk), :].astype(dst_ref.dtype)
        return carry
    lax.fori_loop(0, rows // chunk, body, 0)


def _matmul_kernel(x_ref, w_ref, o_ref):
    o_ref[...] = jnp.dot(x_ref[...], w_ref[...], preferred_element_type=F32)


def matmul(x, w, tm=512, tn=1024):
    m, k = x.shape
    n = w.shape[1]
    tm = min(tm, m)
    tn = min(tn, n)
    assert m % tm == 0 and n % tn == 0
    return pl.pallas_call(
        _matmul_kernel,
        grid=(n // tn, m // tm),
        in_specs=[pl.BlockSpec((tm, k), lambda j, i: (i, 0)),
                  pl.BlockSpec((k, tn), lambda j, i: (0, j))],
        out_specs=pl.BlockSpec((tm, tn), lambda j, i: (i, j)),
        out_shape=jax.ShapeDtypeStruct((m, n), F32),
        compiler_params=_params("arbitrary", "arbitrary"),
        name="proj_matmul",
    )(x, w)


def _matmul2_kernel(xa_ref, xb_ref, w_ref, o_ref):
    ka = xa_ref.shape[1]
    o_ref[...] = (jnp.dot(xa_ref[...], w_ref[:ka, :], preferred_element_type=F32)
                  + jnp.dot(xb_ref[...], w_ref[ka:, :], preferred_element_type=F32))


def matmul_concat(xa, xb, w, tm=512, tn=1024):
    m, ka = xa.shape
    kb = xb.shape[1]
    n = w.shape[1]
    tm = min(tm, m)
    assert m % tm == 0 and n % tn == 0
    return pl.pallas_call(
        _matmul2_kernel,
        grid=(n // tn, m // tm),
        in_specs=[pl.BlockSpec((tm, ka), lambda j, i: (i, 0)),
                  pl.BlockSpec((tm, kb), lambda j, i: (i, 0)),
                  pl.BlockSpec((ka + kb, tn), lambda j, i: (0, j))],
        out_specs=pl.BlockSpec((tm, tn), lambda j, i: (i, j)),
        out_shape=jax.ShapeDtypeStruct((m, n), F32),
        compiler_params=_params("arbitrary", "arbitrary"),
        name="out_matmul",
    )(xa, xb, w)


def _sb_kernel(q_ref, k_ref, v_ref, u_ref, o_ref, kb_ref, vb_ref, *, bq, bk, q_off, tk, scale):
    qi = pl.program_id(2)

    @pl.when(qi == 0)
    def _():
        _cast_rows(k_ref.at[0], kb_ref)
        _cast_rows(v_ref.at[0], vb_ref)

    q = q_ref[0].astype(BF16)
    r0 = qi * bq
    rows = q_off + r0 + lax.broadcasted_iota(I32, (bq, bk), 0)
    cols = lax.broadcasted_iota(I32, (bq, bk), 1)
    n_kb = jnp.minimum((q_off + r0 + bq + bk - 1) // bk, tk // bk)
    tri = u_ref[...]

    def body(it, carry):
        acc, run = carry
        start = pl.multiple_of((n_kb - 1 - it) * bk, bk)
        kblk = kb_ref[pl.ds(start, bk), :]
        vblk = vb_ref[pl.ds(start, bk), :]
        z = lax.dot_general(q, kblk, (((1,), (1,)), ((), ())), preferred_element_type=F32) * scale
        t = jnp.log(1.0 + jnp.exp(-jnp.abs(z)))
        earlier = (cols + start) < rows
        log_1mb = jnp.where(earlier, -(jnp.maximum(z, 0.0) + t), 0.0)
        log_b = jnp.minimum(z, 0.0) - t
        hi = log_1mb.astype(BF16)
        lo = (log_1mb - hi.astype(F32)).astype(BF16)
        between = (jnp.dot(hi, tri, preferred_element_type=F32)
                   + jnp.dot(lo, tri, preferred_element_type=F32) + run)
        w = jnp.where(earlier, jnp.exp(log_b + between), 0.0)
        acc = acc + jnp.dot(w.astype(BF16), vblk, preferred_element_type=F32)
        run = run + jnp.sum(log_1mb, axis=1, keepdims=True)
        return acc, run

    acc, _ = lax.fori_loop(0, n_kb, body, (jnp.zeros((bq, HEAD_DIM), F32), jnp.zeros((bq, 1), F32)))
    o_ref[0] = acc.astype(o_ref.dtype)


def stick_breaking_attention(q_arr, k_arr, v_arr, q_cb, k_cb, v_cb, n_heads, q_off, bq, bk):
    b, tq, _ = q_arr.shape
    tk = k_arr.shape[1]
    assert tq % bq == 0 and tk % bk == 0
    tri = jnp.asarray(np.tril(np.ones((bk, bk), np.float32), -1), BF16)
    kern = functools.partial(_sb_kernel, bq=bq, bk=bk, q_off=q_off, tk=tk, scale=HEAD_DIM ** -0.5)
    return pl.pallas_call(
        kern,
        grid=(b, n_heads, tq // bq),
        in_specs=[pl.BlockSpec((1, bq, HEAD_DIM), lambda b_, h, i: (b_, i, q_cb + h)),
                  pl.BlockSpec((1, tk, HEAD_DIM), lambda b_, h, i: (b_, 0, k_cb + h)),
                  pl.BlockSpec((1, tk, HEAD_DIM), lambda b_, h, i: (b_, 0, v_cb + h)),
                  pl.BlockSpec((bk, bk), lambda b_, h, i: (0, 0))],
        out_specs=pl.BlockSpec((1, bq, HEAD_DIM), lambda b_, h, i: (b_, i, h)),
        out_shape=jax.ShapeDtypeStruct((b, tq, n_heads * HEAD_DIM), BF16),
        scratch_shapes=[pltpu.VMEM((tk, HEAD_DIM), BF16), pltpu.VMEM((tk, HEAD_DIM), BF16)],
        compiler_params=_params("arbitrary", "arbitrary", "arbitrary"),
        name="stick_breaking",
    )(q_arr, k_arr, v_arr, tri)


def _softmax_pv(s, v):
    m = jnp.max(s, axis=1, keepdims=True)
    e = jnp.exp(s - m)
    p = e / jnp.sum(e, axis=1, keepdims=True)
    return jnp.dot(p.astype(BF16), v, preferred_element_type=F32)


def _band_prompt_kernel(q_ref, k_ref, v_ref, bias_ref, o_ref, kb_ref, vb_ref, *, bq, reach, scale):
    qi = pl.program_id(2)

    @pl.when(qi == 0)
    def _():
        kb_ref[:reach, :] = jnp.zeros((reach, HEAD_DIM), BF16)
        vb_ref[:reach, :] = jnp.zeros((reach, HEAD_DIM), BF16)
        _cast_rows(k_ref.at[0], kb_ref, dst_off=reach)
        _cast_rows(v_ref.at[0], vb_ref, dst_off=reach)

    win = reach + bq
    start = pl.multiple_of(qi * bq, bq)
    q = q_ref[0].astype(BF16)
    kw = kb_ref[pl.ds(start, win), :]
    vw = vb_ref[pl.ds(start, win), :]
    s = lax.dot_general(q, kw, (((1,), (1,)), ((), ())), preferred_element_type=F32) * scale
    k_pos = start - reach + lax.broadcasted_iota(I32, (bq, win), 1)
    s = jnp.where(k_pos >= 0, s + bias_ref[0], NEG)
    o_ref[0] = _softmax_pv(s, vw).astype(o_ref.dtype)


def band_attention_prompt(proj, q_cb, k_cb, v_cb, rel_bias, bq=256):
    b, t, _ = proj.shape
    assert bq % CHUNK == 0 and t % bq == 0
    win = BAND_REACH + bq
    i = np.arange(bq)[:, None]
    j = np.arange(win)[None, :]
    rel = np.clip(BAND_REACH + i - j, -REL_CLIP, REL_CLIP) + REL_CLIP
    qc, kc = i // CHUNK, j // CHUNK - N_LEFT_CHUNKS
    in_band = (kc <= qc) & (kc >= qc - N_LEFT_CHUNKS)
    bias = jnp.where(jnp.asarray(in_band)[None], rel_bias.astype(F32)[:, rel], NEG)
    kern = functools.partial(_band_prompt_kernel, bq=bq, reach=BAND_REACH, scale=HEAD_DIM ** -0.5)
    return pl.pallas_call(
        kern,
        grid=(b, H_B, t // bq),
        in_specs=[pl.BlockSpec((1, bq, HEAD_DIM), lambda b_, h, i_: (b_, i_, q_cb + h)),
                  pl.BlockSpec((1, t, HEAD_DIM), lambda b_, h, i_: (b_, 0, k_cb + h)),
                  pl.BlockSpec((1, t, HEAD_DIM), lambda b_, h, i_: (b_, 0, v_cb + h)),
                  pl.BlockSpec((1, bq, win), lambda b_, h, i_: (h, 0, 0))],
        out_specs=pl.BlockSpec((1, bq, HEAD_DIM), lambda b_, h, i_: (b_, i_, h)),
        out_shape=jax.ShapeDtypeStruct((b, t, H_B * HEAD_DIM), BF16),
        scratch_shapes=[pltpu.VMEM((BAND_REACH + t, HEAD_DIM), BF16),
                        pltpu.VMEM((BAND_REACH + t, HEAD_DIM), BF16)],
        compiler_params=_params("arbitrary", "arbitrary", "arbitrary"),
        name="band_prompt",
    )(proj, proj, proj, bias)


def _band_core_kernel(q_ref, k_ref, v_ref, bias_ref, o_ref, *, scale):
    q = q_ref[0].astype(BF16)
    k = k_ref[0].astype(BF16)
    v = v_ref[0].astype(BF16)
    s = lax.dot_general(q, k, (((1,), (1,)), ((), ())), preferred_element_type=F32) * scale
    o_ref[0] = _softmax_pv(s + bias_ref[0], v).astype(o_ref.dtype)


def band_attention_core(q_arr, k_arr, v_arr, q_cb, bias):
    b, cq, _ = q_arr.shape
    ck = k_arr.shape[1]
    h_n = bias.shape[0]
    kern = functools.partial(_band_core_kernel, scale=HEAD_DIM ** -0.5)
    return pl.pallas_call(
        kern,
        grid=(b, h_n),
        in_specs=[pl.BlockSpec((1, cq, HEAD_DIM), lambda b_, h: (b_, 0, q_cb + h)),
                  pl.BlockSpec((1, ck, HEAD_DIM), lambda b_, h: (b_, 0, h)),
                  pl.BlockSpec((1, ck, HEAD_DIM), lambda b_, h: (b_, 0, h)),
                  pl.BlockSpec((1, cq, ck), lambda b_, h: (h, 0, 0))],
        out_specs=pl.BlockSpec((1, cq, HEAD_DIM), lambda b_, h: (b_, 0, h)),
        out_shape=jax.ShapeDtypeStruct((b, cq, h_n * HEAD_DIM), BF16),
        compiler_params=_params("arbitrary", "arbitrary"),
        name="band_core",
    )(q_arr, k_arr, v_arr, bias)


def _layer_norm_rows(h, g, b):
    mu = jnp.mean(h, axis=-1, keepdims=True)
    d = h - mu
    var = jnp.mean(d * d, axis=-1, keepdims=True)
    return d * lax.rsqrt(var + LN_EPS) * g + b


def _ln_router_kernel(x_ref, m_ref, g_ref, b_ref, wr_ref, rb_ref, x1_ref, gate_ref, idx_ref):
    y = _layer_norm_rows(DEEPNORM_ALPHA * x_ref[...] + m_ref[...], g_ref[...], b_ref[...])
    x1_ref[...] = y
    tb = y.shape[0]
    logits = jnp.dot(y.astype(BF16), wr_ref[...], preferred_element_type=F32)
    lane = lax.broadcasted_iota(I32, (tb, LANES), 1)
    lane_f = lane.astype(F32)
    real = lane < N_EXPERTS
    lg = jnp.where(real, logits, -jnp.inf)
    e = jnp.exp(lg - jnp.max(lg, axis=1, keepdims=True))
    aff = e / jnp.sum(e, axis=1, keepdims=True)
    sel = jnp.where(real, aff + rb_ref[...], -jnp.inf)
    group = lane // EXPERTS_PER_GROUP
    best = jnp.zeros((tb, 1), I32)
    best_v = jnp.max(jnp.where(group == 0, sel, -jnp.inf), axis=1, keepdims=True)
    for gi in range(1, N_GROUPS):
        gv = jnp.max(jnp.where(group == gi, sel, -jnp.inf), axis=1, keepdims=True)
        upd = gv > best_v
        best = jnp.where(upd, gi, best)
        best_v = jnp.where(upd, gv, best_v)
    cand = jnp.where((group == best) & real, sel, NEG)
    m1 = jnp.max(cand, axis=1, keepdims=True)
    i1 = jnp.min(jnp.where(cand == m1, lane_f, float(LANES)), axis=1, keepdims=True)
    cand2 = jnp.where(lane_f == i1, -jnp.inf, cand)
    m2 = jnp.max(cand2, axis=1, keepdims=True)
    i2 = jnp.min(jnp.where(cand2 == m2, lane_f, float(LANES)), axis=1, keepdims=True)
    g1 = jnp.sum(jnp.where(lane_f == i1, aff, 0.0), axis=1, keepdims=True)
    g2 = jnp.sum(jnp.where(lane_f == i2, aff, 0.0), axis=1, keepdims=True)
    tot = g1 + g2
    gate_ref[...] = jnp.where(lane == 0, g1 / tot, jnp.where(lane == 1, g2 / tot, 0.0))
    idx_ref[...] = jnp.where(lane == 0, i1, jnp.where(lane == 1, i2, 0.0)).astype(I32)


def ln_router(x, m, g, b, w_router_pad, router_bias_pad, tb=256):
    n, d = x.shape
    assert n % tb == 0
    row = pl.BlockSpec((tb, d), lambda i: (i, 0))
    vec = pl.BlockSpec((1, d), lambda i: (0, 0))
    meta = pl.BlockSpec((tb, LANES), lambda i: (i, 0))
    return pl.pallas_call(
        _ln_router_kernel,
        grid=(n // tb,),
        in_specs=[row, row, vec, vec,
                  pl.BlockSpec((d, LANES), lambda i: (0, 0)),
                  pl.BlockSpec((1, LANES), lambda i: (0, 0))],
        out_specs=[row, meta, meta],
        out_shape=[jax.ShapeDtypeStruct((n, d), F32),
                   jax.ShapeDtypeStruct((n, LANES), F32),
                   jax.ShapeDtypeStruct((n, LANES), I32)],
        compiler_params=_params("arbitrary"),
        name="ln_router",
    )(x, m, g.reshape(1, d), b.reshape(1, d), w_router_pad, router_bias_pad)


def _wait_rows(src_hbm, dst, sem, n):
    def body(r, c):
        pltpu.make_async_copy(src_hbm.at[pl.ds(0, 1), :], dst.at[pl.ds(0, 1), :], sem).wait()
        return c
    lax.fori_loop(0, n, body, 0)


def _gather_cast_kernel(tok_ref, x_hbm, o_ref, buf, sem):
    tm = buf.shape[0]
    base = pl.program_id(0) * tm

    def issue(r, c):
        pltpu.make_async_copy(x_hbm.at[pl.ds(tok_ref[base + r], 1), :], buf.at[pl.ds(r, 1), :], sem).start()
        return c
    lax.fori_loop(0, tm, issue, 0)
    _wait_rows(x_hbm, buf, sem, tm)
    _cast_rows(buf, o_ref)


def gather_rows_bf16(x, row_token, tm):
    n, d = x.shape
    r = row_token.shape[0]
    return pl.pallas_call(
        _gather_cast_kernel,
        grid_spec=pltpu.PrefetchScalarGridSpec(
            num_scalar_prefetch=1,
            grid=(r // tm,),
            in_specs=[pl.BlockSpec(memory_space=pl.ANY)],
            out_specs=pl.BlockSpec((tm, d), lambda i, tok: (i, 0)),
            scratch_shapes=[pltpu.VMEM((tm, d), F32), pltpu.SemaphoreType.DMA(())]),
        out_shape=jax.ShapeDtypeStruct((r, d), BF16),
        compiler_params=_params("arbitrary"),
        name="moe_gather",
    )(row_token, x)


def _expert_changed(te_ref, i):
    return (i == 0) | (te_ref[i] != te_ref[jnp.maximum(i - 1, 0)])


def _moe_up_kernel(te_ref, na_ref, xs_ref, wg_ref, wu_ref, h_ref, wgb, wub):
    i = pl.program_id(1)

    @pl.when(_expert_changed(te_ref, i))
    def _():
        _cast_rows(wg_ref, wgb)
        _cast_rows(wu_ref, wub)

    @pl.when(i < na_ref[0])
    def _():
        x = xs_ref[...]
        a = jnp.dot(x, wgb[...], preferred_element_type=F32)
        u = jnp.dot(x, wub[...], preferred_element_type=F32)
        h_ref[...] = (a * (1.0 / (1.0 + jnp.exp(-a))) * u).astype(h_ref.dtype)

    @pl.when(i >= na_ref[0])
    def _():
        h_ref[...] = jnp.zeros(h_ref.shape, h_ref.dtype)


def _moe_down_kernel(te_ref, na_ref, h_ref, wd_ref, gate_ref, y_ref, wdb):
    i = pl.program_id(1)

    @pl.when(_expert_changed(te_ref, i))
    def _():
        _cast_rows(wd_ref, wdb)

    @pl.when(i < na_ref[0])
    def _():
        y_ref[...] = gate_ref[...] * jnp.dot(h_ref[...], wdb[...], preferred_element_type=F32)

    @pl.when(i >= na_ref[0])
    def _():
        y_ref[...] = jnp.zeros(y_ref.shape, y_ref.dtype)


def moe_grouped(xs, gate_sorted, tile_expert, n_active, w_gate, w_up, w_down, layer, tm, tf=256, tn=1024):
    r, d = xs.shape
    n_tiles = r // tm
    de = w_gate.shape[3]
    tf, tn = min(tf, de), min(tn, d)
    h = pl.pallas_call(
        _moe_up_kernel,
        grid_spec=pltpu.PrefetchScalarGridSpec(
            num_scalar_prefetch=2,
            grid=(de // tf, n_tiles),
            in_specs=[pl.BlockSpec((tm, d), lambda f, i, te, na: (i, 0)),
                      pl.BlockSpec((None, None, d, tf), lambda f, i, te, na: (layer, te[i], 0, f)),
                      pl.BlockSpec((None, None, d, tf), lambda f, i, te, na: (layer, te[i], 0, f))],
            out_specs=pl.BlockSpec((tm, tf), lambda f, i, te, na: (i, f)),
            scratch_shapes=[pltpu.VMEM((d, tf), BF16), pltpu.VMEM((d, tf), BF16)]),
        out_shape=jax.ShapeDtypeStruct((r, de), BF16),
        compiler_params=_params("arbitrary", "arbitrary"),
        name="moe_up",
    )(tile_expert, n_active, xs, w_gate, w_up)
    return pl.pallas_call(
        _moe_down_kernel,
        grid_spec=pltpu.PrefetchScalarGridSpec(
            num_scalar_prefetch=2,
            grid=(d // tn, n_tiles),
            in_specs=[pl.BlockSpec((tm, de), lambda n, i, te, na: (i, 0)),
                      pl.BlockSpec((None, None, de, tn), lambda n, i, te, na: (layer, te[i], 0, n)),
                      pl.BlockSpec((tm, 1), lambda n, i, te, na: (i, 0))],
            out_specs=pl.BlockSpec((tm, tn), lambda n, i, te, na: (i, n)),
            scratch_shapes=[pltpu.VMEM((de, tn), BF16)]),
        out_shape=jax.ShapeDtypeStruct((r, d), F32),
        compiler_params=_params("arbitrary", "arbitrary"),
        name="moe_down",
    )(tile_expert, n_active, h, w_down, gate_sorted)


def _combine_ln_kernel(pos_ref, y_hbm, x_ref, g_ref, b_ref, xo_ref, buf, sem):
    tb = x_ref.shape[0]
    base = pl.program_id(0) * (2 * tb)

    def issue(r, c):
        pltpu.make_async_copy(y_hbm.at[pl.ds(pos_ref[base + r], 1), :], buf.at[pl.ds(r, 1), :], sem).start()
        return c
    lax.fori_loop(0, 2 * tb, issue, 0)
    _wait_rows(y_hbm, buf, sem, 2 * tb)
    f = buf[:tb, :] + buf[tb:, :]
    xo_ref[...] = _layer_norm_rows(DEEPNORM_ALPHA * x_ref[...] + f, g_ref[...], b_ref[...])


def combine_ln(y_sorted, pos_blocks, x1, g, b, tb):
    n, d = x1.shape
    row = pl.BlockSpec((tb, d), lambda i, pos: (i, 0))
    vec = pl.BlockSpec((1, d), lambda i, pos: (0, 0))
    return pl.pallas_call(
        _combine_ln_kernel,
        grid_spec=pltpu.PrefetchScalarGridSpec(
            num_scalar_prefetch=1,
            grid=(n // tb,),
            in_specs=[pl.BlockSpec(memory_space=pl.ANY), row, vec, vec],
            out_specs=row,
            scratch_shapes=[pltpu.VMEM((2 * tb, d), F32), pltpu.SemaphoreType.DMA(())]),
        out_shape=jax.ShapeDtypeStruct((n, d), F32),
        compiler_params=_params("arbitrary"),
        name="moe_combine_ln",
    )(pos_blocks, y_sorted, x1, g.reshape(1, d), b.reshape(1, d))


def moe_block(x, m, ln_g, ln_b, w_router_pad, router_bias_pad, w_gate, w_up, w_down, layer,
              tm=512, tb=128):
    n, d = x.shape
    x1, gate, idx = ln_router(x, m, ln_g[0], ln_b[0], w_router_pad, router_bias_pad)
    e_flat = idx[:, :TOP_K].reshape(-1)
    g_flat = gate[:, :TOP_K].reshape(-1)
    onehot = (e_flat[:, None] == jnp.arange(N_EXPERTS, dtype=I32)[None, :]).astype(I32)
    csum = jnp.cumsum(onehot, axis=0)
    rank = jnp.take_along_axis(csum, e_flat[:, None], axis=1)[:, 0] - 1
    counts = csum[-1]
    padded = (counts + tm - 1) // tm * tm
    ends = jnp.cumsum(padded)
    pos = (ends - padded)[e_flat] + rank
    n_pairs = TOP_K * n
    n_tiles = (n_pairs + N_EXPERTS * (tm - 1)) // tm + 1
    r = n_tiles * tm
    row_token = jnp.zeros((r,), I32).at[pos].set(jnp.arange(n_pairs, dtype=I32) // TOP_K)
    gate_sorted = jnp.zeros((r,), F32).at[pos].set(g_flat).reshape(r, 1)
    n_active = (ends[-1] // tm).astype(I32)
    tile = jnp.arange(n_tiles, dtype=I32)
    tile_expert = jnp.searchsorted(ends, jnp.minimum(tile, n_active - 1) * tm, side='right').astype(I32)
    tile_expert = jnp.minimum(tile_expert, N_EXPERTS - 1)
    xs = gather_rows_bf16(x1, row_token, tm)
    y_sorted = moe_grouped(xs, gate_sorted, tile_expert, n_active.reshape(1), w_gate, w_up, w_down,
                           layer, tm)
    pos_blocks = pos.reshape(n // tb, tb, TOP_K).transpose(0, 2, 1).reshape(-1)
    return combine_ln(y_sorted, pos_blocks, x1, ln_g[1], ln_b[1], tb)


def even_layer(xp, xs, cache_a_k, cache_a_v, cache_b_k, cache_b_v, w_in, w_out, rel_bias):
    bp, t, d = xp.shape
    bs, n, _ = xs.shape
    past, lb = cache_a_k.shape[1], cache_b_k.shape[1]
    hd = H_A * HEAD_DIM
    proj_p = matmul(xp.reshape(bp * t, d).astype(BF16), w_in).reshape(bp, t, -1)
    proj_s = matmul(xs.reshape(bs * n, d).astype(BF16), w_in).reshape(bs, n, -1)
    nb = hd // HEAD_DIM
    qa_cb, ka_cb, va_cb, qb_cb, kb_cb, vb_cb = (s * nb for s in range(6))
    seg = lambda p, s: p[..., s * hd:(s + 1) * hd]
    oa_p = stick_breaking_attention(proj_p, proj_p, proj_p, qa_cb, ka_cb, va_cb, H_A, 0, 256, 256)
    ob_p = band_attention_prompt(proj_p, qb_cb, kb_cb, vb_cb, rel_bias)
    mp = matmul_concat(oa_p.reshape(bp * t, hd), ob_p.reshape(bp * t, hd), w_out)
    bk_s = 256
    tk_a = -(-(past + n) // bk_s) * bk_s
    cat = lambda c, new, tot: jnp.concatenate(
        [c.reshape(bs, c.shape[1], hd), new, jnp.zeros((bs, tot - c.shape[1] - n, hd), F32)], 1)
    ka_s, va_s, kb_s, vb_s = seg(proj_s, 1), seg(proj_s, 2), seg(proj_s, 4), seg(proj_s, 5)
    oa_s = stick_breaking_attention(proj_s, cat(cache_a_k, ka_s, tk_a), cat(cache_a_v, va_s, tk_a),
                                    qa_cb, 0, 0, H_A, past, n, bk_s)
    ck = -(-(lb + n) // LANES) * LANES
    q_pos = past + np.arange(n)
    k_pos = np.concatenate([np.arange(past - lb, past), q_pos, np.full((ck - lb - n,), -1)])
    qc, kc = q_pos[:, None] // CHUNK, k_pos[None, :] // CHUNK
    valid = (kc <= qc) & (kc >= qc - N_LEFT_CHUNKS) & (k_pos[None, :] >= 0)
    rel = np.clip(q_pos[:, None] - k_pos[None, :], -REL_CLIP, REL_CLIP) + REL_CLIP
    bias_s = jnp.where(jnp.asarray(valid)[None], rel_bias.astype(F32)[:, rel], NEG)
    ob_s = band_attention_core(proj_s, cat(cache_b_k, kb_s, ck), cat(cache_b_v, vb_s, ck), qb_cb, bias_s)
    ms = matmul_concat(oa_s.reshape(bs * n, hd), ob_s.reshape(bs * n, hd), w_out)
    keep = min(BAND_REACH, t)
    heads = lambda a: a.reshape(a.shape[0], a.shape[1], H_A, HEAD_DIM)
    caches = (heads(seg(proj_p, 1)), heads(seg(proj_p, 2)), heads(ka_s), heads(va_s),
              heads(seg(proj_p, 4)[:, t - keep:]), heads(seg(proj_p, 5)[:, t - keep:]),
              heads(kb_s), heads(vb_s))
    return mp, ms, caches


def project(x, w_bf16):
    b, t, d = x.shape
    n = w_bf16.shape[1]
    n_pad = (-n) % 1024
    if n_pad:
        w_bf16 = jnp.pad(w_bf16, ((0, 0), (0, n_pad)))
    y = matmul(x.reshape(b * t, d).astype(BF16), w_bf16)
    return y[:, :n].reshape(b, t, n)


def split_cols(h, sizes):
    cuts, acc = [], 0
    for s in sizes[:-1]:
        acc += s
        cuts.append(acc)
    return jnp.split(h, cuts, axis=-1)


def rope(x, pos):
    half = x.shape[-1] // 2
    inv = ROPE_THETA ** (-jnp.arange(half, dtype=F32) / half)
    ang = pos.astype(F32)[:, None] * inv[None, :]
    cos, sin = jnp.cos(ang)[None, :, None, :], jnp.sin(ang)[None, :, None, :]
    xf = x.astype(F32)
    x1, x2 = xf[..., :half], xf[..., half:]
    return jnp.concatenate([x1 * cos - x2 * sin, x2 * cos + x1 * sin], -1).astype(x.dtype)


def dsa_attend(qc, qi, wi, q_pos, kc, vc, ki, k_pos, topk):
    B, Q = qc.shape[:2]
    logits = jnp.einsum('bqhe,bse->bqhs', qi.astype(F32), ki.astype(F32))
    score = jnp.einsum('bqh,bqhs->bqs', wi.astype(F32), jax.nn.relu(logits))
    adm = jnp.broadcast_to(((k_pos[None, :] // CHUNK) <= (q_pos[:, None] // CHUNK))[None], score.shape)
    _, sel = lax.top_k(jnp.where(adm, score, NEG), topk)
    sel_ok = jnp.take_along_axis(adm, sel, axis=-1)
    gather = jax.vmap(lambda rows, i: rows[i])
    ks, vs = gather(kc, sel), gather(vc, sel)
    qg = qc.reshape(B, Q, KV_C, H_C // KV_C, HEAD_DIM).astype(F32)
    s = jnp.einsum('bqgrd,bqkgd->bqgrk', qg, ks.astype(F32)) * (HEAD_DIM ** -0.5)
    s = jnp.where(sel_ok[:, :, None, None, :], s, NEG)
    p = jax.nn.softmax(s, axis=-1)
    o = jnp.einsum('bqgrk,bqkgd->bqgrd', p, vs.astype(F32))
    return o.reshape(B, Q, H_C, HEAD_DIM).astype(qc.dtype)


def causal_conv(raw, buf, conv_w):
    xp = jnp.concatenate([buf, raw], 1)
    out = lax.conv_general_dilated(xp, conv_w[:, None, :], window_strides=(1,), padding='VALID',
                                   dimension_numbers=('NWC', 'WIO', 'NWC'),
                                   feature_group_count=raw.shape[-1])
    return jax.nn.silu(out), xp[:, xp.shape[1] - (CONV_W - 1):]


def l2norm(t):
    return t * lax.rsqrt(jnp.sum(t * t, -1, keepdims=True) + RMS_EPS)


def gdn_inputs(qkv_raw, buf, a_raw, b_raw, conv_w, a_log, dt_bias):
    qkv, new_buf = causal_conv(qkv_raw, buf, conv_w)
    B, T, _ = qkv.shape
    q, k, v = split_cols(qkv, (H_D * DK_D, H_D * DK_D, H_D * DV_D))
    q = l2norm(q.reshape(B, T, H_D, DK_D).astype(F32)) * (DK_D ** -0.5)
    k = l2norm(k.reshape(B, T, H_D, DK_D).astype(F32))
    v = v.reshape(B, T, H_D, DV_D).astype(F32)
    g = -jnp.exp(a_log.astype(F32)) * jax.nn.softplus(a_raw.astype(F32) + dt_bias.astype(F32))
    beta = jax.nn.sigmoid(b_raw.astype(F32))
    return q, k, v, g, beta, new_buf


def gdn_chunk(S, q, k, v, g, beta):
    C = q.shape[1]
    qh, kh, vh = q.transpose(0, 2, 1, 3), k.transpose(0, 2, 1, 3), v.transpose(0, 2, 1, 3)
    G = jnp.cumsum(g, axis=1).transpose(0, 2, 1)
    bt = beta.transpose(0, 2, 1)
    incl = jnp.tril(jnp.ones((C, C), dtype=bool))
    strict = jnp.tril(jnp.ones((C, C), dtype=bool), -1)
    decay = jnp.exp(jnp.where(incl, G[..., :, None] - G[..., None, :], -jnp.inf))
    kk = jnp.einsum('bhtd,bhjd->bhtj', kh, kh)
    a_mat = jnp.where(strict, bt[..., :, None] * kk * decay, 0.0) + jnp.eye(C, dtype=F32)
    rhs = jnp.concatenate([vh * bt[..., None], kh * (bt * jnp.exp(G))[..., None]], -1)
    sol = lax.linalg.triangular_solve(a_mat, rhs, left_side=True, lower=True, unit_diagonal=True)
    u = sol[..., :DV_D] - jnp.einsum('bhtk,bhkv->bhtv', sol[..., DV_D:], S)
    qk = jnp.einsum('bhtd,bhjd->bhtj', qh, kh) * decay
    o = jnp.einsum('bhtk,bhkv->bhtv', qh * jnp.exp(G)[..., None], S) + jnp.einsum('bhtj,bhjv->bhtv', qk, u)
    g_last = G[..., -1]
    S_new = S * jnp.exp(g_last)[..., None, None] + jnp.einsum(
        'bhjk,bhjv->bhkv', kh * jnp.exp(g_last[..., None] - G)[..., None], u)
    return S_new, o.transpose(0, 2, 1, 3)


def gdn_output(o, gate, norm_w):
    B, T = o.shape[:2]
    o = o * lax.rsqrt(jnp.mean(o * o, -1, keepdims=True) + RMS_EPS) * norm_w.astype(F32)
    return (o.reshape(B, T, H_D * DV_D) * jax.nn.silu(gate.astype(F32))).astype(gate.dtype)


def odd_project(x, w_in, pos):
    B, T, _ = x.shape
    qc, kc, vc, qi, ki, wi, qkv_d, a_d, b_d, g_d = split_cols(project(x, w_in), ODD_SIZES)
    qc = rope(qc.reshape(B, T, H_C, HEAD_DIM), pos)
    kc = rope(kc.reshape(B, T, KV_C, HEAD_DIM), pos)
    vc = vc.reshape(B, T, KV_C, HEAD_DIM)
    qi = rope(qi.reshape(B, T, H_IDX, D_IDX), pos)
    ki = rope(ki[:, :, None, :], pos)[:, :, 0]
    return qc, kc, vc, qi, ki, wi, qkv_d, a_d, b_d, g_d


def odd_prompt(x, w_in, w_out, conv_w, a_log, dt_bias, norm_w):
    B, T, _ = x.shape
    pos = jnp.arange(T)
    qc, kc, vc, qi, ki, wi, qkv_d, a_d, b_d, g_d = odd_project(x, w_in, pos)
    nq = T // Q_BLOCK
    topk = min(TOPK_MAX, T // 4)
    blk = lambda t: t.reshape((B, nq, Q_BLOCK) + t.shape[2:]).swapaxes(0, 1)
    oc = lax.map(lambda a: dsa_attend(a[0], a[1], a[2], a[3], kc, vc, ki, pos, topk),
                 (blk(qc), blk(qi), blk(wi), pos.reshape(nq, Q_BLOCK)))
    oc = oc.swapaxes(0, 1).reshape(B, T, H_C * HEAD_DIM)
    buf0 = jnp.zeros((B, CONV_W - 1, QKV_D), qkv_d.dtype)
    q, k, v, g, beta, conv_buf = gdn_inputs(qkv_d, buf0, a_d, b_d, conv_w, a_log, dt_bias)
    nc = T // CHUNK
    chk = lambda t: t.reshape((B, nc, CHUNK) + t.shape[2:]).swapaxes(0, 1)
    S0 = jnp.zeros((B, H_D, DK_D, DV_D), F32)
    S, od = lax.scan(lambda S, c: gdn_chunk(S, *c), S0, (chk(q), chk(k), chk(v), chk(g), chk(beta)))
    od = gdn_output(od.swapaxes(0, 1).reshape(B, T, H_D, DV_D), g_d, norm_w)
    y = project(jnp.concatenate([oc, od], -1), w_out)
    return y, kc, vc, ki, S.astype(x.dtype), conv_buf


def odd_sample(x, cache_c_k, cache_c_v, cache_c_idx_k, state_d_ssm, state_d_conv,
               w_in, w_out, conv_w, a_log, dt_bias, norm_w):
    B, n, _ = x.shape
    past = cache_c_k.shape[1]
    q_pos = past + jnp.arange(n)
    qc, kc, vc, qi, ki, wi, qkv_d, a_d, b_d, g_d = odd_project(x, w_in, q_pos)
    topk = min(TOPK_MAX, (past + n) // 4)
    oc = dsa_attend(qc, qi, wi, q_pos, jnp.concatenate([cache_c_k, kc], 1),
                    jnp.concatenate([cache_c_v, vc], 1), jnp.concatenate([cache_c_idx_k, ki], 1),
                    jnp.arange(past + n), topk).reshape(B, n, H_C * HEAD_DIM)
    q, k, v, g, beta, conv_buf = gdn_inputs(qkv_d, state_d_conv, a_d, b_d, conv_w, a_log, dt_bias)
    S, od = gdn_chunk(state_d_ssm.astype(F32), q, k, v, g, beta)
    od = gdn_output(od, g_d, norm_w)
    y = project(jnp.concatenate([oc, od], -1), w_out)
    return y, kc, vc, ki, S.astype(state_d_ssm.dtype), conv_buf


def kernel(x_prompt, x_sample, cache_a_k, cache_a_v, cache_b_k, cache_b_v, cache_c_k, cache_c_v,
           cache_c_idx_k, state_d_ssm, state_d_conv, w_in_even, w_out_even, b_rel_bias, w_in_odd,
           w_out_odd, d_conv_w, d_a_log, d_dt_bias, d_norm_w, w_router, router_bias, moe_w_gate,
           moe_w_up, moe_w_down, ln_g, ln_b):
    bp, t, d = x_prompt.shape
    bs, n, _ = x_sample.shape
    n_p = bp * t
    w_in_even = w_in_even.astype(BF16)
    w_out_even = w_out_even.astype(BF16)
    w_in_odd = w_in_odd.astype(BF16)
    w_out_odd = w_out_odd.astype(BF16)
    wr_pad = jnp.pad(w_router.astype(BF16), ((0, 0), (0, LANES - N_EXPERTS)))
    rb_pad = jnp.pad(router_bias.astype(F32), (0, LANES - N_EXPERTS)).reshape(1, LANES)
    x = jnp.concatenate([x_prompt.reshape(n_p, d), x_sample.reshape(bs * n, d)], 0)
    for layer in range(DEPTH):
        xp, xs = x[:n_p].reshape(bp, t, d), x[n_p:].reshape(bs, n, d)
        if layer % 2 == 0:
            mp, ms, (a_k_p, a_v_p, a_k_s, a_v_s, b_k_p, b_v_p, b_k_s, b_v_s) = even_layer(
                xp, xs, cache_a_k, cache_a_v, cache_b_k, cache_b_v, w_in_even, w_out_even, b_rel_bias)
        else:
            mp, c_k_p, c_v_p, c_i_p, d_s_p, d_c_p = odd_prompt(xp, w_in_odd, w_out_odd, d_conv_w,
                                                                d_a_log, d_dt_bias, d_norm_w)
            ms, c_k_s, c_v_s, c_i_s, d_s_s, d_c_s = odd_sample(xs, cache_c_k, cache_c_v, cache_c_idx_k,
                                                                state_d_ssm, state_d_conv, w_in_odd,
                                                                w_out_odd, d_conv_w, d_a_log, d_dt_bias,
                                                                d_norm_w)
        m = jnp.concatenate([mp.reshape(n_p, d), ms.reshape(bs * n, d)], 0)
        x = moe_block(x, m, ln_g[layer], ln_b[layer], wr_pad, rb_pad,
                      moe_w_gate, moe_w_up, moe_w_down, layer)
    xp, xs = x[:n_p].reshape(bp, t, d), x[n_p:].reshape(bs, n, d)
    return (xp, xs, a_k_p, a_v_p, a_k_s, a_v_s, b_k_p, b_v_p, b_k_s, b_v_s,
            c_k_p, c_v_p, c_i_p, c_k_s, c_v_s, c_i_s, d_s_p, d_c_p, d_s_s, d_c_s)
```

```python
import functools

import jax
import jax.numpy as jnp
import numpy as np
from jax import lax
from jax.experimental import pallas as pl
from jax.experimental.pallas import tpu as pltpu

D_MODEL = 4096
DEPTH = 2
CHUNK = 64
HEAD_DIM = 128
Q_BLOCK = 128
NEG = -1e30
H_A = 16
H_B = 16
N_LEFT_CHUNKS = 8
BAND_REACH = N_LEFT_CHUNKS * CHUNK
REL_CLIP = 128
EVEN_SIZES = (H_A * HEAD_DIM,) * 3 + (H_B * HEAD_DIM,) * 3
H_C = 16
KV_C = 4
H_IDX = 16
D_IDX = 64
TOPK_MAX = 256
H_D = 16
DK_D = 128
DV_D = 128
CONV_W = 4
QKV_D = H_D * (2 * DK_D + DV_D)
ODD_SIZES = (H_C * HEAD_DIM, KV_C * HEAD_DIM, KV_C * HEAD_DIM, H_IDX * D_IDX, D_IDX, H_IDX,
             QKV_D, H_D, H_D, H_D * DV_D)
N_EXPERTS = 16
N_GROUPS = 4
EXPERTS_PER_GROUP = N_EXPERTS // N_GROUPS
TOP_K = 2
D_EXPERT = 1024
ROPE_THETA = 10000.0
LN_EPS = 1e-5
RMS_EPS = 1e-6
DEEPNORM_ALPHA = (2 * DEPTH) ** 0.25

F32 = jnp.float32
BF16 = jnp.bfloat16
I32 = jnp.int32

LANES = 128
VMEM_LIMIT_BYTES = 56 * 1024 * 1024


def _params(*sem):
    return pltpu.CompilerParams(dimension_semantics=sem, vmem_limit_bytes=VMEM_LIMIT_BYTES)


def _cast_rows(src_ref, dst_ref, dst_off=0, chunk=256):
    rows = src_ref.shape[0]
    assert rows % chunk == 0 and dst_off % chunk == 0

    def body(c, carry):
        r = pl.multiple_of(c * chunk, chunk)
        dst_ref[pl.ds(dst_off + r, chunk), :] = src_ref[pl.ds(r, chunk), :].astype(dst_ref.dtype)
        return carry
    lax.fori_loop(0, rows // chunk, body, 0)


def _matmul_kernel(x_ref, w_ref, o_ref):
    o_ref[...] = jnp.dot(x_ref[...], w_ref[...], preferred_element_type=F32)


def matmul(x, w, tm=512, tn=1024):
    m, k = x.shape
    n = w.shape[1]
    tm = min(tm, m)
    tn = min(tn, n)
    assert m % tm == 0 and n % tn == 0
    return pl.pallas_call(
        _matmul_kernel,
        grid=(n // tn, m // tm),
        in_specs=[pl.BlockSpec((tm, k), lambda j, i: (i, 0)),
                  pl.BlockSpec((k, tn), lambda j, i: (0, j))],
        out_specs=pl.BlockSpec((tm, tn), lambda j, i: (i, j)),
        out_shape=jax.ShapeDtypeStruct((m, n), F32),
        compiler_params=_params("arbitrary", "arbitrary"),
        name="proj_matmul",
    )(x, w)


def _matmul2_kernel(xa_ref, xb_ref, w_ref, o_ref):
    ka = xa_ref.shape[1]
    o_ref[...] = (jnp.dot(xa_ref[...], w_ref[:ka, :], preferred_element_type=F32)
                  + jnp.dot(xb_ref[...], w_ref[ka:, :], preferred_element_type=F32))


def matmul_concat(xa, xb, w, tm=512, tn=1024):
    m, ka = xa.shape
    kb = xb.shape[1]
    n = w.shape[1]
    tm = min(tm, m)
    assert m % tm == 0 and n % tn == 0
    return pl.pallas_call(
        _matmul2_kernel,
        grid=(n // tn, m // tm),
        in_specs=[pl.BlockSpec((tm, ka), lambda j, i: (i, 0)),
                  pl.BlockSpec((tm, kb), lambda j, i: (i, 0)),
                  pl.BlockSpec((ka + kb, tn), lambda j, i: (0, j))],
        out_specs=pl.BlockSpec((tm, tn), lambda j, i: (i, j)),
        out_shape=jax.ShapeDtypeStruct((m, n), F32),
        compiler_params=_params("arbitrary", "arbitrary"),
        name="out_matmul",
    )(xa, xb, w)


def _sb_kernel(q_ref, k_ref, v_ref, u_ref, o_ref, kb_ref, vb_ref, *, bq, bk, q_off, tk, scale):
    qi = pl.program_id(2)

    @pl.when(qi == 0)
    def _():
        _cast_rows(k_ref.at[0], kb_ref)
        _cast_rows(v_ref.at[0], vb_ref)

    q = q_ref[0].astype(BF16)
    r0 = qi * bq
    rows = q_off + r0 + lax.broadcasted_iota(I32, (bq, bk), 0)
    cols = lax.broadcasted_iota(I32, (bq, bk), 1)
    n_kb = jnp.minimum((q_off + r0 + bq + bk - 1) // bk, tk // bk)
    tri = u_ref[...]

    def body(it, carry):
        acc, run = carry
        start = pl.multiple_of((n_kb - 1 - it) * bk, bk)
        kblk = kb_ref[pl.ds(start, bk), :]
        vblk = vb_ref[pl.ds(start, bk), :]
        z = lax.dot_general(q, kblk, (((1,), (1,)), ((), ())), preferred_element_type=F32) * scale
        t = jnp.log(1.0 + jnp.exp(-jnp.abs(z)))
        earlier = (cols + start) < rows
        log_1mb = jnp.where(earlier, -(jnp.maximum(z, 0.0) + t), 0.0)
        log_b = jnp.minimum(z, 0.0) - t
        hi = log_1mb.astype(BF16)
        lo = (log_1mb - hi.astype(F32)).astype(BF16)
        between = (jnp.dot(hi, tri, preferred_element_type=F32)
                   + jnp.dot(lo, tri, preferred_element_type=F32) + run)
        w = jnp.where(earlier, jnp.exp(log_b + between), 0.0)
        acc = acc + jnp.dot(w.astype(BF16), vblk, preferred_element_type=F32)
        run = run + jnp.sum(log_1mb, axis=1, keepdims=True)
        return acc, run

    acc, _ = lax.fori_loop(0, n_kb, body, (jnp.zeros((bq, HEAD_DIM), F32), jnp.zeros((bq, 1), F32)))
    o_ref[0] = acc.astype(o_ref.dtype)


def stick_breaking_attention(q_arr, k_arr, v_arr, q_cb, k_cb, v_cb, n_heads, q_off, bq, bk):
    b, tq, _ = q_arr.shape
    tk = k_arr.shape[1]
    assert tq % bq == 0 and tk % bk == 0
    tri = jnp.asarray(np.tril(np.ones((bk, bk), np.float32), -1), BF16)
    kern = functools.partial(_sb_kernel, bq=bq, bk=bk, q_off=q_off, tk=tk, scale=HEAD_DIM ** -0.5)
    return pl.pallas_call(
        kern,
        grid=(b, n_heads, tq // bq),
        in_specs=[pl.BlockSpec((1, bq, HEAD_DIM), lambda b_, h, i: (b_, i, q_cb + h)),
                  pl.BlockSpec((1, tk, HEAD_DIM), lambda b_, h, i: (b_, 0, k_cb + h)),
                  pl.BlockSpec((1, tk, HEAD_DIM), lambda b_, h, i: (b_, 0, v_cb + h)),
                  pl.BlockSpec((bk, bk), lambda b_, h, i: (0, 0))],
        out_specs=pl.BlockSpec((1, bq, HEAD_DIM), lambda b_, h, i: (b_, i, h)),
        out_shape=jax.ShapeDtypeStruct((b, tq, n_heads * HEAD_DIM), BF16),
        scratch_shapes=[pltpu.VMEM((tk, HEAD_DIM), BF16), pltpu.VMEM((tk, HEAD_DIM), BF16)],
        compiler_params=_params("arbitrary", "arbitrary", "arbitrary"),
        name="stick_breaking",
    )(q_arr, k_arr, v_arr, tri)


def _softmax_pv(s, v):
    m = jnp.max(s, axis=1, keepdims=True)
    e = jnp.exp(s - m)
    p = e / jnp.sum(e, axis=1, keepdims=True)
    return jnp.dot(p.astype(BF16), v, preferred_element_type=F32)


def _band_prompt_kernel(q_ref, k_ref, v_ref, bias_ref, o_ref, kb_ref, vb_ref, *, bq, reach, scale):
    qi = pl.program_id(2)

    @pl.when(qi == 0)
    def _():
        kb_ref[:reach, :] = jnp.zeros((reach, HEAD_DIM), BF16)
        vb_ref[:reach, :] = jnp.zeros((reach, HEAD_DIM), BF16)
        _cast_rows(k_ref.at[0], kb_ref, dst_off=reach)
        _cast_rows(v_ref.at[0], vb_ref, dst_off=reach)

    win = reach + bq
    start = pl.multiple_of(qi * bq, bq)
    q = q_ref[0].astype(BF16)
    kw = kb_ref[pl.ds(start, win), :]
    vw = vb_ref[pl.ds(start, win), :]
    s = lax.dot_general(q, kw, (((1,), (1,)), ((), ())), preferred_element_type=F32) * scale
    k_pos = start - reach + lax.broadcasted_iota(I32, (bq, win), 1)
    s = jnp.where(k_pos >= 0, s + bias_ref[0], NEG)
    o_ref[0] = _softmax_pv(s, vw).astype(o_ref.dtype)


def band_attention_prompt(proj, q_cb, k_cb, v_cb, rel_bias, bq=256):
    b, t, _ = proj.shape
    assert bq % CHUNK == 0 and t % bq == 0
    win = BAND_REACH + bq
    i = np.arange(bq)[:, None]
    j = np.arange(win)[None, :]
    rel = np.clip(BAND_REACH + i - j, -REL_CLIP, REL_CLIP) + REL_CLIP
    qc, kc = i // CHUNK, j // CHUNK - N_LEFT_CHUNKS
    in_band = (kc <= qc) & (kc >= qc - N_LEFT_CHUNKS)
    bias = jnp.where(jnp.asarray(in_band)[None], rel_bias.astype(F32)[:, rel], NEG)
    kern = functools.partial(_band_prompt_kernel, bq=bq, reach=BAND_REACH, scale=HEAD_DIM ** -0.5)
    return pl.pallas_call(
        kern,
        grid=(b, H_B, t // bq),
        in_specs=[pl.BlockSpec((1, bq, HEAD_DIM), lambda b_, h, i_: (b_, i_, q_cb + h)),
                  pl.BlockSpec((1, t, HEAD_DIM), lambda b_, h, i_: (b_, 0, k_cb + h)),
                  pl.BlockSpec((1, t, HEAD_DIM), lambda b_, h, i_: (b_, 0, v_cb + h)),
                  pl.BlockSpec((1, bq, win), lambda b_, h, i_: (h, 0, 0))],
        out_specs=pl.BlockSpec((1, bq, HEAD_DIM), lambda b_, h, i_: (b_, i_, h)),
        out_shape=jax.ShapeDtypeStruct((b, t, H_B * HEAD_DIM), BF16),
        scratch_shapes=[pltpu.VMEM((BAND_REACH + t, HEAD_DIM), BF16),
                        pltpu.VMEM((BAND_REACH + t, HEAD_DIM), BF16)],
        compiler_params=_params("arbitrary", "arbitrary", "arbitrary"),
        name="band_prompt",
    )(proj, proj, proj, bias)


def _band_core_kernel(q_ref, k_ref, v_ref, bias_ref, o_ref, *, scale):
    q = q_ref[0].astype(BF16)
    k = k_ref[0].astype(BF16)
    v = v_ref[0].astype(BF16)
    s = lax.dot_general(q, k, (((1,), (1,)), ((), ())), preferred_element_type=F32) * scale
    o_ref[0] = _softmax_pv(s + bias_ref[0], v).astype(o_ref.dtype)


def band_attention_core(q_arr, k_arr, v_arr, q_cb, bias):
    b, cq, _ = q_arr.shape
    ck = k_arr.shape[1]
    h_n = bias.shape[0]
    kern = functools.partial(_band_core_kernel, scale=HEAD_DIM ** -0.5)
    return pl.pallas_call(
        kern,
        grid=(b, h_n),
        in_specs=[pl.BlockSpec((1, cq, HEAD_DIM), lambda b_, h: (b_, 0, q_cb + h)),
                  pl.BlockSpec((1, ck, HEAD_DIM), lambda b_, h: (b_, 0, h)),
                  pl.BlockSpec((1, ck, HEAD_DIM), lambda b_, h: (b_, 0, h)),
                  pl.BlockSpec((1, cq, ck), lambda b_, h: (h, 0, 0))],
        out_specs=pl.BlockSpec((1, cq, HEAD_DIM), lambda b_, h: (b_, 0, h)),
        out_shape=jax.ShapeDtypeStruct((b, cq, h_n * HEAD_DIM), BF16),
        compiler_params=_params("arbitrary", "arbitrary"),
        name="band_core",
    )(q_arr, k_arr, v_arr, bias)


def _layer_norm_rows(h, g, b):
    mu = jnp.mean(h, axis=-1, keepdims=True)
    d = h - mu
    var = jnp.mean(d * d, axis=-1, keepdims=True)
    return d * lax.rsqrt(var + LN_EPS) * g + b


def _ln_router_kernel(x_ref, m_ref, g_ref, b_ref, wr_ref, rb_ref, x1_ref, gate_ref, idx_ref):
    y = _layer_norm_rows(DEEPNORM_ALPHA * x_ref[...] + m_ref[...], g_ref[...], b_ref[...])
    x1_ref[...] = y
    tb = y.shape[0]
    logits = jnp.dot(y.astype(BF16), wr_ref[...], preferred_element_type=F32)
    lane = lax.broadcasted_iota(I32, (tb, LANES), 1)
    lane_f = lane.astype(F32)
    real = lane < N_EXPERTS
    lg = jnp.where(real, logits, -jnp.inf)
    e = jnp.exp(lg - jnp.max(lg, axis=1, keepdims=True))
    aff = e / jnp.sum(e, axis=1, keepdims=True)
    sel = jnp.where(real, aff + rb_ref[...], -jnp.inf)
    group = lane // EXPERTS_PER_GROUP
    best = jnp.zeros((tb, 1), I32)
    best_v = jnp.max(jnp.where(group == 0, sel, -jnp.inf), axis=1, keepdims=True)
    for gi in range(1, N_GROUPS):
        gv = jnp.max(jnp.where(group == gi, sel, -jnp.inf), axis=1, keepdims=True)
        upd = gv > best_v
        best = jnp.where(upd, gi, best)
        best_v = jnp.where(upd, gv, best_v)
    cand = jnp.where((group == best) & real, sel, NEG)
    m1 = jnp.max(cand, axis=1, keepdims=True)
    i1 = jnp.min(jnp.where(cand == m1, lane_f, float(LANES)), axis=1, keepdims=True)
    cand2 = jnp.where(lane_f == i1, -jnp.inf, cand)
    m2 = jnp.max(cand2, axis=1, keepdims=True)
    i2 = jnp.min(jnp.where(cand2 == m2, lane_f, float(LANES)), axis=1, keepdims=True)
    g1 = jnp.sum(jnp.where(lane_f == i1, aff, 0.0), axis=1, keepdims=True)
    g2 = jnp.sum(jnp.where(lane_f == i2, aff, 0.0), axis=1, keepdims=True)
    tot = g1 + g2
    gate_ref[...] = jnp.where(lane == 0, g1 / tot, jnp.where(lane == 1, g2 / tot, 0.0))
    idx_ref[...] = jnp.where(lane == 0, i1, jnp.where(lane == 1, i2, 0.0)).astype(I32)


def ln_router(x, m, g, b, w_router_pad, router_bias_pad, tb=256):
    n, d = x.shape
    assert n % tb == 0
    row = pl.BlockSpec((tb, d), lambda i: (i, 0))
    vec = pl.BlockSpec((1, d), lambda i: (0, 0))
    meta = pl.BlockSpec((tb, LANES), lambda i: (i, 0))
    return pl.pallas_call(
        _ln_router_kernel,
        grid=(n // tb,),
        in_specs=[row, row, vec, vec,
                  pl.BlockSpec((d, LANES), lambda i: (0, 0)),
                  pl.BlockSpec((1, LANES), lambda i: (0, 0))],
        out_specs=[row, meta, meta],
        out_shape=[jax.ShapeDtypeStruct((n, d), F32),
                   jax.ShapeDtypeStruct((n, LANES), F32),
                   jax.ShapeDtypeStruct((n, LANES), I32)],
        compiler_params=_params("arbitrary"),
        name="ln_router",
    )(x, m, g.reshape(1, d), b.reshape(1, d), w_router_pad, router_bias_pad)


def _wait_rows(src_hbm, dst, sem, n):
    def body(r, c):
        pltpu.make_async_copy(src_hbm.at[pl.ds(0, 1), :], dst.at[pl.ds(0, 1), :], sem).wait()
        return c
    lax.fori_loop(0, n, body, 0)


def _gather_cast_kernel(tok_ref, x_hbm, o_ref, buf, sem):
    tm = buf.shape[0]
    base = pl.program_id(0) * tm

    def issue(r, c):
        pltpu.make_async_copy(x_hbm.at[pl.ds(tok_ref[base + r], 1), :], buf.at[pl.ds(r, 1), :], sem).start()
        return c
    lax.fori_loop(0, tm, issue, 0)
    _wait_rows(x_hbm, buf, sem, tm)
    _cast_rows(buf, o_ref)


def gather_rows_bf16(x, row_token, tm):
    n, d = x.shape
    r = row_token.shape[0]
    return pl.pallas_call(
        _gather_cast_kernel,
        grid_spec=pltpu.PrefetchScalarGridSpec(
            num_scalar_prefetch=1,
            grid=(r // tm,),
            in_specs=[pl.BlockSpec(memory_space=pl.ANY)],
            out_specs=pl.BlockSpec((tm, d), lambda i, tok: (i, 0)),
            scratch_shapes=[pltpu.VMEM((tm, d), F32), pltpu.SemaphoreType.DMA(())]),
        out_shape=jax.ShapeDtypeStruct((r, d), BF16),
        compiler_params=_params("arbitrary"),
        name="moe_gather",
    )(row_token, x)


def _expert_changed(te_ref, i):
    return (i == 0) | (te_ref[i] != te_ref[jnp.maximum(i - 1, 0)])


def _moe_up_kernel(te_ref, na_ref, xs_ref, wg_ref, wu_ref, h_ref, wgb, wub):
    i = pl.program_id(1)

    @pl.when(_expert_changed(te_ref, i))
    def _():
        _cast_rows(wg_ref, wgb)
        _cast_rows(wu_ref, wub)

    @pl.when(i < na_ref[0])
    def _():
        x = xs_ref[...]
        a = jnp.dot(x, wgb[...], preferred_element_type=F32)
        u = jnp.dot(x, wub[...], preferred_element_type=F32)
        h_ref[...] = (a * (1.0 / (1.0 + jnp.exp(-a))) * u).astype(h_ref.dtype)

    @pl.when(i >= na_ref[0])
    def _():
        h_ref[...] = jnp.zeros(h_ref.shape, h_ref.dtype)


def _moe_down_kernel(te_ref, na_ref, h_ref, wd_ref, gate_ref, y_ref, wdb):
    i = pl.program_id(1)

    @pl.when(_expert_changed(te_ref, i))
    def _():
        _cast_rows(wd_ref, wdb)

    @pl.when(i < na_ref[0])
    def _():
        y_ref[...] = gate_ref[...] * jnp.dot(h_ref[...], wdb[...], preferred_element_type=F32)

    @pl.when(i >= na_ref[0])
    def _():
        y_ref[...] = jnp.zeros(y_ref.shape, y_ref.dtype)


def moe_grouped(xs, gate_sorted, tile_expert, n_active, w_gate, w_up, w_down, layer, tm, tf=256, tn=1024):
    r, d = xs.shape
    n_tiles = r // tm
    de = w_gate.shape[3]
    tf, tn = min(tf, de), min(tn, d)
    h = pl.pallas_call(
        _moe_up_kernel,
        grid_spec=pltpu.PrefetchScalarGridSpec(
            num_scalar_prefetch=2,
            grid=(de // tf, n_tiles),
            in_specs=[pl.BlockSpec((tm, d), lambda f, i, te, na: (i, 0)),
                      pl.BlockSpec((None, None, d, tf), lambda f, i, te, na: (layer, te[i], 0, f)),
                      pl.BlockSpec((None, None, d, tf), lambda f, i, te, na: (layer, te[i], 0, f))],
            out_specs=pl.BlockSpec((tm, tf), lambda f, i, te, na: (i, f)),
            scratch_shapes=[pltpu.VMEM((d, tf), BF16), pltpu.VMEM((d, tf), BF16)]),
        out_shape=jax.ShapeDtypeStruct((r, de), BF16),
        compiler_params=_params("arbitrary", "arbitrary"),
        name="moe_up",
    )(tile_expert, n_active, xs, w_gate, w_up)
    return pl.pallas_call(
        _moe_down_kernel,
        grid_spec=pltpu.PrefetchScalarGridSpec(
            num_scalar_prefetch=2,
            grid=(d // tn, n_tiles),
            in_specs=[pl.BlockSpec((tm, de), lambda n, i, te, na: (i, 0)),
                      pl.BlockSpec((None, None, de, tn), lambda n, i, te, na: (layer, te[i], 0, n)),
                      pl.BlockSpec((tm, 1), lambda n, i, te, na: (i, 0))],
            out_specs=pl.BlockSpec((tm, tn), lambda n, i, te, na: (i, n)),
            scratch_shapes=[pltpu.VMEM((de, tn), BF16)]),
        out_shape=jax.ShapeDtypeStruct((r, d), F32),
        compiler_params=_params("arbitrary", "arbitrary"),
        name="moe_down",
    )(tile_expert, n_active, h, w_down, gate_sorted)


def _combine_ln_kernel(pos_ref, y_hbm, x_ref, g_ref, b_ref, xo_ref, buf, sem):
    tb = x_ref.shape[0]
    base = pl.program_id(0) * (2 * tb)

    def issue(r, c):
        pltpu.make_async_copy(y_hbm.at[pl.ds(pos_ref[base + r], 1), :], buf.at[pl.ds(r, 1), :], sem).start()
        return c
    lax.fori_loop(0, 2 * tb, issue, 0)
    _wait_rows(y_hbm, buf, sem, 2 * tb)
    f = buf[:tb, :] + buf[tb:, :]
    xo_ref[...] = _layer_norm_rows(DEEPNORM_ALPHA * x_ref[...] + f, g_ref[...], b_ref[...])


def combine_ln(y_sorted, pos_blocks, x1, g, b, tb):
    n, d = x1.shape
    row = pl.BlockSpec((tb, d), lambda i, pos: (i, 0))
    vec = pl.BlockSpec((1, d), lambda i, pos: (0, 0))
    return pl.pallas_call(
        _combine_ln_kernel,
        grid_spec=pltpu.PrefetchScalarGridSpec(
            num_scalar_prefetch=1,
            grid=(n // tb,),
            in_specs=[pl.BlockSpec(memory_space=pl.ANY), row, vec, vec],
            out_specs=row,
            scratch_shapes=[pltpu.VMEM((2 * tb, d), F32), pltpu.SemaphoreType.DMA(())]),
        out_shape=jax.ShapeDtypeStruct((n, d), F32),
        compiler_params=_params("arbitrary"),
        name="moe_combine_ln",
    )(pos_blocks, y_sorted, x1, g.reshape(1, d), b.reshape(1, d))


def moe_block(x, m, ln_g, ln_b, w_router_pad, router_bias_pad, w_gate, w_up, w_down, layer,
              tm=512, tb=128):
    n, d = x.shape
    x1, gate, idx = ln_router(x, m, ln_g[0], ln_b[0], w_router_pad, router_bias_pad)
    e_flat = idx[:, :TOP_K].reshape(-1)
    g_flat = gate[:, :TOP_K].reshape(-1)
    onehot = (e_flat[:, None] == jnp.arange(N_EXPERTS, dtype=I32)[None, :]).astype(I32)
    csum = jnp.cumsum(onehot, axis=0)
    rank = jnp.take_along_axis(csum, e_flat[:, None], axis=1)[:, 0] - 1
    counts = csum[-1]
    padded = (counts + tm - 1) // tm * tm
    ends = jnp.cumsum(padded)
    pos = (ends - padded)[e_flat] + rank
    n_pairs = TOP_K * n
    n_tiles = (n_pairs + N_EXPERTS * (tm - 1)) // tm + 1
    r = n_tiles * tm
    row_token = jnp.zeros((r,), I32).at[pos].set(jnp.arange(n_pairs, dtype=I32) // TOP_K)
    gate_sorted = jnp.zeros((r,), F32).at[pos].set(g_flat).reshape(r, 1)
    n_active = (ends[-1] // tm).astype(I32)
    tile = jnp.arange(n_tiles, dtype=I32)
    tile_expert = jnp.searchsorted(ends, jnp.minimum(tile, n_active - 1) * tm, side='right').astype(I32)
    tile_expert = jnp.minimum(tile_expert, N_EXPERTS - 1)
    xs = gather_rows_bf16(x1, row_token, tm)
    y_sorted = moe_grouped(xs, gate_sorted, tile_expert, n_active.reshape(1), w_gate, w_up, w_down,
                           layer, tm)
    pos_blocks = pos.reshape(n // tb, tb, TOP_K).transpose(0, 2, 1).reshape(-1)
    return combine_ln(y_sorted, pos_blocks, x1, ln_g[1], ln_b[1], tb)


def even_layer(xp, xs, cache_a_k, cache_a_v, cache_b_k, cache_b_v, w_in, w_out, rel_bias):
    bp, t, d = xp.shape
    bs, n, _ = xs.shape
    past, lb = cache_a_k.shape[1], cache_b_k.shape[1]
    hd = H_A * HEAD_DIM
    proj_p = matmul(xp.reshape(bp * t, d).astype(BF16), w_in).reshape(bp, t, -1)
    proj_s = matmul(xs.reshape(bs * n, d).astype(BF16), w_in).reshape(bs, n, -1)
    nb = hd // HEAD_DIM
    qa_cb, ka_cb, va_cb, qb_cb, kb_cb, vb_cb = (s * nb for s in range(6))
    seg = lambda p, s: p[..., s * hd:(s + 1) * hd]
    oa_p = stick_breaking_attention(proj_p, proj_p, proj_p, qa_cb, ka_cb, va_cb, H_A, 0, 256, 256)
    ob_p = band_attention_prompt(proj_p, qb_cb, kb_cb, vb_cb, rel_bias)
    mp = matmul_concat(oa_p.reshape(bp * t, hd), ob_p.reshape(bp * t, hd), w_out)
    bk_s = 256
    tk_a = -(-(past + n) // bk_s) * bk_s
    cat = lambda c, new, tot: jnp.concatenate(
        [c.reshape(bs, c.shape[1], hd), new, jnp.zeros((bs, tot - c.shape[1] - n, hd), F32)], 1)
    ka_s, va_s, kb_s, vb_s = seg(proj_s, 1), seg(proj_s, 2), seg(proj_s, 4), seg(proj_s, 5)
    oa_s = stick_breaking_attention(proj_s, cat(cache_a_k, ka_s, tk_a), cat(cache_a_v, va_s, tk_a),
                                    qa_cb, 0, 0, H_A, past, n, bk_s)
    ck = -(-(lb + n) // LANES) * LANES
    q_pos = past + np.arange(n)
    k_pos = np.concatenate([np.arange(past - lb, past), q_pos, np.full((ck - lb - n,), -1)])
    qc, kc = q_pos[:, None] // CHUNK, k_pos[None, :] // CHUNK
    valid = (kc <= qc) & (kc >= qc - N_LEFT_CHUNKS) & (k_pos[None, :] >= 0)
    rel = np.clip(q_pos[:, None] - k_pos[None, :], -REL_CLIP, REL_CLIP) + REL_CLIP
    bias_s = jnp.where(jnp.asarray(valid)[None], rel_bias.astype(F32)[:, rel], NEG)
    ob_s = band_attention_core(proj_s, cat(cache_b_k, kb_s, ck), cat(cache_b_v, vb_s, ck), qb_cb, bias_s)
    ms = matmul_concat(oa_s.reshape(bs * n, hd), ob_s.reshape(bs * n, hd), w_out)
    keep = min(BAND_REACH, t)
    heads = lambda a: a.reshape(a.shape[0], a.shape[1], H_A, HEAD_DIM)
    caches = (heads(seg(proj_p, 1)), heads(seg(proj_p, 2)), heads(ka_s), heads(va_s),
              heads(seg(proj_p, 4)[:, t - keep:]), heads(seg(proj_p, 5)[:, t - keep:]),
              heads(kb_s), heads(vb_s))
    return mp, ms, caches


INT_MIN = np.int32(-2 ** 31)


def _rope_kernel(qc_ref, kc_ref, vc_ref, qi_ref, ki_ref, wi_ref, c128_ref, s128_ref, c64_ref, s64_ref,
                 qc_o, kc_o, kcb_o, vcb_o, qi_o, ki_o, kib_o, wi_o):
    c128, s128, c64, s64 = c128_ref[...], s128_ref[...], c64_ref[...], s64_ref[...]
    lane = lax.broadcasted_iota(I32, c64.shape, 1)
    first_half = (lane % D_IDX) < (D_IDX // 2)

    def rot128(x):
        return x * c128 + pltpu.roll(x, HEAD_DIM // 2, 1) * s128

    def rot64(x):
        partner = jnp.where(first_half, pltpu.roll(x, LANES - D_IDX // 2, 1), pltpu.roll(x, D_IDX // 2, 1))
        return x * c64 + partner * s64

    for h in range(H_C):
        sl = slice(h * HEAD_DIM, (h + 1) * HEAD_DIM)
        qc_o[:, sl] = rot128(qc_ref[:, sl]).astype(BF16)
    for h in range(KV_C):
        sl = slice(h * HEAD_DIM, (h + 1) * HEAD_DIM)
        r = rot128(kc_ref[:, sl])
        kc_o[:, sl] = r
        kcb_o[:, sl] = r.astype(BF16)
    vcb_o[...] = vc_ref[...].astype(BF16)
    for j in range(H_IDX * D_IDX // LANES):
        r = rot64(qi_ref[:, j * LANES:(j + 1) * LANES]).astype(BF16)
        qi_o[2 * j] = r[:, :D_IDX]
        qi_o[2 * j + 1] = r[:, D_IDX:]
    r = rot64(ki_ref[...])
    ki_o[...] = r
    kib_o[...] = r.astype(BF16)
    wi_o[...] = wi_ref[...]


def rope_tables(pos):
    def tab(half, reps):
        inv = ROPE_THETA ** (-jnp.arange(half, dtype=F32) / half)
        ang = pos.astype(F32)[:, None] * inv[None, :]
        c, s = jnp.cos(ang), jnp.sin(ang)
        return jnp.tile(jnp.concatenate([c, c], 1), (1, reps)), jnp.tile(jnp.concatenate([-s, s], 1), (1, reps))
    c128, s128 = tab(HEAD_DIM // 2, 1)
    c64, s64 = tab(D_IDX // 2, 2)
    return c128, s128, c64, s64


def rope_split(proj, pos, cols, tb=256):
    n = proj.shape[0]
    assert n % tb == 0
    tabs = rope_tables(pos)
    dq, dk, di = H_C * HEAD_DIM, KV_C * HEAD_DIM, H_IDX * D_IDX
    spec = lambda w, c0: pl.BlockSpec((tb, w), lambda i: (i, c0 // w))
    row = lambda w: pl.BlockSpec((tb, w), lambda i: (i, 0))
    return pl.pallas_call(
        _rope_kernel,
        grid=(n // tb,),
        in_specs=[spec(dq, cols['qc']), spec(dk, cols['kc']), spec(dk, cols['vc']), spec(di, cols['qi']),
                  spec(LANES, cols['ki']), spec(LANES, cols['wi'])] + [row(LANES)] * 4,
        out_specs=[row(dq), row(dk), row(dk), row(dk),
                   pl.BlockSpec((H_IDX, tb, D_IDX), lambda i: (0, i, 0)), row(LANES), row(LANES), row(LANES)],
        out_shape=[jax.ShapeDtypeStruct((n, dq), BF16), jax.ShapeDtypeStruct((n, dk), F32),
                   jax.ShapeDtypeStruct((n, dk), BF16), jax.ShapeDtypeStruct((n, dk), BF16),
                   jax.ShapeDtypeStruct((H_IDX, n, D_IDX), BF16), jax.ShapeDtypeStruct((n, LANES), F32),
                   jax.ShapeDtypeStruct((n, LANES), BF16), jax.ShapeDtypeStruct((n, LANES), F32)],
        compiler_params=_params("arbitrary"),
        name="rope_split",
    )(proj, proj, proj, proj, proj, proj, *tabs)


def _dsa_kernel(qc_ref, qi_ref, wi_ref, kc_ref, vc_ref, ki_ref, tri_ref, o_ref, key_ref, mask_ref, *,
                bq, bk, q_off, s_real, s_pad, topk, scale):
    qi_blk = pl.program_id(1)
    r0 = qi_blk * bq
    adm_end = ((q_off + r0 + bq - 1) // CHUNK + 1) * CHUNK
    n_kb = jnp.minimum((adm_end + bk - 1) // bk, s_pad // bk)
    q_pos = q_off + r0 + lax.broadcasted_iota(I32, (bq, bk), 0)
    col = lax.broadcasted_iota(I32, (bq, bk), 1)
    kf = float(topk)

    def admissible(start):
        k_pos = col + start
        return ((k_pos // CHUNK) <= (q_pos // CHUNK)) & (k_pos < s_real)

    def lane_fold(x):
        acc = x[:, :LANES]
        for j in range(1, bk // LANES):
            acc = acc + x[:, j * LANES:(j + 1) * LANES]
        return acc

    def count(pred_fn):
        def body(kb, acc):
            start = pl.multiple_of(kb * bk, bk)
            blk = key_ref[:, pl.ds(start, bk)]
            return acc + lane_fold(jnp.where(pred_fn(blk), 1.0, 0.0))
        part = lax.fori_loop(0, n_kb, body, jnp.zeros((bq, LANES), F32))
        return jnp.sum(part, axis=1, keepdims=True)

    wi = wi_ref[0]
    wcols = [wi[:, h:h + 1] for h in range(H_IDX)]

    def score_body(kb, c):
        start = pl.multiple_of(kb * bk, bk)
        ki_blk = ki_ref[0, pl.ds(start, bk), :]
        acc = jnp.zeros((bq, bk), F32)
        for h in range(H_IDX):
            lg = lax.dot_general(qi_ref[h], ki_blk, (((1,), (1,)), ((), ())), preferred_element_type=F32)
            acc = acc + wcols[h] * jnp.maximum(lg, 0.0)
        bits = pltpu.bitcast(acc + 0.0, I32)
        key = jnp.where(bits < 0, bits ^ 0x7FFFFFFF, bits)
        key_ref[:, pl.ds(start, bk)] = jnp.where(admissible(start), key, INT_MIN)
        return c
    lax.fori_loop(0, n_kb, score_body, 0)

    prefix = jnp.where(count(lambda b: b >= 0) >= kf, 0, INT_MIN).astype(I32)

    def bit_body(it, prefix):
        cand = prefix | jnp.left_shift(jnp.int32(1), 30 - it)
        return jnp.where(count(lambda b: b >= cand) >= kf, cand, prefix)
    thr = lax.fori_loop(0, 31, bit_body, prefix)

    need = kf - count(lambda b: b > thr)
    tri = tri_ref[...]

    def mask_body(kb, run):
        start = pl.multiple_of(kb * bk, bk)
        blk = key_ref[:, pl.ds(start, bk)]
        tie = blk == thr
        tie_f = jnp.where(tie, 1.0, 0.0)
        before = jnp.dot(tie_f.astype(BF16), tri, preferred_element_type=F32) + run
        sel = ((blk > thr) | (tie & (before < need))) & admissible(start)
        mask_ref[:, pl.ds(start, bk)] = jnp.where(sel, 0.0, NEG)
        return run + jnp.sum(tie_f, axis=1, keepdims=True)
    lax.fori_loop(0, n_kb, mask_body, jnp.zeros((bq, 1), F32))

    rep = H_C // KV_C
    for g in range(KV_C):
        qg = jnp.concatenate([qc_ref[0, :, (g * rep + r) * HEAD_DIM:(g * rep + r + 1) * HEAD_DIM]
                              for r in range(rep)], axis=0)

        def att_body(kb, carry):
            m, l, acc = carry
            start = pl.multiple_of(kb * bk, bk)
            kblk = kc_ref[0, pl.ds(start, bk), g * HEAD_DIM:(g + 1) * HEAD_DIM]
            vblk = vc_ref[0, pl.ds(start, bk), g * HEAD_DIM:(g + 1) * HEAD_DIM]
            mk = mask_ref[:, pl.ds(start, bk)]
            s = lax.dot_general(qg, kblk, (((1,), (1,)), ((), ())), preferred_element_type=F32) * scale
            s = s + jnp.concatenate([mk] * rep, axis=0)
            m_new = jnp.maximum(m, jnp.max(s, axis=1, keepdims=True))
            alpha = jnp.exp(m - m_new)
            p = jnp.exp(s - m_new)
            l = alpha * l + jnp.sum(p, axis=1, keepdims=True)
            acc = alpha * acc + jnp.dot(p.astype(BF16), vblk, preferred_element_type=F32)
            return m_new, l, acc
        m0 = jnp.full((rep * bq, 1), NEG, F32)
        _, l, acc = lax.fori_loop(0, n_kb, att_body,
                                  (m0, jnp.zeros((rep * bq, 1), F32), jnp.zeros((rep * bq, HEAD_DIM), F32)))
        o = acc / l
        for r in range(rep):
            o_ref[0, :, (g * rep + r) * HEAD_DIM:(g * rep + r + 1) * HEAD_DIM] = (
                o[r * bq:(r + 1) * bq].astype(o_ref.dtype))


def dsa_attention(qc, qi_hm, wi, kc, vc, ki, q_off, s_real, topk, bq, bk):
    b, tq, _ = qc.shape
    s_pad = kc.shape[1]
    assert tq % bq == 0 and s_pad % bk == 0 and bk % LANES == 0
    nq = tq // bq
    tri = jnp.asarray(np.triu(np.ones((bk, bk), np.float32), 1), BF16)
    kern = functools.partial(_dsa_kernel, bq=bq, bk=bk, q_off=q_off, s_real=s_real, s_pad=s_pad,
                             topk=topk, scale=HEAD_DIM ** -0.5)
    once = dict(pipeline_mode=pl.Buffered(1))
    return pl.pallas_call(
        kern,
        grid=(b, nq),
        in_specs=[pl.BlockSpec((1, bq, H_C * HEAD_DIM), lambda b_, i: (b_, i, 0)),
                  pl.BlockSpec((H_IDX, bq, D_IDX), lambda b_, i: (0, b_ * nq + i, 0)),
                  pl.BlockSpec((1, bq, LANES), lambda b_, i: (b_, i, 0)),
                  pl.BlockSpec((1, s_pad, KV_C * HEAD_DIM), lambda b_, i: (b_, 0, 0), **once),
                  pl.BlockSpec((1, s_pad, KV_C * HEAD_DIM), lambda b_, i: (b_, 0, 0), **once),
                  pl.BlockSpec((1, s_pad, D_IDX), lambda b_, i: (b_, 0, 0), **once),
                  pl.BlockSpec((bk, bk), lambda b_, i: (0, 0))],
        out_specs=pl.BlockSpec((1, bq, H_C * HEAD_DIM), lambda b_, i: (b_, i, 0)),
        out_shape=jax.ShapeDtypeStruct((b, tq, H_C * HEAD_DIM), BF16),
        scratch_shapes=[pltpu.VMEM((bq, s_pad), I32), pltpu.VMEM((bq, s_pad), F32)],
        compiler_params=_params("arbitrary", "arbitrary"),
        name="dsa_attention",
    )(qc, qi_hm, wi, kc, vc, ki, tri)


HIGHEST = lax.Precision.HIGHEST
HALO = 8


def _gdn_prep_kernel(x_ref, halo_ref, cw_ref, a_ref, b_ref, alog_ref, dtb_ref, o_ref, g_ref, beta_ref, *,
                     heads_per_blk):
    j = pl.program_id(1)
    x = x_ref[...]
    tb = x.shape[0]
    halo = halo_ref[0]
    row = lax.broadcasted_iota(I32, halo.shape, 0)
    acc = x * cw_ref[CONV_W - 1:CONV_W, :]
    for k in range(1, CONV_W):
        rolled = pltpu.roll(x, k, 0)
        top = jnp.where(row < k, pltpu.roll(halo, k, 0), rolled[:HALO])
        shifted = jnp.concatenate([top, rolled[HALO:]], axis=0) if tb > HALO else top
        acc = acc + shifted * cw_ref[CONV_W - 1 - k:CONV_W - k, :]
    y = acc * (1.0 / (1.0 + jnp.exp(-acc)))
    seg = j // (H_D // heads_per_blk)
    scale = jnp.where(seg == 0, DK_D ** -0.5, 1.0)
    for h in range(heads_per_blk):
        sl = slice(h * LANES, (h + 1) * LANES)
        t = y[:, sl]
        nrm = t * lax.rsqrt(jnp.sum(t * t, axis=1, keepdims=True) + RMS_EPS) * scale
        o_ref[:, sl] = jnp.where(seg == 2, t, nrm)

    @pl.when(j == 0)
    def _():
        z = a_ref[...] + dtb_ref[...]
        softplus = jnp.maximum(z, 0.0) + jnp.log(1.0 + jnp.exp(-jnp.abs(z)))
        g_ref[...] = -jnp.exp(alog_ref[...]) * softplus
        beta_ref[...] = 1.0 / (1.0 + jnp.exp(-b_ref[...]))


def gdn_prep(proj, n_rows, halo, conv_w, a_log_pad, dt_bias_pad, qkv_col, a_col, b_col, tb):
    cb = 512
    hpb = cb // LANES
    kern = functools.partial(_gdn_prep_kernel, heads_per_blk=hpb)
    vec = pl.BlockSpec((1, LANES), lambda i, j: (0, 0))
    return pl.pallas_call(
        kern,
        grid=(n_rows // tb, QKV_D // cb),
        in_specs=[pl.BlockSpec((tb, cb), lambda i, j: (i, qkv_col // cb + j)),
                  pl.BlockSpec((1, HALO, cb), lambda i, j: (i, 0, j)),
                  pl.BlockSpec((CONV_W, cb), lambda i, j: (0, j)),
                  pl.BlockSpec((tb, LANES), lambda i, j: (i, a_col // LANES)),
                  pl.BlockSpec((tb, LANES), lambda i, j: (i, b_col // LANES)), vec, vec],
        out_specs=[pl.BlockSpec((tb, cb), lambda i, j: (i, j)),
                   pl.BlockSpec((tb, LANES), lambda i, j: (i, 0)),
                   pl.BlockSpec((tb, LANES), lambda i, j: (i, 0))],
        out_shape=[jax.ShapeDtypeStruct((n_rows, QKV_D), F32), jax.ShapeDtypeStruct((n_rows, LANES), F32),
                   jax.ShapeDtypeStruct((n_rows, LANES), F32)],
        compiler_params=_params("arbitrary", "arbitrary"),
        name="gdn_prep",
    )(proj, halo, conv_w, proj, proj, a_log_pad, dt_bias_pad)


def _split3(x):
    a = x.astype(BF16)
    r = x - a.astype(F32)
    b = r.astype(BF16)
    c = (r - b.astype(F32)).astype(BF16)
    return a, b, c


def _dot01(m01, x):
    return sum(jnp.dot(m01, p, preferred_element_type=F32) for p in _split3(x))


def _gdn_local_kernel(q_ref, k_ref, v_ref, g_ref, beta_ref, solv_ref, solk_ref, qe_ref, kd_ref, qkd_ref,
                      egl_ref, *, c, hb):
    hg = pl.program_id(1)
    ri = lax.broadcasted_iota(I32, (c, c), 0)
    ci = lax.broadcasted_iota(I32, (c, c), 1)
    incl, strict, eye = ri >= ci, ri > ci, ri == ci
    tril = jnp.where(incl, 1.0, 0.0).astype(BF16)
    ones = jnp.ones((c, c), BF16)
    gcum = _dot01(tril, g_ref[...])
    beta_all = beta_ref[...]
    egl_ref[0] = jnp.exp(gcum[c - 1:c, :])
    lane = lax.broadcasted_iota(I32, (c, LANES), 1)
    for hh in range(hb):
        pick = lane == hg * hb + hh
        gc = jnp.sum(jnp.where(pick, gcum, 0.0), axis=1, keepdims=True)
        bc = jnp.sum(jnp.where(pick, beta_all, 0.0), axis=1, keepdims=True)
        g_last = gc[c - 1:c, :]
        grow = _dot01(ones, jnp.where(eye, gc, 0.0))
        decay = jnp.where(incl, jnp.exp(jnp.where(incl, gc - grow, 0.0)), 0.0)
        sl = slice(hh * LANES, (hh + 1) * LANES)
        q, k, v = q_ref[:, sl], k_ref[:, sl], v_ref[:, sl]
        qb, kb = q.astype(BF16), k.astype(BF16)
        kk = lax.dot_general(kb, kb, (((1,), (1,)), ((), ())), preferred_element_type=F32)
        nmat = jnp.where(strict, bc * kk * decay, 0.0)
        y = jnp.concatenate([v * bc, k * (bc * jnp.exp(gc))], axis=1)
        y = y - jnp.dot(nmat, y, precision=HIGHEST, preferred_element_type=F32)
        p = nmat
        for _ in range(int(np.log2(c)) - 1):
            p = jnp.dot(p, p, precision=HIGHEST, preferred_element_type=F32)
            y = y + jnp.dot(p, y, precision=HIGHEST, preferred_element_type=F32)
        solv_ref[:, sl] = y[:, :DV_D]
        solk_ref[:, sl] = y[:, DV_D:].astype(BF16)
        qk = lax.dot_general(qb, kb, (((1,), (1,)), ((), ())), preferred_element_type=F32)
        qkd_ref[0, hh] = (qk * decay).astype(BF16)
        qe_ref[:, sl] = (q * jnp.exp(gc)).astype(BF16)
        kd_ref[:, sl] = (k * jnp.exp(g_last - gc)).astype(BF16)


def gdn_local(qkv, g, beta, c, hb=4):
    n = qkv.shape[0]
    nc = n // c
    dh = H_D * DK_D
    kern = functools.partial(_gdn_local_kernel, c=c, hb=hb)
    w = hb * LANES
    seg = lambda s: pl.BlockSpec((c, w), lambda i, j: (i, s * (H_D // hb) + j))
    col = pl.BlockSpec((c, w), lambda i, j: (i, j))
    meta = pl.BlockSpec((c, LANES), lambda i, j: (i, 0))
    return pl.pallas_call(
        kern,
        grid=(nc, H_D // hb),
        in_specs=[seg(0), seg(1), seg(2), meta, meta],
        out_specs=[col, col, col, col,
                   pl.BlockSpec((1, hb, c, c), lambda i, j: (i, j, 0, 0)),
                   pl.BlockSpec((1, 1, LANES), lambda i, j: (i, 0, 0))],
        out_shape=[jax.ShapeDtypeStruct((n, dh), F32), jax.ShapeDtypeStruct((n, dh), BF16),
                   jax.ShapeDtypeStruct((n, dh), BF16), jax.ShapeDtypeStruct((n, dh), BF16),
                   jax.ShapeDtypeStruct((nc, H_D, c, c), BF16),
                   jax.ShapeDtypeStruct((nc, 1, LANES), F32)],
        compiler_params=_params("arbitrary", "arbitrary"),
        name="gdn_local",
    )(qkv, qkv, qkv, g, beta)


def _gdn_scan_kernel(s0_ref, solv_ref, solk_ref, qe_ref, kd_ref, qkd_ref, egl_ref, gate_ref, nw_ref,
                     od_ref, sout_ref, s_ref):
    ci = pl.program_id(1)

    @pl.when(ci == 0)
    def _():
        s_ref[...] = s0_ref[0]

    egl = egl_ref[0]
    nw = nw_ref[...]
    for h in range(H_D):
        sl = slice(h * LANES, (h + 1) * LANES)
        s_old = s_ref[h]
        sb = s_old.astype(BF16)
        u = solv_ref[:, sl] - jnp.dot(solk_ref[:, sl], sb, preferred_element_type=F32)
        ub = u.astype(BF16)
        o = (jnp.dot(qe_ref[:, sl], sb, preferred_element_type=F32)
             + jnp.dot(qkd_ref[0, h], ub, preferred_element_type=F32))
        s_ref[h] = s_old * egl[:, h:h + 1] + lax.dot_general(
            kd_ref[:, sl], ub, (((0,), (0,)), ((), ())), preferred_element_type=F32)
        o = o * lax.rsqrt(jnp.mean(o * o, axis=1, keepdims=True) + RMS_EPS) * nw
        gt = gate_ref[:, sl]
        od_ref[:, sl] = (o * (gt * (1.0 / (1.0 + jnp.exp(-gt))))).astype(od_ref.dtype)

    @pl.when(ci == pl.num_programs(1) - 1)
    def _():
        sout_ref[0] = s_ref[...]


def gdn_scan(s0, solv, solk, qe, kd, qkd, egl, gate_arr, gate_col, norm_w, c):
    b = s0.shape[0]
    n = solv.shape[0]
    cps = n // b // c
    dh = H_D * DV_D
    col = pl.BlockSpec((c, dh), lambda b_, i: (b_ * cps + i, 0))
    st = pl.BlockSpec((1, H_D, DK_D, DV_D), lambda b_, i: (b_, 0, 0, 0))
    return pl.pallas_call(
        _gdn_scan_kernel,
        grid=(b, cps),
        in_specs=[st, col, col, col, col,
                  pl.BlockSpec((1, H_D, c, c), lambda b_, i: (b_ * cps + i, 0, 0, 0)),
                  pl.BlockSpec((1, 1, LANES), lambda b_, i: (b_ * cps + i, 0, 0)),
                  pl.BlockSpec((c, dh), lambda b_, i: (b_ * cps + i, gate_col // dh)),
                  pl.BlockSpec((1, LANES), lambda b_, i: (0, 0))],
        out_specs=[col, st],
        out_shape=[jax.ShapeDtypeStruct((n, dh), BF16), jax.ShapeDtypeStruct(s0.shape, F32)],
        scratch_shapes=[pltpu.VMEM((H_D, DK_D, DV_D), F32)],
        compiler_params=_params("arbitrary", "arbitrary"),
        name="gdn_scan",
    )(s0, solv, solk, qe, kd, qkd, egl, gate_arr, norm_w.reshape(1, LANES))


def gdn_mixer(proj, n_rows, cols, halo, s0, conv_w, a_log, dt_bias, norm_w, c, tb):
    pad = lambda v: jnp.pad(v.astype(F32), (0, LANES - H_D)).reshape(1, LANES)
    qkv, g, beta = gdn_prep(proj, n_rows, halo, conv_w, pad(a_log), pad(dt_bias),
                            cols['qkv'], cols['a'], cols['b'], tb)
    solv, solk, qe, kd, qkd, egl = gdn_local(qkv, g, beta, c)
    return gdn_scan(s0, solv, solk, qe, kd, qkd, egl, proj, cols['g'], norm_w, c)


ODD_COLS = {}
_orig, _new = 0, 0
_order = dict(qc=0, kc=1, vc=2, qi=3, ki=4, wi=5, qkv=6, a=7, b=8, g=9)
_starts = np.concatenate([[0], np.cumsum(ODD_SIZES)])
for _name in ('qc', 'g', 'kc', 'vc', 'qi', 'qkv', 'ki', 'wi', 'a', 'b'):
    _w = ODD_SIZES[_order[_name]]
    ODD_COLS[_name] = (_new, int(_starts[_order[_name]]), _w)
    _new += -(-_w // LANES) * LANES
ODD_WIDTH = _new


def reorder_w_in_odd(w):
    parts = []
    for name in ('qc', 'g', 'kc', 'vc', 'qi', 'qkv', 'ki', 'wi', 'a', 'b'):
        _, o0, wd = ODD_COLS[name]
        parts.append(jnp.pad(w[:, o0:o0 + wd], ((0, 0), (0, (-wd) % LANES))))
    return jnp.concatenate(parts, axis=1).astype(BF16)


def odd_layer(x, n_p, bs, n, cache_c_k, cache_c_v, cache_c_idx_k, state_d_ssm, state_d_conv,
              w_in_r, w_out, conv_w, a_log, dt_bias, norm_w):
    cols = {k: v[0] for k, v in ODD_COLS.items()}
    past = cache_c_k.shape[1]
    assert n >= CONV_W - 1 and n_p % 256 == 0
    proj = matmul(x.astype(BF16), w_in_r, tn=ODD_WIDTH // 10)
    pos = jnp.concatenate([jnp.arange(n_p), jnp.tile(past + jnp.arange(n), bs)])
    qc_b, kc_f, kc_b, vc_b, qi_hm, ki_f, ki_b, wi_f = rope_split(proj, pos, cols)
    dk = KV_C * HEAD_DIM
    topk_p = min(TOPK_MAX, n_p // 4)
    oc_p = dsa_attention(qc_b[:n_p][None], qi_hm[:, :n_p], wi_f[:n_p][None], kc_b[:n_p][None],
                         vc_b[:n_p][None], ki_b[:n_p, :D_IDX][None], 0, n_p, topk_p, 128, 512)
    bk_s = 384
    s_real = past + n
    s_pad = -(-s_real // bk_s) * bk_s
    cat = lambda c, new: jnp.concatenate(
        [c.reshape(bs, past, -1).astype(BF16), new, jnp.zeros((bs, s_pad - s_real, new.shape[-1]), BF16)], 1)
    oc_s = dsa_attention(qc_b[n_p:].reshape(bs, n, -1), qi_hm[:, n_p:], wi_f[n_p:].reshape(bs, n, LANES),
                         cat(cache_c_k, kc_b[n_p:].reshape(bs, n, dk)),
                         cat(cache_c_v, vc_b[n_p:].reshape(bs, n, dk)),
                         cat(cache_c_idx_k, ki_b[n_p:, :D_IDX].reshape(bs, n, D_IDX)),
                         past, s_real, min(TOPK_MAX, s_real // 4), n, bk_s)
    tb_p = 256
    q0 = cols['qkv']
    prev_rows = (np.arange(1, n_p // tb_p)[:, None] * tb_p - HALO + np.arange(HALO)[None, :]).reshape(-1)
    halo_p = jnp.concatenate([jnp.zeros((1, HALO, QKV_D), F32),
                              proj[prev_rows, q0:q0 + QKV_D].reshape(-1, HALO, QKV_D)], 0)
    s0_p = jnp.zeros((1, H_D, DK_D, DV_D), F32)
    od_p, s_p = gdn_mixer(proj, n_p, cols, halo_p, s0_p, conv_w, a_log, dt_bias, norm_w, CHUNK, tb_p)
    proj_s = proj[n_p:]
    halo_s = jnp.concatenate([jnp.zeros((bs, HALO - (CONV_W - 1), QKV_D), F32), state_d_conv], 1)
    od_s, s_s = gdn_mixer(proj_s, bs * n, cols, halo_s, state_d_ssm, conv_w, a_log, dt_bias, norm_w, n, n)
    dh = H_C * HEAD_DIM
    mp = matmul_concat(oc_p.reshape(n_p, dh), od_p, w_out)
    ms = matmul_concat(oc_s.reshape(bs * n, dh), od_s, w_out)
    v0 = cols['vc']
    qkv_s = proj_s[:, q0:q0 + QKV_D].reshape(bs, n, QKV_D)
    caches = (kc_f[:n_p].reshape(1, n_p, KV_C, HEAD_DIM), proj[:n_p, v0:v0 + dk].reshape(1, n_p, KV_C, HEAD_DIM),
              ki_f[:n_p, :D_IDX][None],
              kc_f[n_p:].reshape(bs, n, KV_C, HEAD_DIM), proj_s[:, v0:v0 + dk].reshape(bs, n, KV_C, HEAD_DIM),
              ki_f[n_p:, :D_IDX].reshape(bs, n, D_IDX),
              s_p, proj[n_p - (CONV_W - 1):n_p, q0:q0 + QKV_D][None],
              s_s, qkv_s[:, n - (CONV_W - 1):])
    return mp, ms, caches


def kernel(x_prompt, x_sample, cache_a_k, cache_a_v, cache_b_k, cache_b_v, cache_c_k, cache_c_v,
           cache_c_idx_k, state_d_ssm, state_d_conv, w_in_even, w_out_even, b_rel_bias, w_in_odd,
           w_out_odd, d_conv_w, d_a_log, d_dt_bias, d_norm_w, w_router, router_bias, moe_w_gate,
           moe_w_up, moe_w_down, ln_g, ln_b):
    bp, t, d = x_prompt.shape
    bs, n, _ = x_sample.shape
    assert bp == 1
    n_p = bp * t
    w_in_even = w_in_even.astype(BF16)
    w_out_even = w_out_even.astype(BF16)
    w_in_odd_r = reorder_w_in_odd(w_in_odd)
    w_out_odd = w_out_odd.astype(BF16)
    wr_pad = jnp.pad(w_router.astype(BF16), ((0, 0), (0, LANES - N_EXPERTS)))
    rb_pad = jnp.pad(router_bias.astype(F32), (0, LANES - N_EXPERTS)).reshape(1, LANES)
    x = jnp.concatenate([x_prompt.reshape(n_p, d), x_sample.reshape(bs * n, d)], 0)
    for layer in range(DEPTH):
        if layer % 2 == 0:
            xp, xs = x[:n_p].reshape(bp, t, d), x[n_p:].reshape(bs, n, d)
            mp, ms, (a_k_p, a_v_p, a_k_s, a_v_s, b_k_p, b_v_p, b_k_s, b_v_s) = even_layer(
                xp, xs, cache_a_k, cache_a_v, cache_b_k, cache_b_v, w_in_even, w_out_even, b_rel_bias)
        else:
            mp, ms, (c_k_p, c_v_p, c_i_p, c_k_s, c_v_s, c_i_s, d_s_p, d_c_p, d_s_s, d_c_s) = odd_layer(
                x, n_p, bs, n, cache_c_k, cache_c_v, cache_c_idx_k, state_d_ssm, state_d_conv,
                w_in_odd_r, w_out_odd, d_conv_w, d_a_log, d_dt_bias, d_norm_w)
        m = jnp.concatenate([mp.reshape(n_p, d), ms.reshape(bs * n, d)], 0)
        x = moe_block(x, m, ln_g[layer], ln_b[layer], wr_pad, rb_pad,
                      moe_w_gate, moe_w_up, moe_w_down, layer)
    xp, xs = x[:n_p].reshape(bp, t, d), x[n_p:].reshape(bs, n, d)
    return (xp, xs, a_k_p, a_v_p, a_k_s, a_v_s, b_k_p, b_v_p, b_k_s, b_v_s,
            c_k_p, c_v_p, c_i_p, c_k_s, c_v_s, c_i_s, d_s_p, d_c_p, d_s_s, d_c_s)


def _old_project(x, w_bf16):
    b, t, d = x.shape
    n = w_bf16.shape[1]
    n_pad = (-n) % 1024
    if n_pad:
        w_bf16 = jnp.pad(w_bf16, ((0, 0), (0, n_pad)))
    y = matmul(x.reshape(b * t, d).astype(BF16), w_bf16)
    return y[:, :n].reshape(b, t, n)


def split_cols(h, sizes):
    cuts, acc = [], 0
    for s in sizes[:-1]:
        acc += s
        cuts.append(acc)
    return jnp.split(h, cuts, axis=-1)


def rope(x, pos):
    half = x.shape[-1] // 2
    inv = ROPE_THETA ** (-jnp.arange(half, dtype=F32) / half)
    ang = pos.astype(F32)[:, None] * inv[None, :]
    cos, sin = jnp.cos(ang)[None, :, None, :], jnp.sin(ang)[None, :, None, :]
    xf = x.astype(F32)
    x1, x2 = xf[..., :half], xf[..., half:]
    return jnp.concatenate([x1 * cos - x2 * sin, x2 * cos + x1 * sin], -1).astype(x.dtype)


def dsa_attend(qc, qi, wi, q_pos, kc, vc, ki, k_pos, topk):
    B, Q = qc.shape[:2]
    logits = jnp.einsum('bqhe,bse->bqhs', qi.astype(F32), ki.astype(F32))
    score = jnp.einsum('bqh,bqhs->bqs', wi.astype(F32), jax.nn.relu(logits))
    adm = jnp.broadcast_to(((k_pos[None, :] // CHUNK) <= (q_pos[:, None] // CHUNK))[None], score.shape)
    _, sel = lax.top_k(jnp.where(adm, score, NEG), topk)
    sel_ok = jnp.take_along_axis(adm, sel, axis=-1)
    gather = jax.vmap(lambda rows, i: rows[i])
    ks, vs = gather(kc, sel), gather(vc, sel)
    qg = qc.reshape(B, Q, KV_C, H_C // KV_C, HEAD_DIM).astype(F32)
    s = jnp.einsum('bqgrd,bqkgd->bqgrk', qg, ks.astype(F32)) * (HEAD_DIM ** -0.5)
    s = jnp.where(sel_ok[:, :, None, None, :], s, NEG)
    p = jax.nn.softmax(s, axis=-1)
    o = jnp.einsum('bqgrk,bqkgd->bqgrd', p, vs.astype(F32))
    return o.reshape(B, Q, H_C, HEAD_DIM).astype(qc.dtype)


def causal_conv(raw, buf, conv_w):
    xp = jnp.concatenate([buf, raw], 1)
    out = lax.conv_general_dilated(xp, conv_w[:, None, :], window_strides=(1,), padding='VALID',
                                   dimension_numbers=('NWC', 'WIO', 'NWC'),
                                   feature_group_count=raw.shape[-1])
    return jax.nn.silu(out), xp[:, xp.shape[1] - (CONV_W - 1):]


def l2norm(t):
    return t * lax.rsqrt(jnp.sum(t * t, -1, keepdims=True) + RMS_EPS)


def gdn_inputs(qkv_raw, buf, a_raw, b_raw, conv_w, a_log, dt_bias):
    qkv, new_buf = causal_conv(qkv_raw, buf, conv_w)
    B, T, _ = qkv.shape
    q, k, v = split_cols(qkv, (H_D * DK_D, H_D * DK_D, H_D * DV_D))
    q = l2norm(q.reshape(B, T, H_D, DK_D).astype(F32)) * (DK_D ** -0.5)
    k = l2norm(k.reshape(B, T, H_D, DK_D).astype(F32))
    v = v.reshape(B, T, H_D, DV_D).astype(F32)
    g = -jnp.exp(a_log.astype(F32)) * jax.nn.softplus(a_raw.astype(F32) + dt_bias.astype(F32))
    beta = jax.nn.sigmoid(b_raw.astype(F32))
    return q, k, v, g, beta, new_buf


def gdn_chunk(S, q, k, v, g, beta):
    C = q.shape[1]
    qh, kh, vh = q.transpose(0, 2, 1, 3), k.transpose(0, 2, 1, 3), v.transpose(0, 2, 1, 3)
    G = jnp.cumsum(g, axis=1).transpose(0, 2, 1)
    bt = beta.transpose(0, 2, 1)
    incl = jnp.tril(jnp.ones((C, C), dtype=bool))
    strict = jnp.tril(jnp.ones((C, C), dtype=bool), -1)
    decay = jnp.exp(jnp.where(incl, G[..., :, None] - G[..., None, :], -jnp.inf))
    kk = jnp.einsum('bhtd,bhjd->bhtj', kh, kh)
    a_mat = jnp.where(strict, bt[..., :, None] * kk * decay, 0.0) + jnp.eye(C, dtype=F32)
    rhs = jnp.concatenate([vh * bt[..., None], kh * (bt * jnp.exp(G))[..., None]], -1)
    sol = lax.linalg.triangular_solve(a_mat, rhs, left_side=True, lower=True, unit_diagonal=True)
    u = sol[..., :DV_D] - jnp.einsum('bhtk,bhkv->bhtv', sol[..., DV_D:], S)
    qk = jnp.einsum('bhtd,bhjd->bhtj', qh, kh) * decay
    o = jnp.einsum('bhtk,bhkv->bhtv', qh * jnp.exp(G)[..., None], S) + jnp.einsum('bhtj,bhjv->bhtv', qk, u)
    g_last = G[..., -1]
    S_new = S * jnp.exp(g_last)[..., None, None] + jnp.einsum(
        'bhjk,bhjv->bhkv', kh * jnp.exp(g_last[..., None] - G)[..., None], u)
    return S_new, o.transpose(0, 2, 1, 3)


def gdn_output(o, gate, norm_w):
    B, T = o.shape[:2]
    o = o * lax.rsqrt(jnp.mean(o * o, -1, keepdims=True) + RMS_EPS) * norm_w.astype(F32)
    return (o.reshape(B, T, H_D * DV_D) * jax.nn.silu(gate.astype(F32))).astype(gate.dtype)


def odd_project(x, w_in, pos):
    B, T, _ = x.shape
    qc, kc, vc, qi, ki, wi, qkv_d, a_d, b_d, g_d = split_cols(project(x, w_in), ODD_SIZES)
    qc = rope(qc.reshape(B, T, H_C, HEAD_DIM), pos)
    kc = rope(kc.reshape(B, T, KV_C, HEAD_DIM), pos)
    vc = vc.reshape(B, T, KV_C, HEAD_DIM)
    qi = rope(qi.reshape(B, T, H_IDX, D_IDX), pos)
    ki = rope(ki[:, :, None, :], pos)[:, :, 0]
    return qc, kc, vc, qi, ki, wi, qkv_d, a_d, b_d, g_d


def odd_prompt(x, w_in, w_out, conv_w, a_log, dt_bias, norm_w):
    B, T, _ = x.shape
    pos = jnp.arange(T)
    qc, kc, vc, qi, ki, wi, qkv_d, a_d, b_d, g_d = odd_project(x, w_in, pos)
    nq = T // Q_BLOCK
    topk = min(TOPK_MAX, T // 4)
    blk = lambda t: t.reshape((B, nq, Q_BLOCK) + t.shape[2:]).swapaxes(0, 1)
    oc = lax.map(lambda a: dsa_attend(a[0], a[1], a[2], a[3], kc, vc, ki, pos, topk),
                 (blk(qc), blk(qi), blk(wi), pos.reshape(nq, Q_BLOCK)))
    oc = oc.swapaxes(0, 1).reshape(B, T, H_C * HEAD_DIM)
    buf0 = jnp.zeros((B, CONV_W - 1, QKV_D), qkv_d.dtype)
    q, k, v, g, beta, conv_buf = gdn_inputs(qkv_d, buf0, a_d, b_d, conv_w, a_log, dt_bias)
    nc = T // CHUNK
    chk = lambda t: t.reshape((B, nc, CHUNK) + t.shape[2:]).swapaxes(0, 1)
    S0 = jnp.zeros((B, H_D, DK_D, DV_D), F32)
    S, od = lax.scan(lambda S, c: gdn_chunk(S, *c), S0, (chk(q), chk(k), chk(v), chk(g), chk(beta)))
    od = gdn_output(od.swapaxes(0, 1).reshape(B, T, H_D, DV_D), g_d, norm_w)
    y = project(jnp.concatenate([oc, od], -1), w_out)
    return y, kc, vc, ki, S.astype(x.dtype), conv_buf


def odd_sample(x, cache_c_k, cache_c_v, cache_c_idx_k, state_d_ssm, state_d_conv,
               w_in, w_out, conv_w, a_log, dt_bias, norm_w):
    B, n, _ = x.shape
    past = cache_c_k.shape[1]
    q_pos = past + jnp.arange(n)
    qc, kc, vc, qi, ki, wi, qkv_d, a_d, b_d, g_d = odd_project(x, w_in, q_pos)
    topk = min(TOPK_MAX, (past + n) // 4)
    oc = dsa_attend(qc, qi, wi, q_pos, jnp.concatenate([cache_c_k, kc], 1),
                    jnp.concatenate([cache_c_v, vc], 1), jnp.concatenate([cache_c_idx_k, ki], 1),
                    jnp.arange(past + n), topk).reshape(B, n, H_C * HEAD_DIM)
    q, k, v, g, beta, conv_buf = gdn_inputs(qkv_d, state_d_conv, a_d, b_d, conv_w, a_log, dt_bias)
    S, od = gdn_chunk(state_d_ssm.astype(F32), q, k, v, g, beta)
    od = gdn_output(od, g_d, norm_w)
    y = project(jnp.concatenate([oc, od], -1), w_out)
    return y, kc, vc, ki, S.astype(state_d_ssm.dtype), conv_buf


def _old_kernel(x_prompt, x_sample, cache_a_k, cache_a_v, cache_b_k, cache_b_v, cache_c_k, cache_c_v,
                cache_c_idx_k, state_d_ssm, state_d_conv, w_in_even, w_out_even, b_rel_bias, w_in_odd,
                w_out_odd, d_conv_w, d_a_log, d_dt_bias, d_norm_w, w_router, router_bias, moe_w_gate,
                moe_w_up, moe_w_down, ln_g, ln_b):
    bp, t, d = x_prompt.shape
    bs, n, _ = x_sample.shape
    n_p = bp * t
    w_in_even = w_in_even.astype(BF16)
    w_out_even = w_out_even.astype(BF16)
    w_in_odd = w_in_odd.astype(BF16)
    w_out_odd = w_out_odd.astype(BF16)
    wr_pad = jnp.pad(w_router.astype(BF16), ((0, 0), (0, LANES - N_EXPERTS)))
    rb_pad = jnp.pad(router_bias.astype(F32), (0, LANES - N_EXPERTS)).reshape(1, LANES)
    x = jnp.concatenate([x_prompt.reshape(n_p, d), x_sample.reshape(bs * n, d)], 0)
    for layer in range(DEPTH):
        xp, xs = x[:n_p].reshape(bp, t, d), x[n_p:].reshape(bs, n, d)
        if layer % 2 == 0:
            mp, ms, (a_k_p, a_v_p, a_k_s, a_v_s, b_k_p, b_v_p, b_k_s, b_v_s) = even_layer(
                xp, xs, cache_a_k, cache_a_v, cache_b_k, cache_b_v, w_in_even, w_out_even, b_rel_bias)
        else:
            mp, c_k_p, c_v_p, c_i_p, d_s_p, d_c_p = odd_prompt(xp, w_in_odd, w_out_odd, d_conv_w,
                                                                d_a_log, d_dt_bias, d_norm_w)
            ms, c_k_s, c_v_s, c_i_s, d_s_s, d_c_s = odd_sample(xs, cache_c_k, cache_c_v, cache_c_idx_k,
                                                                state_d_ssm, state_d_conv, w_in_odd,
                                                                w_out_odd, d_conv_w, d_a_log, d_dt_bias,
                                                                d_norm_w)
        m = jnp.concatenate([mp.reshape(n_p, d), ms.reshape(bs * n, d)], 0)
        x = moe_block(x, m, ln_g[layer], ln_b[layer], wr_pad, rb_pad,
                      moe_w_gate, moe_w_up, moe_w_down, layer)
    xp, xs = x[:n_p].reshape(bp, t, d), x[n_p:].reshape(bs, n, d)
    return (xp, xs, a_k_p, a_v_p, a_k_s, a_v_s, b_k_p, b_v_p, b_k_s, b_v_s,
            c_k_p, c_v_p, c_i_p, c_k_s, c_v_s, c_i_s, d_s_p, d_c_p, d_s_s, d_c_s)
```

```python
import functools

import jax
import jax.numpy as jnp
import numpy as np
from jax import lax
from jax.experimental import pallas as pl
from jax.experimental.pallas import tpu as pltpu

D_MODEL = 4096
DEPTH = 2
CHUNK = 64
HEAD_DIM = 128
Q_BLOCK = 128
NEG = -1e30
H_A = 16
H_B = 16
N_LEFT_CHUNKS = 8
BAND_REACH = N_LEFT_CHUNKS * CHUNK
REL_CLIP = 128
EVEN_SIZES = (H_A * HEAD_DIM,) * 3 + (H_B * HEAD_DIM,) * 3
H_C = 16
KV_C = 4
H_IDX = 16
D_IDX = 64
TOPK_MAX = 256
H_D = 16
DK_D = 128
DV_D = 128
CONV_W = 4
QKV_D = H_D * (2 * DK_D + DV_D)
ODD_SIZES = (H_C * HEAD_DIM, KV_C * HEAD_DIM, KV_C * HEAD_DIM, H_IDX * D_IDX, D_IDX, H_IDX,
             QKV_D, H_D, H_D, H_D * DV_D)
N_EXPERTS = 16
N_GROUPS = 4
EXPERTS_PER_GROUP = N_EXPERTS // N_GROUPS
TOP_K = 2
D_EXPERT = 1024
ROPE_THETA = 10000.0
LN_EPS = 1e-5
RMS_EPS = 1e-6
DEEPNORM_ALPHA = (2 * DEPTH) ** 0.25

F32 = jnp.float32
BF16 = jnp.bfloat16
I32 = jnp.int32

F32_EXP_UNDERFLOW = -105.0
LANES = 128
VMEM_LIMIT_BYTES = 56 * 1024 * 1024


def _params(*sem):
    return pltpu.CompilerParams(dimension_semantics=sem, vmem_limit_bytes=VMEM_LIMIT_BYTES)


def _cast_rows(src_ref, dst_ref, dst_off=0, chunk=256):
    rows = src_ref.shape[0]
    assert rows % chunk == 0 and dst_off % chunk == 0

    def body(c, carry):
        r = pl.multiple_of(c * chunk, chunk)
        dst_ref[pl.ds(dst_off + r, chunk), :] = src_ref[pl.ds(r, chunk), :].astype(dst_ref.dtype)
        return carry
    lax.fori_loop(0, rows // chunk, body, 0)


def _matmul_kernel(x_ref, w_ref, o_ref):
    o_ref[...] = jnp.dot(x_ref[...], w_ref[...], preferred_element_type=F32)


def matmul(x, w, tm=512, tn=1024):
    m, k = x.shape
    n = w.shape[1]
    tm = min(tm, m)
    tn = min(tn, n)
    assert m % tm == 0 and n % tn == 0
    return pl.pallas_call(
        _matmul_kernel,
        grid=(n // tn, m // tm),
        in_specs=[pl.BlockSpec((tm, k), lambda j, i: (i, 0)),
                  pl.BlockSpec((k, tn), lambda j, i: (0, j))],
        out_specs=pl.BlockSpec((tm, tn), lambda j, i: (i, j)),
        out_shape=jax.ShapeDtypeStruct((m, n), F32),
        compiler_params=_params("arbitrary", "arbitrary"),
        name="proj_matmul",
    )(x, w)


def _matmul2_kernel(xa_ref, xb_ref, w_ref, o_ref):
    ka = xa_ref.shape[1]
    o_ref[...] = (jnp.dot(xa_ref[...], w_ref[:ka, :], preferred_element_type=F32)
                  + jnp.dot(xb_ref[...], w_ref[ka:, :], preferred_element_type=F32))


def matmul_concat(xa, xb, w, tm=512, tn=1024):
    m, ka = xa.shape
    kb = xb.shape[1]
    n = w.shape[1]
    tm = min(tm, m)
    assert m % tm == 0 and n % tn == 0
    return pl.pallas_call(
        _matmul2_kernel,
        grid=(n // tn, m // tm),
        in_specs=[pl.BlockSpec((tm, ka), lambda j, i: (i, 0)),
                  pl.BlockSpec((tm, kb), lambda j, i: (i, 0)),
                  pl.BlockSpec((ka + kb, tn), lambda j, i: (0, j))],
        out_specs=pl.BlockSpec((tm, tn), lambda j, i: (i, j)),
        out_shape=jax.ShapeDtypeStruct((m, n), F32),
        compiler_params=_params("arbitrary", "arbitrary"),
        name="out_matmul",
    )(xa, xb, w)


def _sb_kernel(q_ref, k_ref, v_ref, u_ref, o_ref, kb_ref, vb_ref, *, bq, bk, q_off, tk, scale):
    qi = pl.program_id(2)

    @pl.when(qi == 0)
    def _():
        _cast_rows(k_ref.at[0], kb_ref)
        _cast_rows(v_ref.at[0], vb_ref)

    q = q_ref[0].astype(BF16)
    r0 = qi * bq
    rows = q_off + r0 + lax.broadcasted_iota(I32, (bq, bk), 0)
    cols = lax.broadcasted_iota(I32, (bq, bk), 1)
    n_kb = jnp.minimum((q_off + r0 + bq + bk - 1) // bk, tk // bk)
    tri = u_ref[...]

    def live(carry):
        it, _, _, max_run = carry
        return (it < n_kb) & (max_run > F32_EXP_UNDERFLOW)

    def body(carry):
        it, acc, run, _ = carry
        start = pl.multiple_of((n_kb - 1 - it) * bk, bk)
        kblk = kb_ref[pl.ds(start, bk), :]
        vblk = vb_ref[pl.ds(start, bk), :]
        z = lax.dot_general(q, kblk, (((1,), (1,)), ((), ())), preferred_element_type=F32) * scale
        softplus = jnp.maximum(z, 0.0) + jnp.log(1.0 + jnp.exp(-jnp.abs(z)))
        earlier = (cols + start) < rows
        log_1mb = jnp.where(earlier, -softplus, 0.0)
        log_b = z - softplus
        hi = log_1mb.astype(BF16)
        lo = (log_1mb - hi.astype(F32)).astype(BF16)
        between = (jnp.dot(hi, tri, preferred_element_type=F32)
                   + jnp.dot(lo, tri, preferred_element_type=F32) + run)
        w = jnp.where(earlier, jnp.exp(log_b + between), 0.0)
        acc = acc + jnp.dot(w.astype(BF16), vblk, preferred_element_type=F32)
        run = run + jnp.sum(log_1mb, axis=1, keepdims=True)
        return it + 1, acc, run, jnp.max(run)

    _, acc, _, _ = lax.while_loop(
        live, body, (jnp.int32(0), jnp.zeros((bq, HEAD_DIM), F32), jnp.zeros((bq, 1), F32), jnp.float32(0.0)))
    o_ref[0] = acc.astype(o_ref.dtype)


def stick_breaking_attention(q_arr, k_arr, v_arr, q_cb, k_cb, v_cb, n_heads, q_off, bq, bk):
    b, tq, _ = q_arr.shape
    tk = k_arr.shape[1]
    assert tq % bq == 0 and tk % bk == 0
    tri = jnp.asarray(np.tril(np.ones((bk, bk), np.float32), -1), BF16)
    kern = functools.partial(_sb_kernel, bq=bq, bk=bk, q_off=q_off, tk=tk, scale=HEAD_DIM ** -0.5)
    return pl.pallas_call(
        kern,
        grid=(b, n_heads, tq // bq),
        in_specs=[pl.BlockSpec((1, bq, HEAD_DIM), lambda b_, h, i: (b_, i, q_cb + h)),
                  pl.BlockSpec((1, tk, HEAD_DIM), lambda b_, h, i: (b_, 0, k_cb + h)),
                  pl.BlockSpec((1, tk, HEAD_DIM), lambda b_, h, i: (b_, 0, v_cb + h)),
                  pl.BlockSpec((bk, bk), lambda b_, h, i: (0, 0))],
        out_specs=pl.BlockSpec((1, bq, HEAD_DIM), lambda b_, h, i: (b_, i, h)),
        out_shape=jax.ShapeDtypeStruct((b, tq, n_heads * HEAD_DIM), BF16),
        scratch_shapes=[pltpu.VMEM((tk, HEAD_DIM), BF16), pltpu.VMEM((tk, HEAD_DIM), BF16)],
        compiler_params=_params("arbitrary", "arbitrary", "arbitrary"),
        name="stick_breaking",
    )(q_arr, k_arr, v_arr, tri)


def _softmax_pv(s, v):
    m = jnp.max(s, axis=1, keepdims=True)
    e = jnp.exp(s - m)
    p = e / jnp.sum(e, axis=1, keepdims=True)
    return jnp.dot(p.astype(BF16), v, preferred_element_type=F32)


def _band_prompt_kernel(q_ref, k_ref, v_ref, bias_ref, o_ref, kb_ref, vb_ref, *, bq, reach, scale):
    qi = pl.program_id(2)

    @pl.when(qi == 0)
    def _():
        kb_ref[:reach, :] = jnp.zeros((reach, HEAD_DIM), BF16)
        vb_ref[:reach, :] = jnp.zeros((reach, HEAD_DIM), BF16)
        _cast_rows(k_ref.at[0], kb_ref, dst_off=reach)
        _cast_rows(v_ref.at[0], vb_ref, dst_off=reach)

    win = reach + bq
    start = pl.multiple_of(qi * bq, bq)
    q = q_ref[0].astype(BF16)
    kw = kb_ref[pl.ds(start, win), :]
    vw = vb_ref[pl.ds(start, win), :]
    s = lax.dot_general(q, kw, (((1,), (1,)), ((), ())), preferred_element_type=F32) * scale
    k_pos = start - reach + lax.broadcasted_iota(I32, (bq, win), 1)
    s = jnp.where(k_pos >= 0, s + bias_ref[0], NEG)
    o_ref[0] = _softmax_pv(s, vw).astype(o_ref.dtype)


def band_attention_prompt(proj, q_cb, k_cb, v_cb, rel_bias, bq=256):
    b, t, _ = proj.shape
    assert bq % CHUNK == 0 and t % bq == 0
    win = BAND_REACH + bq
    i = np.arange(bq)[:, None]
    j = np.arange(win)[None, :]
    rel = np.clip(BAND_REACH + i - j, -REL_CLIP, REL_CLIP) + REL_CLIP
    qc, kc = i // CHUNK, j // CHUNK - N_LEFT_CHUNKS
    in_band = (kc <= qc) & (kc >= qc - N_LEFT_CHUNKS)
    bias = jnp.where(jnp.asarray(in_band)[None], rel_bias.astype(F32)[:, rel], NEG)
    kern = functools.partial(_band_prompt_kernel, bq=bq, reach=BAND_REACH, scale=HEAD_DIM ** -0.5)
    return pl.pallas_call(
        kern,
        grid=(b, H_B, t // bq),
        in_specs=[pl.BlockSpec((1, bq, HEAD_DIM), lambda b_, h, i_: (b_, i_, q_cb + h)),
                  pl.BlockSpec((1, t, HEAD_DIM), lambda b_, h, i_: (b_, 0, k_cb + h)),
                  pl.BlockSpec((1, t, HEAD_DIM), lambda b_, h, i_: (b_, 0, v_cb + h)),
                  pl.BlockSpec((1, bq, win), lambda b_, h, i_: (h, 0, 0))],
        out_specs=pl.BlockSpec((1, bq, HEAD_DIM), lambda b_, h, i_: (b_, i_, h)),
        out_shape=jax.ShapeDtypeStruct((b, t, H_B * HEAD_DIM), BF16),
        scratch_shapes=[pltpu.VMEM((BAND_REACH + t, HEAD_DIM), BF16),
                        pltpu.VMEM((BAND_REACH + t, HEAD_DIM), BF16)],
        compiler_params=_params("arbitrary", "arbitrary", "arbitrary"),
        name="band_prompt",
    )(proj, proj, proj, bias)


def _band_core_kernel(q_ref, k_ref, v_ref, bias_ref, o_ref, *, scale):
    q = q_ref[0].astype(BF16)
    k = k_ref[0].astype(BF16)
    v = v_ref[0].astype(BF16)
    s = lax.dot_general(q, k, (((1,), (1,)), ((), ())), preferred_element_type=F32) * scale
    o_ref[0] = _softmax_pv(s + bias_ref[0], v).astype(o_ref.dtype)


def band_attention_core(q_arr, k_arr, v_arr, q_cb, bias):
    b, cq, _ = q_arr.shape
    ck = k_arr.shape[1]
    h_n = bias.shape[0]
    kern = functools.partial(_band_core_kernel, scale=HEAD_DIM ** -0.5)
    return pl.pallas_call(
        kern,
        grid=(b, h_n),
        in_specs=[pl.BlockSpec((1, cq, HEAD_DIM), lambda b_, h: (b_, 0, q_cb + h)),
                  pl.BlockSpec((1, ck, HEAD_DIM), lambda b_, h: (b_, 0, h)),
                  pl.BlockSpec((1, ck, HEAD_DIM), lambda b_, h: (b_, 0, h)),
                  pl.BlockSpec((1, cq, ck), lambda b_, h: (h, 0, 0))],
        out_specs=pl.BlockSpec((1, cq, HEAD_DIM), lambda b_, h: (b_, 0, h)),
        out_shape=jax.ShapeDtypeStruct((b, cq, h_n * HEAD_DIM), BF16),
        compiler_params=_params("arbitrary", "arbitrary"),
        name="band_core",
    )(q_arr, k_arr, v_arr, bias)


def _layer_norm_rows(h, g, b):
    mu = jnp.mean(h, axis=-1, keepdims=True)
    d = h - mu
    var = jnp.mean(d * d, axis=-1, keepdims=True)
    return d * lax.rsqrt(var + LN_EPS) * g + b


def _ln_router_kernel(x_ref, m_ref, g_ref, b_ref, wr_ref, rb_ref, x1_ref, gate_ref, idx_ref):
    y = _layer_norm_rows(DEEPNORM_ALPHA * x_ref[...] + m_ref[...], g_ref[...], b_ref[...])
    x1_ref[...] = y
    tb = y.shape[0]
    logits = jnp.dot(y.astype(BF16), wr_ref[...], preferred_element_type=F32)
    lane = lax.broadcasted_iota(I32, (tb, LANES), 1)
    lane_f = lane.astype(F32)
    real = lane < N_EXPERTS
    lg = jnp.where(real, logits, -jnp.inf)
    e = jnp.exp(lg - jnp.max(lg, axis=1, keepdims=True))
    aff = e / jnp.sum(e, axis=1, keepdims=True)
    sel = jnp.where(real, aff + rb_ref[...], -jnp.inf)
    group = lane // EXPERTS_PER_GROUP
    best = jnp.zeros((tb, 1), I32)
    best_v = jnp.max(jnp.where(group == 0, sel, -jnp.inf), axis=1, keepdims=True)
    for gi in range(1, N_GROUPS):
        gv = jnp.max(jnp.where(group == gi, sel, -jnp.inf), axis=1, keepdims=True)
        upd = gv > best_v
        best = jnp.where(upd, gi, best)
        best_v = jnp.where(upd, gv, best_v)
    cand = jnp.where((group == best) & real, sel, NEG)
    m1 = jnp.max(cand, axis=1, keepdims=True)
    i1 = jnp.min(jnp.where(cand == m1, lane_f, float(LANES)), axis=1, keepdims=True)
    cand2 = jnp.where(lane_f == i1, -jnp.inf, cand)
    m2 = jnp.max(cand2, axis=1, keepdims=True)
    i2 = jnp.min(jnp.where(cand2 == m2, lane_f, float(LANES)), axis=1, keepdims=True)
    g1 = jnp.sum(jnp.where(lane_f == i1, aff, 0.0), axis=1, keepdims=True)
    g2 = jnp.sum(jnp.where(lane_f == i2, aff, 0.0), axis=1, keepdims=True)
    tot = g1 + g2
    gate_ref[...] = jnp.where(lane == 0, g1 / tot, jnp.where(lane == 1, g2 / tot, 0.0))
    idx_ref[...] = jnp.where(lane == 0, i1, jnp.where(lane == 1, i2, 0.0)).astype(I32)


def ln_router(x, m, g, b, w_router_pad, router_bias_pad, tb=256):
    n, d = x.shape
    assert n % tb == 0
    row = pl.BlockSpec((tb, d), lambda i: (i, 0))
    vec = pl.BlockSpec((1, d), lambda i: (0, 0))
    meta = pl.BlockSpec((tb, LANES), lambda i: (i, 0))
    return pl.pallas_call(
        _ln_router_kernel,
        grid=(n // tb,),
        in_specs=[row, row, vec, vec,
                  pl.BlockSpec((d, LANES), lambda i: (0, 0)),
                  pl.BlockSpec((1, LANES), lambda i: (0, 0))],
        out_specs=[row, meta, meta],
        out_shape=[jax.ShapeDtypeStruct((n, d), F32),
                   jax.ShapeDtypeStruct((n, LANES), F32),
                   jax.ShapeDtypeStruct((n, LANES), I32)],
        compiler_params=_params("arbitrary"),
        name="ln_router",
    )(x, m, g.reshape(1, d), b.reshape(1, d), w_router_pad, router_bias_pad)


def _wait_rows(src_hbm, dst, sem, n):
    def body(r, c):
        pltpu.make_async_copy(src_hbm.at[pl.ds(0, 1), :], dst.at[pl.ds(0, 1), :], sem).wait()
        return c
    lax.fori_loop(0, n, body, 0)


def _gather_cast_kernel(tok_ref, na_ref, x_hbm, o_ref, buf, sem):
    tm = buf.shape[0]
    i = pl.program_id(0)
    base = i * tm

    @pl.when(i < na_ref[0])
    def _():
        def issue(r, c):
            pltpu.make_async_copy(x_hbm.at[pl.ds(tok_ref[base + r], 1), :], buf.at[pl.ds(r, 1), :], sem).start()
            return c
        lax.fori_loop(0, tm, issue, 0, unroll=8)
        _wait_rows(x_hbm, buf, sem, tm)
        _cast_rows(buf, o_ref)

    @pl.when(i >= na_ref[0])
    def _():
        o_ref[...] = jnp.zeros(o_ref.shape, o_ref.dtype)


def gather_rows_bf16(x, row_token, n_active, tm):
    n, d = x.shape
    r = row_token.shape[0]
    return pl.pallas_call(
        _gather_cast_kernel,
        grid_spec=pltpu.PrefetchScalarGridSpec(
            num_scalar_prefetch=2,
            grid=(r // tm,),
            in_specs=[pl.BlockSpec(memory_space=pl.ANY)],
            out_specs=pl.BlockSpec((tm, d), lambda i, tok, na: (i, 0)),
            scratch_shapes=[pltpu.VMEM((tm, d), F32), pltpu.SemaphoreType.DMA(())]),
        out_shape=jax.ShapeDtypeStruct((r, d), BF16),
        compiler_params=_params("arbitrary"),
        name="moe_gather",
    )(row_token, n_active, x)


def _expert_changed(te_ref, i):
    return (i == 0) | (te_ref[i] != te_ref[jnp.maximum(i - 1, 0)])


def _moe_up_kernel(te_ref, na_ref, xs_ref, wg_ref, wu_ref, h_ref, wgb, wub):
    i = pl.program_id(1)

    @pl.when(_expert_changed(te_ref, i))
    def _():
        _cast_rows(wg_ref, wgb)
        _cast_rows(wu_ref, wub)

    @pl.when(i < na_ref[0])
    def _():
        x = xs_ref[...]
        a = jnp.dot(x, wgb[...], preferred_element_type=F32)
        u = jnp.dot(x, wub[...], preferred_element_type=F32)
        h_ref[...] = (a * (1.0 / (1.0 + jnp.exp(-a))) * u).astype(h_ref.dtype)

    @pl.when(i >= na_ref[0])
    def _():
        h_ref[...] = jnp.zeros(h_ref.shape, h_ref.dtype)


def _moe_down_kernel(te_ref, na_ref, h_ref, wd_ref, gate_ref, y_ref, wdb):
    i = pl.program_id(1)

    @pl.when(_expert_changed(te_ref, i))
    def _():
        _cast_rows(wd_ref, wdb)

    @pl.when(i < na_ref[0])
    def _():
        y_ref[...] = gate_ref[...] * jnp.dot(h_ref[...], wdb[...], preferred_element_type=F32)

    @pl.when(i >= na_ref[0])
    def _():
        y_ref[...] = jnp.zeros(y_ref.shape, y_ref.dtype)


def moe_grouped(xs, gate_sorted, tile_expert, n_active, w_gate, w_up, w_down, layer, tm, tf=256, tn=1024):
    r, d = xs.shape
    n_tiles = r // tm
    de = w_gate.shape[3]
    tf, tn = min(tf, de), min(tn, d)
    h = pl.pallas_call(
        _moe_up_kernel,
        grid_spec=pltpu.PrefetchScalarGridSpec(
            num_scalar_prefetch=2,
            grid=(de // tf, n_tiles),
            in_specs=[pl.BlockSpec((tm, d), lambda f, i, te, na: (i, 0)),
                      pl.BlockSpec((None, None, d, tf), lambda f, i, te, na: (layer, te[i], 0, f)),
                      pl.BlockSpec((None, None, d, tf), lambda f, i, te, na: (layer, te[i], 0, f))],
            out_specs=pl.BlockSpec((tm, tf), lambda f, i, te, na: (i, f)),
            scratch_shapes=[pltpu.VMEM((d, tf), BF16), pltpu.VMEM((d, tf), BF16)]),
        out_shape=jax.ShapeDtypeStruct((r, de), BF16),
        compiler_params=_params("arbitrary", "arbitrary"),
        name="moe_up",
    )(tile_expert, n_active, xs, w_gate, w_up)
    return pl.pallas_call(
        _moe_down_kernel,
        grid_spec=pltpu.PrefetchScalarGridSpec(
            num_scalar_prefetch=2,
            grid=(d // tn, n_tiles),
            in_specs=[pl.BlockSpec((tm, de), lambda n, i, te, na: (i, 0)),
                      pl.BlockSpec((None, None, de, tn), lambda n, i, te, na: (layer, te[i], 0, n)),
                      pl.BlockSpec((tm, 1), lambda n, i, te, na: (i, 0))],
            out_specs=pl.BlockSpec((tm, tn), lambda n, i, te, na: (i, n)),
            scratch_shapes=[pltpu.VMEM((de, tn), BF16)]),
        out_shape=jax.ShapeDtypeStruct((r, d), F32),
        compiler_params=_params("arbitrary", "arbitrary"),
        name="moe_down",
    )(tile_expert, n_active, h, w_down, gate_sorted)


def _combine_ln_kernel(pos_ref, y_hbm, x_ref, g_ref, b_ref, xo_ref, buf, sem):
    tb = x_ref.shape[0]
    base = pl.program_id(0) * (2 * tb)

    def issue(r, c):
        pltpu.make_async_copy(y_hbm.at[pl.ds(pos_ref[base + r], 1), :], buf.at[pl.ds(r, 1), :], sem).start()
        return c
    lax.fori_loop(0, 2 * tb, issue, 0)
    _wait_rows(y_hbm, buf, sem, 2 * tb)
    f = buf[:tb, :] + buf[tb:, :]
    xo_ref[...] = _layer_norm_rows(DEEPNORM_ALPHA * x_ref[...] + f, g_ref[...], b_ref[...])


def combine_ln(y_sorted, pos_blocks, x1, g, b, tb):
    n, d = x1.shape
    row = pl.BlockSpec((tb, d), lambda i, pos: (i, 0))
    vec = pl.BlockSpec((1, d), lambda i, pos: (0, 0))
    return pl.pallas_call(
        _combine_ln_kernel,
        grid_spec=pltpu.PrefetchScalarGridSpec(
            num_scalar_prefetch=1,
            grid=(n // tb,),
            in_specs=[pl.BlockSpec(memory_space=pl.ANY), row, vec, vec],
            out_specs=row,
            scratch_shapes=[pltpu.VMEM((2 * tb, d), F32), pltpu.SemaphoreType.DMA(())]),
        out_shape=jax.ShapeDtypeStruct((n, d), F32),
        compiler_params=_params("arbitrary"),
        name="moe_combine_ln",
    )(pos_blocks, y_sorted, x1, g.reshape(1, d), b.reshape(1, d))


def moe_block(x, m, ln_g, ln_b, w_router_pad, router_bias_pad, w_gate, w_up, w_down, layer,
              tm=512, tb=128):
    n, d = x.shape
    x1, gate, idx = ln_router(x, m, ln_g[0], ln_b[0], w_router_pad, router_bias_pad)
    e_flat = idx[:, :TOP_K].reshape(-1)
    g_flat = gate[:, :TOP_K].reshape(-1)
    onehot = (e_flat[:, None] == jnp.arange(N_EXPERTS, dtype=I32)[None, :]).astype(I32)
    csum = jnp.cumsum(onehot, axis=0)
    rank = jnp.take_along_axis(csum, e_flat[:, None], axis=1)[:, 0] - 1
    counts = csum[-1]
    padded = (counts + tm - 1) // tm * tm
    ends = jnp.cumsum(padded)
    pos = (ends - padded)[e_flat] + rank
    n_pairs = TOP_K * n
    n_tiles = (n_pairs + N_EXPERTS * (tm - 1)) // tm + 1
    r = n_tiles * tm
    row_token = jnp.zeros((r,), I32).at[pos].set(jnp.arange(n_pairs, dtype=I32) // TOP_K)
    gate_sorted = jnp.zeros((r,), F32).at[pos].set(g_flat).reshape(r, 1)
    n_active = (ends[-1] // tm).astype(I32)
    tile = jnp.arange(n_tiles, dtype=I32)
    tile_expert = jnp.searchsorted(ends, jnp.minimum(tile, n_active - 1) * tm, side='right').astype(I32)
    tile_expert = jnp.minimum(tile_expert, N_EXPERTS - 1)
    n_active = n_active.reshape(1)
    xs = gather_rows_bf16(x1, row_token, n_active, tm)
    y_sorted = moe_grouped(xs, gate_sorted, tile_expert, n_active, w_gate, w_up, w_down, layer, tm)
    pos_blocks = pos.reshape(n // tb, tb, TOP_K).transpose(0, 2, 1).reshape(-1)
    return combine_ln(y_sorted, pos_blocks, x1, ln_g[1], ln_b[1], tb)


def even_layer(xp, xs, cache_a_k, cache_a_v, cache_b_k, cache_b_v, w_in, w_out, rel_bias):
    bp, t, d = xp.shape
    bs, n, _ = xs.shape
    past, lb = cache_a_k.shape[1], cache_b_k.shape[1]
    hd = H_A * HEAD_DIM
    proj_p = matmul(xp.reshape(bp * t, d).astype(BF16), w_in).reshape(bp, t, -1)
    proj_s = matmul(xs.reshape(bs * n, d).astype(BF16), w_in).reshape(bs, n, -1)
    nb = hd // HEAD_DIM
    qa_cb, ka_cb, va_cb, qb_cb, kb_cb, vb_cb = (s * nb for s in range(6))
    seg = lambda p, s: p[..., s * hd:(s + 1) * hd]
    oa_p = stick_breaking_attention(proj_p, proj_p, proj_p, qa_cb, ka_cb, va_cb, H_A, 0, 256, 256)
    ob_p = band_attention_prompt(proj_p, qb_cb, kb_cb, vb_cb, rel_bias)
    mp = matmul_concat(oa_p.reshape(bp * t, hd), ob_p.reshape(bp * t, hd), w_out)
    bk_s = 256
    tk_a = -(-(past + n) // bk_s) * bk_s
    cat = lambda c, new, tot: jnp.concatenate(
        [c.reshape(bs, c.shape[1], hd), new, jnp.zeros((bs, tot - c.shape[1] - n, hd), F32)], 1)
    ka_s, va_s, kb_s, vb_s = seg(proj_s, 1), seg(proj_s, 2), seg(proj_s, 4), seg(proj_s, 5)
    oa_s = stick_breaking_attention(proj_s, cat(cache_a_k, ka_s, tk_a), cat(cache_a_v, va_s, tk_a),
                                    qa_cb, 0, 0, H_A, past, n, bk_s)
    ck = -(-(lb + n) // LANES) * LANES
    q_pos = past + np.arange(n)
    k_pos = np.concatenate([np.arange(past - lb, past), q_pos, np.full((ck - lb - n,), -1)])
    qc, kc = q_pos[:, None] // CHUNK, k_pos[None, :] // CHUNK
    valid = (kc <= qc) & (kc >= qc - N_LEFT_CHUNKS) & (k_pos[None, :] >= 0)
    rel = np.clip(q_pos[:, None] - k_pos[None, :], -REL_CLIP, REL_CLIP) + REL_CLIP
    bias_s = jnp.where(jnp.asarray(valid)[None], rel_bias.astype(F32)[:, rel], NEG)
    ob_s = band_attention_core(proj_s, cat(cache_b_k, kb_s, ck), cat(cache_b_v, vb_s, ck), qb_cb, bias_s)
    ms = matmul_concat(oa_s.reshape(bs * n, hd), ob_s.reshape(bs * n, hd), w_out)
    keep = min(BAND_REACH, t)
    heads = lambda a: a.reshape(a.shape[0], a.shape[1], H_A, HEAD_DIM)
    caches = (heads(seg(proj_p, 1)), heads(seg(proj_p, 2)), heads(ka_s), heads(va_s),
              heads(seg(proj_p, 4)[:, t - keep:]), heads(seg(proj_p, 5)[:, t - keep:]),
              heads(kb_s), heads(vb_s))
    return mp, ms, caches


INT_MIN = np.int32(-2 ** 31)


def _rope_kernel(qc_ref, kc_ref, vc_ref, qi_ref, ki_ref, wi_ref, c128_ref, s128_ref, c64_ref, s64_ref,
                 qc_o, kc_o, kcb_o, vcb_o, qi_o, ki_o, kib_o, wi_o):
    c128, s128, c64, s64 = c128_ref[...], s128_ref[...], c64_ref[...], s64_ref[...]
    lane = lax.broadcasted_iota(I32, c64.shape, 1)
    first_half = (lane % D_IDX) < (D_IDX // 2)

    def rot128(x):
        return x * c128 + pltpu.roll(x, HEAD_DIM // 2, 1) * s128

    def rot64(x):
        partner = jnp.where(first_half, pltpu.roll(x, LANES - D_IDX // 2, 1), pltpu.roll(x, D_IDX // 2, 1))
        return x * c64 + partner * s64

    for h in range(H_C):
        sl = slice(h * HEAD_DIM, (h + 1) * HEAD_DIM)
        qc_o[:, sl] = rot128(qc_ref[:, sl]).astype(BF16)
    for h in range(KV_C):
        sl = slice(h * HEAD_DIM, (h + 1) * HEAD_DIM)
        r = rot128(kc_ref[:, sl])
        kc_o[:, sl] = r
        kcb_o[:, sl] = r.astype(BF16)
    vcb_o[...] = vc_ref[...].astype(BF16)
    for j in range(H_IDX * D_IDX // LANES):
        r = rot64(qi_ref[:, j * LANES:(j + 1) * LANES]).astype(BF16)
        qi_o[2 * j] = r[:, :D_IDX]
        qi_o[2 * j + 1] = r[:, D_IDX:]
    r = rot64(ki_ref[...])
    ki_o[...] = r
    kib_o[...] = r.astype(BF16)
    wi_o[...] = wi_ref[...]


def rope_tables(pos):
    def tab(half, reps):
        inv = ROPE_THETA ** (-jnp.arange(half, dtype=F32) / half)
        ang = pos.astype(F32)[:, None] * inv[None, :]
        c, s = jnp.cos(ang), jnp.sin(ang)
        return jnp.tile(jnp.concatenate([c, c], 1), (1, reps)), jnp.tile(jnp.concatenate([-s, s], 1), (1, reps))
    c128, s128 = tab(HEAD_DIM // 2, 1)
    c64, s64 = tab(D_IDX // 2, 2)
    return c128, s128, c64, s64


def rope_split(proj, pos, cols, tb=256):
    n = proj.shape[0]
    assert n % tb == 0
    tabs = rope_tables(pos)
    dq, dk, di = H_C * HEAD_DIM, KV_C * HEAD_DIM, H_IDX * D_IDX
    spec = lambda w, c0: pl.BlockSpec((tb, w), lambda i: (i, c0 // w))
    row = lambda w: pl.BlockSpec((tb, w), lambda i: (i, 0))
    return pl.pallas_call(
        _rope_kernel,
        grid=(n // tb,),
        in_specs=[spec(dq, cols['qc']), spec(dk, cols['kc']), spec(dk, cols['vc']), spec(di, cols['qi']),
                  spec(LANES, cols['ki']), spec(LANES, cols['wi'])] + [row(LANES)] * 4,
        out_specs=[row(dq), row(dk), row(dk), row(dk),
                   pl.BlockSpec((H_IDX, tb, D_IDX), lambda i: (0, i, 0)), row(LANES), row(LANES), row(LANES)],
        out_shape=[jax.ShapeDtypeStruct((n, dq), BF16), jax.ShapeDtypeStruct((n, dk), F32),
                   jax.ShapeDtypeStruct((n, dk), BF16), jax.ShapeDtypeStruct((n, dk), BF16),
                   jax.ShapeDtypeStruct((H_IDX, n, D_IDX), BF16), jax.ShapeDtypeStruct((n, LANES), F32),
                   jax.ShapeDtypeStruct((n, LANES), BF16), jax.ShapeDtypeStruct((n, LANES), F32)],
        compiler_params=_params("arbitrary"),
        name="rope_split",
    )(proj, proj, proj, proj, proj, proj, *tabs)


def _dsa_kernel(qc_ref, qi_ref, wi_ref, kc_ref, vc_ref, ki_ref, tri_ref, o_ref, key_ref, mask_ref, *,
                bq, bk, q_off, s_real, s_pad, topk, scale):
    qi_blk = pl.program_id(1)
    r0 = qi_blk * bq
    adm_end = ((q_off + r0 + bq - 1) // CHUNK + 1) * CHUNK
    n_kb = jnp.minimum((adm_end + bk - 1) // bk, s_pad // bk)
    q_pos = q_off + r0 + lax.broadcasted_iota(I32, (bq, bk), 0)
    col = lax.broadcasted_iota(I32, (bq, bk), 1)
    kf = float(topk)

    def admissible(start):
        k_pos = col + start
        return ((k_pos // CHUNK) <= (q_pos // CHUNK)) & (k_pos < s_real)

    def lane_fold(x):
        acc = x[:, :LANES]
        for j in range(1, bk // LANES):
            acc = acc + x[:, j * LANES:(j + 1) * LANES]
        return acc

    def count(pred_fn):
        def body(kb, acc):
            start = pl.multiple_of(kb * bk, bk)
            blk = key_ref[:, pl.ds(start, bk)]
            return acc + lane_fold(jnp.where(pred_fn(blk), 1.0, 0.0))
        part = lax.fori_loop(0, n_kb, body, jnp.zeros((bq, LANES), F32))
        return jnp.sum(part, axis=1, keepdims=True)

    wi = wi_ref[0]
    wcols = [wi[:, h:h + 1] for h in range(H_IDX)]

    def score_body(kb, c):
        start = pl.multiple_of(kb * bk, bk)
        ki_blk = ki_ref[0, pl.ds(start, bk), :]
        acc = jnp.zeros((bq, bk), F32)
        for h in range(H_IDX):
            lg = lax.dot_general(qi_ref[h], ki_blk, (((1,), (1,)), ((), ())), preferred_element_type=F32)
            acc = acc + wcols[h] * jnp.maximum(lg, 0.0)
        bits = pltpu.bitcast(acc + 0.0, I32)
        key = jnp.where(bits < 0, bits ^ 0x7FFFFFFF, bits)
        key_ref[:, pl.ds(start, bk)] = jnp.where(admissible(start), key, INT_MIN)
        return c
    lax.fori_loop(0, n_kb, score_body, 0)

    prefix = jnp.where(count(lambda b: b >= 0) >= kf, 0, INT_MIN).astype(I32)

    def bit_body(it, prefix):
        cand = prefix | jnp.left_shift(jnp.int32(1), 30 - it)
        return jnp.where(count(lambda b: b >= cand) >= kf, cand, prefix)
    thr = lax.fori_loop(0, 31, bit_body, prefix)

    need = kf - count(lambda b: b > thr)
    tri = tri_ref[...]

    def mask_body(kb, run):
        start = pl.multiple_of(kb * bk, bk)
        blk = key_ref[:, pl.ds(start, bk)]
        tie = blk == thr
        tie_f = jnp.where(tie, 1.0, 0.0)
        before = jnp.dot(tie_f.astype(BF16), tri, preferred_element_type=F32) + run
        sel = ((blk > thr) | (tie & (before < need))) & admissible(start)
        mask_ref[:, pl.ds(start, bk)] = jnp.where(sel, 0.0, NEG)
        return run + jnp.sum(tie_f, axis=1, keepdims=True)
    lax.fori_loop(0, n_kb, mask_body, jnp.zeros((bq, 1), F32))

    rep = H_C // KV_C
    for g in range(KV_C):
        qg = jnp.concatenate([qc_ref[0, :, (g * rep + r) * HEAD_DIM:(g * rep + r + 1) * HEAD_DIM]
                              for r in range(rep)], axis=0)

        def att_body(kb, carry):
            m, l, acc = carry
            start = pl.multiple_of(kb * bk, bk)
            kblk = kc_ref[0, pl.ds(start, bk), g * HEAD_DIM:(g + 1) * HEAD_DIM]
            vblk = vc_ref[0, pl.ds(start, bk), g * HEAD_DIM:(g + 1) * HEAD_DIM]
            mk = mask_ref[:, pl.ds(start, bk)]
            s = lax.dot_general(qg, kblk, (((1,), (1,)), ((), ())), preferred_element_type=F32) * scale
            s = s + jnp.concatenate([mk] * rep, axis=0)
            m_new = jnp.maximum(m, jnp.max(s, axis=1, keepdims=True))
            alpha = jnp.exp(m - m_new)
            p = jnp.exp(s - m_new)
            l = alpha * l + jnp.sum(p, axis=1, keepdims=True)
            acc = alpha * acc + jnp.dot(p.astype(BF16), vblk, preferred_element_type=F32)
            return m_new, l, acc
        m0 = jnp.full((rep * bq, 1), NEG, F32)
        _, l, acc = lax.fori_loop(0, n_kb, att_body,
                                  (m0, jnp.zeros((rep * bq, 1), F32), jnp.zeros((rep * bq, HEAD_DIM), F32)))
        o = acc / l
        for r in range(rep):
            o_ref[0, :, (g * rep + r) * HEAD_DIM:(g * rep + r + 1) * HEAD_DIM] = (
                o[r * bq:(r + 1) * bq].astype(o_ref.dtype))


def dsa_attention(qc, qi_hm, wi, kc, vc, ki, q_off, s_real, topk, bq, bk):
    b, tq, _ = qc.shape
    s_pad = kc.shape[1]
    assert tq % bq == 0 and s_pad % bk == 0 and bk % LANES == 0
    nq = tq // bq
    tri = jnp.asarray(np.triu(np.ones((bk, bk), np.float32), 1), BF16)
    kern = functools.partial(_dsa_kernel, bq=bq, bk=bk, q_off=q_off, s_real=s_real, s_pad=s_pad,
                             topk=topk, scale=HEAD_DIM ** -0.5)
    once = dict(pipeline_mode=pl.Buffered(1))
    return pl.pallas_call(
        kern,
        grid=(b, nq),
        in_specs=[pl.BlockSpec((1, bq, H_C * HEAD_DIM), lambda b_, i: (b_, i, 0)),
                  pl.BlockSpec((H_IDX, bq, D_IDX), lambda b_, i: (0, b_ * nq + i, 0)),
                  pl.BlockSpec((1, bq, LANES), lambda b_, i: (b_, i, 0)),
                  pl.BlockSpec((1, s_pad, KV_C * HEAD_DIM), lambda b_, i: (b_, 0, 0), **once),
                  pl.BlockSpec((1, s_pad, KV_C * HEAD_DIM), lambda b_, i: (b_, 0, 0), **once),
                  pl.BlockSpec((1, s_pad, D_IDX), lambda b_, i: (b_, 0, 0), **once),
                  pl.BlockSpec((bk, bk), lambda b_, i: (0, 0))],
        out_specs=pl.BlockSpec((1, bq, H_C * HEAD_DIM), lambda b_, i: (b_, i, 0)),
        out_shape=jax.ShapeDtypeStruct((b, tq, H_C * HEAD_DIM), BF16),
        scratch_shapes=[pltpu.VMEM((bq, s_pad), I32), pltpu.VMEM((bq, s_pad), F32)],
        compiler_params=_params("arbitrary", "arbitrary"),
        name="dsa_attention",
    )(qc, qi_hm, wi, kc, vc, ki, tri)


HALO = 8


def _gdn_prep_kernel(x_ref, prev_ref, halo_ref, cw_ref, a_ref, b_ref, alog_ref, dtb_ref, o_ref, g_ref,
                     beta_ref, *, heads_per_blk, blocks_per_seq):
    j = pl.program_id(1)
    x = x_ref[...]
    tb = x.shape[0]
    halo = jnp.where(pl.program_id(0) % blocks_per_seq == 0, halo_ref[0], prev_ref[...])
    row = lax.broadcasted_iota(I32, halo.shape, 0)
    acc = x * cw_ref[CONV_W - 1:CONV_W, :]
    for k in range(1, CONV_W):
        rolled = pltpu.roll(x, k, 0)
        top = jnp.where(row < k, pltpu.roll(halo, k, 0), rolled[:HALO])
        shifted = jnp.concatenate([top, rolled[HALO:]], axis=0) if tb > HALO else top
        acc = acc + shifted * cw_ref[CONV_W - 1 - k:CONV_W - k, :]
    y = acc * (1.0 / (1.0 + jnp.exp(-acc)))
    seg = j // (H_D // heads_per_blk)
    scale = jnp.where(seg == 0, DK_D ** -0.5, 1.0)
    for h in range(heads_per_blk):
        sl = slice(h * LANES, (h + 1) * LANES)
        t = y[:, sl]
        nrm = t * lax.rsqrt(jnp.sum(t * t, axis=1, keepdims=True) + RMS_EPS) * scale
        o_ref[:, sl] = jnp.where(seg == 2, t, nrm)

    @pl.when(j == 0)
    def _():
        z = a_ref[...] + dtb_ref[...]
        softplus = jnp.maximum(z, 0.0) + jnp.log(1.0 + jnp.exp(-jnp.abs(z)))
        g_ref[...] = -jnp.exp(alog_ref[...]) * softplus
        beta_ref[...] = 1.0 / (1.0 + jnp.exp(-b_ref[...]))


def gdn_prep(proj, n_rows, halo, conv_w, a_log_pad, dt_bias_pad, qkv_col, a_col, b_col, tb):
    cb = 512
    hpb = cb // LANES
    bps = n_rows // tb // halo.shape[0]
    kern = functools.partial(_gdn_prep_kernel, heads_per_blk=hpb, blocks_per_seq=bps)
    vec = pl.BlockSpec((1, LANES), lambda i, j: (0, 0))
    return pl.pallas_call(
        kern,
        grid=(n_rows // tb, QKV_D // cb),
        in_specs=[pl.BlockSpec((tb, cb), lambda i, j: (i, qkv_col // cb + j)),
                  pl.BlockSpec((HALO, cb), lambda i, j: (jnp.maximum(i * (tb // HALO) - 1, 0),
                                                        qkv_col // cb + j)),
                  pl.BlockSpec((1, HALO, cb), lambda i, j: (i // bps, 0, j)),
                  pl.BlockSpec((CONV_W, cb), lambda i, j: (0, j)),
                  pl.BlockSpec((tb, LANES), lambda i, j: (i, a_col // LANES)),
                  pl.BlockSpec((tb, LANES), lambda i, j: (i, b_col // LANES)), vec, vec],
        out_specs=[pl.BlockSpec((tb, cb), lambda i, j: (i, j)),
                   pl.BlockSpec((tb, LANES), lambda i, j: (i, 0)),
                   pl.BlockSpec((tb, LANES), lambda i, j: (i, 0))],
        out_shape=[jax.ShapeDtypeStruct((n_rows, QKV_D), F32), jax.ShapeDtypeStruct((n_rows, LANES), F32),
                   jax.ShapeDtypeStruct((n_rows, LANES), F32)],
        compiler_params=_params("arbitrary", "arbitrary"),
        name="gdn_prep",
    )(proj, proj, halo, conv_w, proj, proj, a_log_pad, dt_bias_pad)


def _split3(x):
    a = x.astype(BF16)
    r = x - a.astype(F32)
    b = r.astype(BF16)
    c = (r - b.astype(F32)).astype(BF16)
    return a, b, c


def _dot_f32(a, b):
    a_hi, a_lo, _ = _split3(a)
    b_hi, b_lo, _ = _split3(b)
    return (jnp.dot(a_hi, b_hi, preferred_element_type=F32) + jnp.dot(a_hi, b_lo, preferred_element_type=F32)
            + jnp.dot(a_lo, b_hi, preferred_element_type=F32))


def _gdn_local_kernel(q_ref, k_ref, v_ref, g_ref, beta_ref, solv_ref, solk_ref, qe_ref, kd_ref, qkd_ref,
                      egl_ref, *, c, hb):
    assert hb == H_D
    ri = lax.broadcasted_iota(I32, (c, c), 0)
    ci = lax.broadcasted_iota(I32, (c, c), 1)
    incl, strict, eye = ri >= ci, ri > ci, ri == ci
    tril = jnp.where(incl, 1.0, 0.0).astype(BF16)
    g_parts = _split3(g_ref[...])
    gcum = sum(jnp.dot(tril, p, preferred_element_type=F32) for p in g_parts)
    gcum_t = sum(lax.dot_general(p, tril, (((0,), (1,)), ((), ())), preferred_element_type=F32)
                 for p in g_parts)
    beta_all = beta_ref[...]
    egl_ref[0] = jnp.exp(gcum[c - 1:c, :])
    ident = jnp.where(eye, 1.0, 0.0)
    heads = range(hb)
    sls = [slice(hh * LANES, (hh + 1) * LANES) for hh in heads]
    nmats, rhs = [], []
    for hh in heads:
        gc = gcum[:, hh:hh + 1]
        bc = beta_all[:, hh:hh + 1]
        decay = jnp.where(incl, jnp.exp(jnp.where(incl, gc - gcum_t[hh:hh + 1, :], 0.0)), 0.0)
        q, k, v = q_ref[:, sls[hh]], k_ref[:, sls[hh]], v_ref[:, sls[hh]]
        qb, kb = q.astype(BF16), k.astype(BF16)
        kk = lax.dot_general(kb, kb, (((1,), (1,)), ((), ())), preferred_element_type=F32)
        qk = lax.dot_general(qb, kb, (((1,), (1,)), ((), ())), preferred_element_type=F32)
        nmats.append(jnp.where(strict, bc * kk * decay, 0.0))
        rhs.append(jnp.concatenate([v * bc, k * (bc * jnp.exp(gc))], axis=1))
        qkd_ref[0, hh] = (qk * decay).astype(BF16)
        qe_ref[:, sls[hh]] = (q * jnp.exp(gc)).astype(BF16)
        kd_ref[:, sls[hh]] = (k * jnp.exp(gc[c - 1:c, :] - gc)).astype(BF16)
    invs = [ident - n for n in nmats]
    pows = nmats
    for _ in range(int(np.log2(c)) - 1):
        pows = [_dot_f32(p, p) for p in pows]
        invs = [t + _dot_f32(t, p) for t, p in zip(invs, pows)]
    for hh in heads:
        y = _dot_f32(invs[hh], rhs[hh])
        solv_ref[:, sls[hh]] = y[:, :DV_D]
        solk_ref[:, sls[hh]] = y[:, DV_D:].astype(BF16)


def gdn_local(qkv, g, beta, c, hb=H_D):
    n = qkv.shape[0]
    nc = n // c
    dh = H_D * DK_D
    kern = functools.partial(_gdn_local_kernel, c=c, hb=hb)
    w = hb * LANES
    seg = lambda s: pl.BlockSpec((c, w), lambda i, j: (i, s * (H_D // hb) + j))
    col = pl.BlockSpec((c, w), lambda i, j: (i, j))
    meta = pl.BlockSpec((c, LANES), lambda i, j: (i, 0))
    return pl.pallas_call(
        kern,
        grid=(nc, H_D // hb),
        in_specs=[seg(0), seg(1), seg(2), meta, meta],
        out_specs=[col, col, col, col,
                   pl.BlockSpec((1, hb, c, c), lambda i, j: (i, j, 0, 0)),
                   pl.BlockSpec((1, 1, LANES), lambda i, j: (i, 0, 0))],
        out_shape=[jax.ShapeDtypeStruct((n, dh), F32), jax.ShapeDtypeStruct((n, dh), BF16),
                   jax.ShapeDtypeStruct((n, dh), BF16), jax.ShapeDtypeStruct((n, dh), BF16),
                   jax.ShapeDtypeStruct((nc, H_D, c, c), BF16),
                   jax.ShapeDtypeStruct((nc, 1, LANES), F32)],
        compiler_params=_params("arbitrary", "arbitrary"),
        name="gdn_local",
    )(qkv, qkv, qkv, g, beta)


def _gdn_scan_kernel(s0_ref, solv_ref, solk_ref, qe_ref, kd_ref, qkd_ref, egl_ref, gate_ref, nw_ref,
                     od_ref, sout_ref, s_ref):
    ci = pl.program_id(1)

    @pl.when(ci == 0)
    def _():
        s_ref[...] = s0_ref[0]

    egl = egl_ref[0]
    nw = nw_ref[...]
    for h in range(H_D):
        sl = slice(h * LANES, (h + 1) * LANES)
        s_old = s_ref[h]
        sb = s_old.astype(BF16)
        u = solv_ref[:, sl] - jnp.dot(solk_ref[:, sl], sb, preferred_element_type=F32)
        ub = u.astype(BF16)
        o = (jnp.dot(qe_ref[:, sl], sb, preferred_element_type=F32)
             + jnp.dot(qkd_ref[0, h], ub, preferred_element_type=F32))
        s_ref[h] = s_old * egl[:, h:h + 1] + lax.dot_general(
            kd_ref[:, sl], ub, (((0,), (0,)), ((), ())), preferred_element_type=F32)
        o = o * lax.rsqrt(jnp.mean(o * o, axis=1, keepdims=True) + RMS_EPS) * nw
        gt = gate_ref[:, sl]
        od_ref[:, sl] = (o * (gt * (1.0 / (1.0 + jnp.exp(-gt))))).astype(od_ref.dtype)

    @pl.when(ci == pl.num_programs(1) - 1)
    def _():
        sout_ref[0] = s_ref[...]


def gdn_scan(s0, solv, solk, qe, kd, qkd, egl, gate_arr, gate_col, norm_w, c):
    b = s0.shape[0]
    n = solv.shape[0]
    cps = n // b // c
    dh = H_D * DV_D
    col = pl.BlockSpec((c, dh), lambda b_, i: (b_ * cps + i, 0))
    st = pl.BlockSpec((1, H_D, DK_D, DV_D), lambda b_, i: (b_, 0, 0, 0))
    return pl.pallas_call(
        _gdn_scan_kernel,
        grid=(b, cps),
        in_specs=[st, col, col, col, col,
                  pl.BlockSpec((1, H_D, c, c), lambda b_, i: (b_ * cps + i, 0, 0, 0)),
                  pl.BlockSpec((1, 1, LANES), lambda b_, i: (b_ * cps + i, 0, 0)),
                  pl.BlockSpec((c, dh), lambda b_, i: (b_ * cps + i, gate_col // dh)),
                  pl.BlockSpec((1, LANES), lambda b_, i: (0, 0))],
        out_specs=[col, st],
        out_shape=[jax.ShapeDtypeStruct((n, dh), BF16), jax.ShapeDtypeStruct(s0.shape, F32)],
        scratch_shapes=[pltpu.VMEM((H_D, DK_D, DV_D), F32)],
        compiler_params=_params("arbitrary", "arbitrary"),
        name="gdn_scan",
    )(s0, solv, solk, qe, kd, qkd, egl, gate_arr, norm_w.reshape(1, LANES))


def gdn_mixer(proj, n_rows, cols, halo, s0, conv_w, a_log, dt_bias, norm_w, c, tb):
    pad = lambda v: jnp.pad(v.astype(F32), (0, LANES - H_D)).reshape(1, LANES)
    qkv, g, beta = gdn_prep(proj, n_rows, halo, conv_w, pad(a_log), pad(dt_bias),
                            cols['qkv'], cols['a'], cols['b'], tb)
    solv, solk, qe, kd, qkd, egl = gdn_local(qkv, g, beta, c)
    return gdn_scan(s0, solv, solk, qe, kd, qkd, egl, proj, cols['g'], norm_w, c)


ODD_COLS = {}
_orig, _new = 0, 0
_order = dict(qc=0, kc=1, vc=2, qi=3, ki=4, wi=5, qkv=6, a=7, b=8, g=9)
_starts = np.concatenate([[0], np.cumsum(ODD_SIZES)])
for _name in ('qc', 'g', 'kc', 'vc', 'qi', 'qkv', 'ki', 'wi', 'a', 'b'):
    _w = ODD_SIZES[_order[_name]]
    ODD_COLS[_name] = (_new, int(_starts[_order[_name]]), _w)
    _new += -(-_w // LANES) * LANES
ODD_WIDTH = _new


def reorder_w_in_odd(w):
    parts = []
    for name in ('qc', 'g', 'kc', 'vc', 'qi', 'qkv', 'ki', 'wi', 'a', 'b'):
        _, o0, wd = ODD_COLS[name]
        parts.append(jnp.pad(w[:, o0:o0 + wd], ((0, 0), (0, (-wd) % LANES))))
    return jnp.concatenate(parts, axis=1).astype(BF16)


def odd_layer(x, n_p, bs, n, cache_c_k, cache_c_v, cache_c_idx_k, state_d_ssm, state_d_conv,
              w_in_r, w_out, conv_w, a_log, dt_bias, norm_w):
    cols = {k: v[0] for k, v in ODD_COLS.items()}
    past = cache_c_k.shape[1]
    assert n >= CONV_W - 1 and n_p % 256 == 0
    proj = matmul(x.astype(BF16), w_in_r, tn=ODD_WIDTH // 10)
    pos = jnp.concatenate([jnp.arange(n_p), jnp.tile(past + jnp.arange(n), bs)])
    qc_b, kc_f, kc_b, vc_b, qi_hm, ki_f, ki_b, wi_f = rope_split(proj, pos, cols)
    dk = KV_C * HEAD_DIM
    topk_p = min(TOPK_MAX, n_p // 4)
    oc_p = dsa_attention(qc_b[:n_p][None], qi_hm[:, :n_p], wi_f[:n_p][None], kc_b[:n_p][None],
                         vc_b[:n_p][None], ki_b[:n_p, :D_IDX][None], 0, n_p, topk_p, 128, 512)
    bk_s = 384
    s_real = past + n
    s_pad = -(-s_real // bk_s) * bk_s
    cat = lambda c, new: jnp.concatenate(
        [c.reshape(bs, past, -1).astype(BF16), new, jnp.zeros((bs, s_pad - s_real, new.shape[-1]), BF16)], 1)
    oc_s = dsa_attention(qc_b[n_p:].reshape(bs, n, -1), qi_hm[:, n_p:], wi_f[n_p:].reshape(bs, n, LANES),
                         cat(cache_c_k, kc_b[n_p:].reshape(bs, n, dk)),
                         cat(cache_c_v, vc_b[n_p:].reshape(bs, n, dk)),
                         cat(cache_c_idx_k, ki_b[n_p:, :D_IDX].reshape(bs, n, D_IDX)),
                         past, s_real, min(TOPK_MAX, s_real // 4), n, bk_s)
    tb_p = 256
    q0 = cols['qkv']
    halo_p = jnp.zeros((1, HALO, QKV_D), F32)
    s0_p = jnp.zeros((1, H_D, DK_D, DV_D), F32)
    od_p, s_p = gdn_mixer(proj, n_p, cols, halo_p, s0_p, conv_w, a_log, dt_bias, norm_w, CHUNK, tb_p)
    proj_s = proj[n_p:]
    halo_s = jnp.concatenate([jnp.zeros((bs, HALO - (CONV_W - 1), QKV_D), F32), state_d_conv], 1)
    od_s, s_s = gdn_mixer(proj_s, bs * n, cols, halo_s, state_d_ssm, conv_w, a_log, dt_bias, norm_w, n, n)
    dh = H_C * HEAD_DIM
    mp = matmul_concat(oc_p.reshape(n_p, dh), od_p, w_out)
    ms = matmul_concat(oc_s.reshape(bs * n, dh), od_s, w_out)
    v0 = cols['vc']
    qkv_s = proj_s[:, q0:q0 + QKV_D].reshape(bs, n, QKV_D)
    caches = (kc_f[:n_p].reshape(1, n_p, KV_C, HEAD_DIM), proj[:n_p, v0:v0 + dk].reshape(1, n_p, KV_C, HEAD_DIM),
              ki_f[:n_p, :D_IDX][None],
              kc_f[n_p:].reshape(bs, n, KV_C, HEAD_DIM), proj_s[:, v0:v0 + dk].reshape(bs, n, KV_C, HEAD_DIM),
              ki_f[n_p:, :D_IDX].reshape(bs, n, D_IDX),
              s_p, proj[n_p - (CONV_W - 1):n_p, q0:q0 + QKV_D][None],
              s_s, qkv_s[:, n - (CONV_W - 1):])
    return mp, ms, caches


def kernel(x_prompt, x_sample, cache_a_k, cache_a_v, cache_b_k, cache_b_v, cache_c_k, cache_c_v,
           cache_c_idx_k, state_d_ssm, state_d_conv, w_in_even, w_out_even, b_rel_bias, w_in_odd,
           w_out_odd, d_conv_w, d_a_log, d_dt_bias, d_norm_w, w_router, router_bias, moe_w_gate,
           moe_w_up, moe_w_down, ln_g, ln_b):
    bp, t, d = x_prompt.shape
    bs, n, _ = x_sample.shape
    assert bp == 1
    n_p = bp * t
    w_in_even = w_in_even.astype(BF16)
    w_out_even = w_out_even.astype(BF16)
    w_in_odd_r = reorder_w_in_odd(w_in_odd)
    w_out_odd = w_out_odd.astype(BF16)
    wr_pad = jnp.pad(w_router.astype(BF16), ((0, 0), (0, LANES - N_EXPERTS)))
    rb_pad = jnp.pad(router_bias.astype(F32), (0, LANES - N_EXPERTS)).reshape(1, LANES)
    x = jnp.concatenate([x_prompt.reshape(n_p, d), x_sample.reshape(bs * n, d)], 0)
    for layer in range(DEPTH):
        if layer % 2 == 0:
            xp, xs = x[:n_p].reshape(bp, t, d), x[n_p:].reshape(bs, n, d)
            mp, ms, (a_k_p, a_v_p, a_k_s, a_v_s, b_k_p, b_v_p, b_k_s, b_v_s) = even_layer(
                xp, xs, cache_a_k, cache_a_v, cache_b_k, cache_b_v, w_in_even, w_out_even, b_rel_bias)
        else:
            mp, ms, (c_k_p, c_v_p, c_i_p, c_k_s, c_v_s, c_i_s, d_s_p, d_c_p, d_s_s, d_c_s) = odd_layer(
                x, n_p, bs, n, cache_c_k, cache_c_v, cache_c_idx_k, state_d_ssm, state_d_conv,
                w_in_odd_r, w_out_odd, d_conv_w, d_a_log, d_dt_bias, d_norm_w)
        m = jnp.concatenate([mp.reshape(n_p, d), ms.reshape(bs * n, d)], 0)
        x = moe_block(x, m, ln_g[layer], ln_b[layer], wr_pad, rb_pad,
                      moe_w_gate, moe_w_up, moe_w_down, layer)
    xp, xs = x[:n_p].reshape(bp, t, d), x[n_p:].reshape(bs, n, d)
    return (xp, xs, a_k_p, a_v_p, a_k_s, a_v_s, b_k_p, b_v_p, b_k_s, b_v_s,
            c_k_p, c_v_p, c_i_p, c_k_s, c_v_s, c_i_s, d_s_p, d_c_p, d_s_s, d_c_s)


def _old_project(x, w_bf16):
    b, t, d = x.shape
    n = w_bf16.shape[1]
    n_pad = (-n) % 1024
    if n_pad:
        w_bf16 = jnp.pad(w_bf16, ((0, 0), (0, n_pad)))
    y = matmul(x.reshape(b * t, d).astype(BF16), w_bf16)
    return y[:, :n].reshape(b, t, n)


def split_cols(h, sizes):
    cuts, acc = [], 0
    for s in sizes[:-1]:
        acc += s
        cuts.append(acc)
    return jnp.split(h, cuts, axis=-1)


def rope(x, pos):
    half = x.shape[-1] // 2
    inv = ROPE_THETA ** (-jnp.arange(half, dtype=F32) / half)
    ang = pos.astype(F32)[:, None] * inv[None, :]
    cos, sin = jnp.cos(ang)[None, :, None, :], jnp.sin(ang)[None, :, None, :]
    xf = x.astype(F32)
    x1, x2 = xf[..., :half], xf[..., half:]
    return jnp.concatenate([x1 * cos - x2 * sin, x2 * cos + x1 * sin], -1).astype(x.dtype)


def dsa_attend(qc, qi, wi, q_pos, kc, vc, ki, k_pos, topk):
    B, Q = qc.shape[:2]
    logits = jnp.einsum('bqhe,bse->bqhs', qi.astype(F32), ki.astype(F32))
    score = jnp.einsum('bqh,bqhs->bqs', wi.astype(F32), jax.nn.relu(logits))
    adm = jnp.broadcast_to(((k_pos[None, :] // CHUNK) <= (q_pos[:, None] // CHUNK))[None], score.shape)
    _, sel = lax.top_k(jnp.where(adm, score, NEG), topk)
    sel_ok = jnp.take_along_axis(adm, sel, axis=-1)
    gather = jax.vmap(lambda rows, i: rows[i])
    ks, vs = gather(kc, sel), gather(vc, sel)
    qg = qc.reshape(B, Q, KV_C, H_C // KV_C, HEAD_DIM).astype(F32)
    s = jnp.einsum('bqgrd,bqkgd->bqgrk', qg, ks.astype(F32)) * (HEAD_DIM ** -0.5)
    s = jnp.where(sel_ok[:, :, None, None, :], s, NEG)
    p = jax.nn.softmax(s, axis=-1)
    o = jnp.einsum('bqgrk,bqkgd->bqgrd', p, vs.astype(F32))
    return o.reshape(B, Q, H_C, HEAD_DIM).astype(qc.dtype)


def causal_conv(raw, buf, conv_w):
    xp = jnp.concatenate([buf, raw], 1)
    out = lax.conv_general_dilated(xp, conv_w[:, None, :], window_strides=(1,), padding='VALID',
                                   dimension_numbers=('NWC', 'WIO', 'NWC'),
                                   feature_group_count=raw.shape[-1])
    return jax.nn.silu(out), xp[:, xp.shape[1] - (CONV_W - 1):]


def l2norm(t):
    return t * lax.rsqrt(jnp.sum(t * t, -1, keepdims=True) + RMS_EPS)


def gdn_inputs(qkv_raw, buf, a_raw, b_raw, conv_w, a_log, dt_bias):
    qkv, new_buf = causal_conv(qkv_raw, buf, conv_w)
    B, T, _ = qkv.shape
    q, k, v = split_cols(qkv, (H_D * DK_D, H_D * DK_D, H_D * DV_D))
    q = l2norm(q.reshape(B, T, H_D, DK_D).astype(F32)) * (DK_D ** -0.5)
    k = l2norm(k.reshape(B, T, H_D, DK_D).astype(F32))
    v = v.reshape(B, T, H_D, DV_D).astype(F32)
    g = -jnp.exp(a_log.astype(F32)) * jax.nn.softplus(a_raw.astype(F32) + dt_bias.astype(F32))
    beta = jax.nn.sigmoid(b_raw.astype(F32))
    return q, k, v, g, beta, new_buf


def gdn_chunk(S, q, k, v, g, beta):
    C = q.shape[1]
    qh, kh, vh = q.transpose(0, 2, 1, 3), k.transpose(0, 2, 1, 3), v.transpose(0, 2, 1, 3)
    G = jnp.cumsum(g, axis=1).transpose(0, 2, 1)
    bt = beta.transpose(0, 2, 1)
    incl = jnp.tril(jnp.ones((C, C), dtype=bool))
    strict = jnp.tril(jnp.ones((C, C), dtype=bool), -1)
    decay = jnp.exp(jnp.where(incl, G[..., :, None] - G[..., None, :], -jnp.inf))
    kk = jnp.einsum('bhtd,bhjd->bhtj', kh, kh)
    a_mat = jnp.where(strict, bt[..., :, None] * kk * decay, 0.0) + jnp.eye(C, dtype=F32)
    rhs = jnp.concatenate([vh * bt[..., None], kh * (bt * jnp.exp(G))[..., None]], -1)
    sol = lax.linalg.triangular_solve(a_mat, rhs, left_side=True, lower=True, unit_diagonal=True)
    u = sol[..., :DV_D] - jnp.einsum('bhtk,bhkv->bhtv', sol[..., DV_D:], S)
    qk = jnp.einsum('bhtd,bhjd->bhtj', qh, kh) * decay
    o = jnp.einsum('bhtk,bhkv->bhtv', qh * jnp.exp(G)[..., None], S) + jnp.einsum('bhtj,bhjv->bhtv', qk, u)
    g_last = G[..., -1]
    S_new = S * jnp.exp(g_last)[..., None, None] + jnp.einsum(
        'bhjk,bhjv->bhkv', kh * jnp.exp(g_last[..., None] - G)[..., None], u)
    return S_new, o.transpose(0, 2, 1, 3)


def gdn_output(o, gate, norm_w):
    B, T = o.shape[:2]
    o = o * lax.rsqrt(jnp.mean(o * o, -1, keepdims=True) + RMS_EPS) * norm_w.astype(F32)
    return (o.reshape(B, T, H_D * DV_D) * jax.nn.silu(gate.astype(F32))).astype(gate.dtype)


def odd_project(x, w_in, pos):
    B, T, _ = x.shape
    qc, kc, vc, qi, ki, wi, qkv_d, a_d, b_d, g_d = split_cols(project(x, w_in), ODD_SIZES)
    qc = rope(qc.reshape(B, T, H_C, HEAD_DIM), pos)
    kc = rope(kc.reshape(B, T, KV_C, HEAD_DIM), pos)
    vc = vc.reshape(B, T, KV_C, HEAD_DIM)
    qi = rope(qi.reshape(B, T, H_IDX, D_IDX), pos)
    ki = rope(ki[:, :, None, :], pos)[:, :, 0]
    return qc, kc, vc, qi, ki, wi, qkv_d, a_d, b_d, g_d


def odd_prompt(x, w_in, w_out, conv_w, a_log, dt_bias, norm_w):
    B, T, _ = x.shape
    pos = jnp.arange(T)
    qc, kc, vc, qi, ki, wi, qkv_d, a_d, b_d, g_d = odd_project(x, w_in, pos)
    nq = T // Q_BLOCK
    topk = min(TOPK_MAX, T // 4)
    blk = lambda t: t.reshape((B, nq, Q_BLOCK) + t.shape[2:]).swapaxes(0, 1)
    oc = lax.map(lambda a: dsa_attend(a[0], a[1], a[2], a[3], kc, vc, ki, pos, topk),
                 (blk(qc), blk(qi), blk(wi), pos.reshape(nq, Q_BLOCK)))
    oc = oc.swapaxes(0, 1).reshape(B, T, H_C * HEAD_DIM)
    buf0 = jnp.zeros((B, CONV_W - 1, QKV_D), qkv_d.dtype)
    q, k, v, g, beta, conv_buf = gdn_inputs(qkv_d, buf0, a_d, b_d, conv_w, a_log, dt_bias)
    nc = T // CHUNK
    chk = lambda t: t.reshape((B, nc, CHUNK) + t.shape[2:]).swapaxes(0, 1)
    S0 = jnp.zeros((B, H_D, DK_D, DV_D), F32)
    S, od = lax.scan(lambda S, c: gdn_chunk(S, *c), S0, (chk(q), chk(k), chk(v), chk(g), chk(beta)))
    od = gdn_output(od.swapaxes(0, 1).reshape(B, T, H_D, DV_D), g_d, norm_w)
    y = project(jnp.concatenate([oc, od], -1), w_out)
    return y, kc, vc, ki, S.astype(x.dtype), conv_buf


def odd_sample(x, cache_c_k, cache_c_v, cache_c_idx_k, state_d_ssm, state_d_conv,
               w_in, w_out, conv_w, a_log, dt_bias, norm_w):
    B, n, _ = x.shape
    past = cache_c_k.shape[1]
    q_pos = past + jnp.arange(n)
    qc, kc, vc, qi, ki, wi, qkv_d, a_d, b_d, g_d = odd_project(x, w_in, q_pos)
    topk = min(TOPK_MAX, (past + n) // 4)
    oc = dsa_attend(qc, qi, wi, q_pos, jnp.concatenate([cache_c_k, kc], 1),
                    jnp.concatenate([cache_c_v, vc], 1), jnp.concatenate([cache_c_idx_k, ki], 1),
                    jnp.arange(past + n), topk).reshape(B, n, H_C * HEAD_DIM)
    q, k, v, g, beta, conv_buf = gdn_inputs(qkv_d, state_d_conv, a_d, b_d, conv_w, a_log, dt_bias)
    S, od = gdn_chunk(state_d_ssm.astype(F32), q, k, v, g, beta)
    od = gdn_output(od, g_d, norm_w)
    y = project(jnp.concatenate([oc, od], -1), w_out)
    return y, kc, vc, ki, S.astype(state_d_ssm.dtype), conv_buf


def _old_kernel(x_prompt, x_sample, cache_a_k, cache_a_v, cache_b_k, cache_b_v, cache_c_k, cache_c_v,
                cache_c_idx_k, state_d_ssm, state_d_conv, w_in_even, w_out_even, b_rel_bias, w_in_odd,
                w_out_odd, d_conv_w, d_a_log, d_dt_bias, d_norm_w, w_router, router_bias, moe_w_gate,
                moe_w_up, moe_w_down, ln_g, ln_b):
    bp, t, d = x_prompt.shape
    bs, n, _ = x_sample.shape
    n_p = bp * t
    w_in_even = w_in_even.astype(BF16)
    w_out_even = w_out_even.astype(BF16)
    w_in_odd = w_in_odd.astype(BF16)
    w_out_odd = w_out_odd.astype(BF16)
    wr_pad = jnp.pad(w_router.astype(BF16), ((0, 0), (0, LANES - N_EXPERTS)))
    rb_pad = jnp.pad(router_bias.astype(F32), (0, LANES - N_EXPERTS)).reshape(1, LANES)
    x = jnp.concatenate([x_prompt.reshape(n_p, d), x_sample.reshape(bs * n, d)], 0)
    for layer in range(DEPTH):
        xp, xs = x[:n_p].reshape(bp, t, d), x[n_p:].reshape(bs, n, d)
        if layer % 2 == 0:
            mp, ms, (a_k_p, a_v_p, a_k_s, a_v_s, b_k_p, b_v_p, b_k_s, b_v_s) = even_layer(
                xp, xs, cache_a_k, cache_a_v, cache_b_k, cache_b_v, w_in_even, w_out_even, b_rel_bias)
        else:
            mp, c_k_p, c_v_p, c_i_p, d_s_p, d_c_p = odd_prompt(xp, w_in_odd, w_out_odd, d_conv_w,
                                                                d_a_log, d_dt_bias, d_norm_w)
            ms, c_k_s, c_v_s, c_i_s, d_s_s, d_c_s = odd_sample(xs, cache_c_k, cache_c_v, cache_c_idx_k,
                                                                state_d_ssm, state_d_conv, w_in_odd,
                                                                w_out_odd, d_conv_w, d_a_log, d_dt_bias,
                                                                d_norm_w)
        m = jnp.concatenate([mp.reshape(n_p, d), ms.reshape(bs * n, d)], 0)
        x = moe_block(x, m, ln_g[layer], ln_b[layer], wr_pad, rb_pad,
                      moe_w_gate, moe_w_up, moe_w_down, layer)
    xp, xs = x[:n_p].reshape(bp, t, d), x[n_p:].reshape(bs, n, d)
    return (xp, xs, a_k_p, a_v_p, a_k_s, a_v_s, b_k_p, b_v_p, b_k_s, b_v_s,
            c_k_p, c_v_p, c_i_p, c_k_s, c_v_s, c_i_s, d_s_p, d_c_p, d_s_s, d_c_s)
```

```python
import functools

import jax
import jax.numpy as jnp
import numpy as np
from jax import lax
from jax.experimental import pallas as pl
from jax.experimental.pallas import tpu as pltpu

D_MODEL = 4096
DEPTH = 2
CHUNK = 64
HEAD_DIM = 128
Q_BLOCK = 128
NEG = -1e30
H_A = 16
H_B = 16
N_LEFT_CHUNKS = 8
BAND_REACH = N_LEFT_CHUNKS * CHUNK
REL_CLIP = 128
EVEN_SIZES = (H_A * HEAD_DIM,) * 3 + (H_B * HEAD_DIM,) * 3
H_C = 16
KV_C = 4
H_IDX = 16
D_IDX = 64
TOPK_MAX = 256
H_D = 16
DK_D = 128
DV_D = 128
CONV_W = 4
QKV_D = H_D * (2 * DK_D + DV_D)
ODD_SIZES = (H_C * HEAD_DIM, KV_C * HEAD_DIM, KV_C * HEAD_DIM, H_IDX * D_IDX, D_IDX, H_IDX,
             QKV_D, H_D, H_D, H_D * DV_D)
N_EXPERTS = 16
N_GROUPS = 4
EXPERTS_PER_GROUP = N_EXPERTS // N_GROUPS
TOP_K = 2
D_EXPERT = 1024
ROPE_THETA = 10000.0
LN_EPS = 1e-5
RMS_EPS = 1e-6
DEEPNORM_ALPHA = (2 * DEPTH) ** 0.25

F32 = jnp.float32
BF16 = jnp.bfloat16
I32 = jnp.int32

F32_EXP_UNDERFLOW = -105.0
LANES = 128
VMEM_LIMIT_BYTES = 56 * 1024 * 1024


def _params(*sem):
    return pltpu.CompilerParams(dimension_semantics=sem, vmem_limit_bytes=VMEM_LIMIT_BYTES)


def _cast_rows(src_ref, dst_ref, dst_off=0, chunk=256):
    rows = src_ref.shape[0]
    assert rows % chunk == 0 and dst_off % chunk == 0

    def body(c, carry):
        r = pl.multiple_of(c * chunk, chunk)
        dst_ref[pl.ds(dst_off + r, chunk), :] = src_ref[pl.ds(r, chunk), :].astype(dst_ref.dtype)
        return carry
    lax.fori_loop(0, rows // chunk, body, 0)


def _matmul_kernel(x_ref, w_ref, o_ref):
    o_ref[...] = jnp.dot(x_ref[...], w_ref[...], preferred_element_type=F32)


def matmul(x, w, tm=512, tn=1024):
    m, k = x.shape
    n = w.shape[1]
    tm = min(tm, m)
    tn = min(tn, n)
    assert m % tm == 0 and n % tn == 0
    return pl.pallas_call(
        _matmul_kernel,
        grid=(n // tn, m // tm),
        in_specs=[pl.BlockSpec((tm, k), lambda j, i: (i, 0)),
                  pl.BlockSpec((k, tn), lambda j, i: (0, j))],
        out_specs=pl.BlockSpec((tm, tn), lambda j, i: (i, j)),
        out_shape=jax.ShapeDtypeStruct((m, n), F32),
        compiler_params=_params("arbitrary", "arbitrary"),
        name="proj_matmul",
    )(x, w)


def _matmul2_kernel(xa_ref, xb_ref, w_ref, o_ref):
    ka = xa_ref.shape[1]
    o_ref[...] = (jnp.dot(xa_ref[...], w_ref[:ka, :], preferred_element_type=F32)
                  + jnp.dot(xb_ref[...], w_ref[ka:, :], preferred_element_type=F32))


def matmul_concat(xa, xb, w, tm=512, tn=1024):
    m, ka = xa.shape
    kb = xb.shape[1]
    n = w.shape[1]
    tm = min(tm, m)
    assert m % tm == 0 and n % tn == 0
    return pl.pallas_call(
        _matmul2_kernel,
        grid=(n // tn, m // tm),
        in_specs=[pl.BlockSpec((tm, ka), lambda j, i: (i, 0)),
                  pl.BlockSpec((tm, kb), lambda j, i: (i, 0)),
                  pl.BlockSpec((ka + kb, tn), lambda j, i: (0, j))],
        out_specs=pl.BlockSpec((tm, tn), lambda j, i: (i, j)),
        out_shape=jax.ShapeDtypeStruct((m, n), F32),
        compiler_params=_params("arbitrary", "arbitrary"),
        name="out_matmul",
    )(xa, xb, w)


def _stage_rows(piece_refs, dst_ref):
    off = 0
    for ref in piece_refs:
        rows = ref.shape[1]
        if rows % 256 == 0 and off % 256 == 0:
            _cast_rows(ref.at[0], dst_ref, dst_off=off)
        else:
            dst_ref[off:off + rows, :] = ref[0].astype(dst_ref.dtype)
        off += rows
    if off < dst_ref.shape[0]:
        dst_ref[off:, :] = jnp.zeros((dst_ref.shape[0] - off, dst_ref.shape[1]), dst_ref.dtype)


def _sb_kernel(q_ref, *refs, n_pieces, bq, bk, q_off, tk, scale):
    k_refs, v_refs = refs[:n_pieces], refs[n_pieces:2 * n_pieces]
    u_ref, o_ref, kb_ref, vb_ref = refs[2 * n_pieces:]
    qi = pl.program_id(2)

    @pl.when(qi == 0)
    def _():
        _stage_rows(k_refs, kb_ref)
        _stage_rows(v_refs, vb_ref)

    q = q_ref[0].astype(BF16)
    r0 = qi * bq
    rows = q_off + r0 + lax.broadcasted_iota(I32, (bq, bk), 0)
    cols = lax.broadcasted_iota(I32, (bq, bk), 1)
    n_kb = jnp.minimum((q_off + r0 + bq + bk - 1) // bk, tk // bk)
    tri = u_ref[...]

    def live(carry):
        it, _, _, max_run = carry
        return (it < n_kb) & (max_run > F32_EXP_UNDERFLOW)

    def body(carry):
        it, acc, run, _ = carry
        start = pl.multiple_of((n_kb - 1 - it) * bk, bk)
        kblk = kb_ref[pl.ds(start, bk), :]
        vblk = vb_ref[pl.ds(start, bk), :]
        z = lax.dot_general(q, kblk, (((1,), (1,)), ((), ())), preferred_element_type=F32) * scale
        softplus = jnp.maximum(z, 0.0) + jnp.log(1.0 + jnp.exp(-jnp.abs(z)))
        earlier = (cols + start) < rows
        log_1mb = jnp.where(earlier, -softplus, 0.0)
        log_b = z - softplus
        hi = log_1mb.astype(BF16)
        lo = (log_1mb - hi.astype(F32)).astype(BF16)
        between = (jnp.dot(hi, tri, preferred_element_type=F32)
                   + jnp.dot(lo, tri, preferred_element_type=F32) + run)
        w = jnp.where(earlier, jnp.exp(log_b + between), 0.0)
        acc = acc + jnp.dot(w.astype(BF16), vblk, preferred_element_type=F32)
        run = run + jnp.sum(log_1mb, axis=1, keepdims=True)
        return it + 1, acc, run, jnp.max(run)

    _, acc, _, _ = lax.while_loop(
        live, body, (jnp.int32(0), jnp.zeros((bq, HEAD_DIM), F32), jnp.zeros((bq, 1), F32), jnp.float32(0.0)))
    o_ref[0] = acc.astype(o_ref.dtype)


def _piece_specs(pieces, which):
    specs = []
    for piece in pieces:
        arr, cb = piece[which], piece[2 + which]
        specs.append(pl.BlockSpec((1, arr.shape[1], HEAD_DIM), lambda b_, h, *_, cb=cb: (b_, 0, cb + h)))
    return specs


def stick_breaking_attention(q_arr, q_cb, kv_pieces, n_heads, q_off, bq, bk):
    b, tq, _ = q_arr.shape
    tk = -(-sum(p[0].shape[1] for p in kv_pieces) // bk) * bk
    assert tq % bq == 0
    tri = jnp.asarray(np.tril(np.ones((bk, bk), np.float32), -1), BF16)
    kern = functools.partial(_sb_kernel, n_pieces=len(kv_pieces), bq=bq, bk=bk, q_off=q_off, tk=tk,
                             scale=HEAD_DIM ** -0.5)
    return pl.pallas_call(
        kern,
        grid=(b, n_heads, tq // bq),
        in_specs=[pl.BlockSpec((1, bq, HEAD_DIM), lambda b_, h, i: (b_, i, q_cb + h))]
        + _piece_specs(kv_pieces, 0) + _piece_specs(kv_pieces, 1)
        + [pl.BlockSpec((bk, bk), lambda b_, h, i: (0, 0))],
        out_specs=pl.BlockSpec((1, bq, HEAD_DIM), lambda b_, h, i: (b_, i, h)),
        out_shape=jax.ShapeDtypeStruct((b, tq, n_heads * HEAD_DIM), BF16),
        scratch_shapes=[pltpu.VMEM((tk, HEAD_DIM), BF16), pltpu.VMEM((tk, HEAD_DIM), BF16)],
        compiler_params=_params("arbitrary", "arbitrary", "arbitrary"),
        name="stick_breaking",
    )(q_arr, *[p[0] for p in kv_pieces], *[p[1] for p in kv_pieces], tri)


def _softmax_pv(s, v):
    m = jnp.max(s, axis=1, keepdims=True)
    e = jnp.exp(s - m)
    p = e / jnp.sum(e, axis=1, keepdims=True)
    return jnp.dot(p.astype(BF16), v, preferred_element_type=F32)


def _band_prompt_kernel(q_ref, k_ref, v_ref, bias_ref, o_ref, kb_ref, vb_ref, *, bq, reach, scale):
    qi = pl.program_id(2)

    @pl.when(qi == 0)
    def _():
        kb_ref[:reach, :] = jnp.zeros((reach, HEAD_DIM), BF16)
        vb_ref[:reach, :] = jnp.zeros((reach, HEAD_DIM), BF16)
        _cast_rows(k_ref.at[0], kb_ref, dst_off=reach)
        _cast_rows(v_ref.at[0], vb_ref, dst_off=reach)

    win = reach + bq
    start = pl.multiple_of(qi * bq, bq)
    q = q_ref[0].astype(BF16)
    kw = kb_ref[pl.ds(start, win), :]
    vw = vb_ref[pl.ds(start, win), :]
    s = lax.dot_general(q, kw, (((1,), (1,)), ((), ())), preferred_element_type=F32) * scale
    k_pos = start - reach + lax.broadcasted_iota(I32, (bq, win), 1)
    s = jnp.where(k_pos >= 0, s + bias_ref[0], NEG)
    o_ref[0] = _softmax_pv(s, vw).astype(o_ref.dtype)


def band_attention_prompt(proj, q_cb, k_cb, v_cb, rel_bias, bq=256):
    b, t, _ = proj.shape
    assert bq % CHUNK == 0 and t % bq == 0
    win = BAND_REACH + bq
    i = np.arange(bq)[:, None]
    j = np.arange(win)[None, :]
    qc, kc = i // CHUNK, j // CHUNK - N_LEFT_CHUNKS
    in_band = (kc <= qc) & (kc >= qc - N_LEFT_CHUNKS)
    period = win + bq
    k = np.arange(period)
    i_minus_j = np.where(k < win, -k, period - k)
    rel = np.clip(BAND_REACH + i_minus_j, -REL_CLIP, REL_CLIP) + REL_CLIP
    vec = rel_bias.astype(F32)[:, rel]
    toep = jnp.tile(vec, (1, bq))[:, :bq * (period - 1)].reshape(-1, bq, period - 1)[:, :, :win]
    bias = jnp.where(jnp.asarray(in_band)[None], toep, NEG)
    kern = functools.partial(_band_prompt_kernel, bq=bq, reach=BAND_REACH, scale=HEAD_DIM ** -0.5)
    return pl.pallas_call(
        kern,
        grid=(b, H_B, t // bq),
        in_specs=[pl.BlockSpec((1, bq, HEAD_DIM), lambda b_, h, i_: (b_, i_, q_cb + h)),
                  pl.BlockSpec((1, t, HEAD_DIM), lambda b_, h, i_: (b_, 0, k_cb + h)),
                  pl.BlockSpec((1, t, HEAD_DIM), lambda b_, h, i_: (b_, 0, v_cb + h)),
                  pl.BlockSpec((1, bq, win), lambda b_, h, i_: (h, 0, 0))],
        out_specs=pl.BlockSpec((1, bq, HEAD_DIM), lambda b_, h, i_: (b_, i_, h)),
        out_shape=jax.ShapeDtypeStruct((b, t, H_B * HEAD_DIM), BF16),
        scratch_shapes=[pltpu.VMEM((BAND_REACH + t, HEAD_DIM), BF16),
                        pltpu.VMEM((BAND_REACH + t, HEAD_DIM), BF16)],
        compiler_params=_params("arbitrary", "arbitrary", "arbitrary"),
        name="band_prompt",
    )(proj, proj, proj, bias)


def _band_core_kernel(q_ref, *refs, n_pieces, scale):
    k_refs, v_refs = refs[:n_pieces], refs[n_pieces:2 * n_pieces]
    bias_ref, o_ref, kb_ref, vb_ref = refs[2 * n_pieces:]
    _stage_rows(k_refs, kb_ref)
    _stage_rows(v_refs, vb_ref)
    q = q_ref[0].astype(BF16)
    s = lax.dot_general(q, kb_ref[...], (((1,), (1,)), ((), ())), preferred_element_type=F32) * scale
    o_ref[0] = _softmax_pv(s + bias_ref[0], vb_ref[...]).astype(o_ref.dtype)


def band_attention_core(q_arr, q_cb, kv_pieces, bias):
    b, cq, _ = q_arr.shape
    h_n, _, ck = bias.shape
    kern = functools.partial(_band_core_kernel, n_pieces=len(kv_pieces), scale=HEAD_DIM ** -0.5)
    return pl.pallas_call(
        kern,
        grid=(b, h_n),
        in_specs=[pl.BlockSpec((1, cq, HEAD_DIM), lambda b_, h: (b_, 0, q_cb + h))]
        + _piece_specs(kv_pieces, 0) + _piece_specs(kv_pieces, 1)
        + [pl.BlockSpec((1, cq, ck), lambda b_, h: (h, 0, 0))],
        out_specs=pl.BlockSpec((1, cq, HEAD_DIM), lambda b_, h: (b_, 0, h)),
        out_shape=jax.ShapeDtypeStruct((b, cq, h_n * HEAD_DIM), BF16),
        scratch_shapes=[pltpu.VMEM((ck, HEAD_DIM), BF16), pltpu.VMEM((ck, HEAD_DIM), BF16)],
        compiler_params=_params("arbitrary", "arbitrary"),
        name="band_core",
    )(q_arr, *[p[0] for p in kv_pieces], *[p[1] for p in kv_pieces], bias)


def _layer_norm_rows(h, g, b):
    mu = jnp.mean(h, axis=-1, keepdims=True)
    d = h - mu
    var = jnp.mean(d * d, axis=-1, keepdims=True)
    return d * lax.rsqrt(var + LN_EPS) * g + b


def _ln_router_kernel(x_ref, m_ref, g_ref, b_ref, wr_ref, rb_ref, x1_ref, gate_ref, idx_ref):
    y = _layer_norm_rows(DEEPNORM_ALPHA * x_ref[...] + m_ref[...], g_ref[...], b_ref[...])
    x1_ref[...] = y
    tb = y.shape[0]
    logits = jnp.dot(y.astype(BF16), wr_ref[...], preferred_element_type=F32)
    lane = lax.broadcasted_iota(I32, (tb, LANES), 1)
    lane_f = lane.astype(F32)
    real = lane < N_EXPERTS
    lg = jnp.where(real, logits, -jnp.inf)
    e = jnp.exp(lg - jnp.max(lg, axis=1, keepdims=True))
    aff = e / jnp.sum(e, axis=1, keepdims=True)
    sel = jnp.where(real, aff + rb_ref[...], -jnp.inf)
    group = lane // EXPERTS_PER_GROUP
    best = jnp.zeros((tb, 1), I32)
    best_v = jnp.max(jnp.where(group == 0, sel, -jnp.inf), axis=1, keepdims=True)
    for gi in range(1, N_GROUPS):
        gv = jnp.max(jnp.where(group == gi, sel, -jnp.inf), axis=1, keepdims=True)
        upd = gv > best_v
        best = jnp.where(upd, gi, best)
        best_v = jnp.where(upd, gv, best_v)
    cand = jnp.where((group == best) & real, sel, NEG)
    m1 = jnp.max(cand, axis=1, keepdims=True)
    i1 = jnp.min(jnp.where(cand == m1, lane_f, float(LANES)), axis=1, keepdims=True)
    cand2 = jnp.where(lane_f == i1, -jnp.inf, cand)
    m2 = jnp.max(cand2, axis=1, keepdims=True)
    i2 = jnp.min(jnp.where(cand2 == m2, lane_f, float(LANES)), axis=1, keepdims=True)
    g1 = jnp.sum(jnp.where(lane_f == i1, aff, 0.0), axis=1, keepdims=True)
    g2 = jnp.sum(jnp.where(lane_f == i2, aff, 0.0), axis=1, keepdims=True)
    tot = g1 + g2
    gate_ref[...] = jnp.where(lane == 0, g1 / tot, jnp.where(lane == 1, g2 / tot, 0.0))
    idx_ref[...] = jnp.where(lane == 0, i1, jnp.where(lane == 1, i2, 0.0)).astype(I32)


def ln_router(x, m, g, b, w_router_pad, router_bias_pad, tb=256):
    n, d = x.shape
    assert n % tb == 0
    row = pl.BlockSpec((tb, d), lambda i: (i, 0))
    vec = pl.BlockSpec((1, d), lambda i: (0, 0))
    meta = pl.BlockSpec((tb, LANES), lambda i: (i, 0))
    return pl.pallas_call(
        _ln_router_kernel,
        grid=(n // tb,),
        in_specs=[row, row, vec, vec,
                  pl.BlockSpec((d, LANES), lambda i: (0, 0)),
                  pl.BlockSpec((1, LANES), lambda i: (0, 0))],
        out_specs=[row, meta, meta],
        out_shape=[jax.ShapeDtypeStruct((n, d), F32),
                   jax.ShapeDtypeStruct((n, LANES), F32),
                   jax.ShapeDtypeStruct((n, LANES), I32)],
        compiler_params=_params("arbitrary"),
        name="ln_router",
    )(x, m, g.reshape(1, d), b.reshape(1, d), w_router_pad, router_bias_pad)


def _wait_rows(src_hbm, dst, sem, n):
    def body(r, c):
        pltpu.make_async_copy(src_hbm.at[pl.ds(0, 1), :], dst.at[pl.ds(0, 1), :], sem).wait()
        return c
    lax.fori_loop(0, n, body, 0)


def _gather_cast_kernel(tok_ref, na_ref, x_hbm, o_ref, buf, sem):
    tm = buf.shape[0]
    i = pl.program_id(0)
    base = i * tm

    @pl.when(i < na_ref[0])
    def _():
        def issue(r, c):
            pltpu.make_async_copy(x_hbm.at[pl.ds(tok_ref[base + r], 1), :], buf.at[pl.ds(r, 1), :], sem).start()
            return c
        lax.fori_loop(0, tm, issue, 0, unroll=8)
        _wait_rows(x_hbm, buf, sem, tm)
        _cast_rows(buf, o_ref)

    @pl.when(i >= na_ref[0])
    def _():
        o_ref[...] = jnp.zeros(o_ref.shape, o_ref.dtype)


def gather_rows_bf16(x, row_token, n_active, tm):
    n, d = x.shape
    r = row_token.shape[0]
    return pl.pallas_call(
        _gather_cast_kernel,
        grid_spec=pltpu.PrefetchScalarGridSpec(
            num_scalar_prefetch=2,
            grid=(r // tm,),
            in_specs=[pl.BlockSpec(memory_space=pl.ANY)],
            out_specs=pl.BlockSpec((tm, d), lambda i, tok, na: (i, 0)),
            scratch_shapes=[pltpu.VMEM((tm, d), F32), pltpu.SemaphoreType.DMA(())]),
        out_shape=jax.ShapeDtypeStruct((r, d), BF16),
        compiler_params=_params("arbitrary"),
        name="moe_gather",
    )(row_token, n_active, x)


def _expert_changed(te_ref, i):
    return (i == 0) | (te_ref[i] != te_ref[jnp.maximum(i - 1, 0)])


def _moe_up_kernel(te_ref, na_ref, xs_ref, wg_ref, wu_ref, h_ref, wgb, wub):
    i = pl.program_id(1)

    @pl.when(_expert_changed(te_ref, i))
    def _():
        _cast_rows(wg_ref, wgb)
        _cast_rows(wu_ref, wub)

    @pl.when(i < na_ref[0])
    def _():
        x = xs_ref[...]
        a = jnp.dot(x, wgb[...], preferred_element_type=F32)
        u = jnp.dot(x, wub[...], preferred_element_type=F32)
        h_ref[...] = (a * (1.0 / (1.0 + jnp.exp(-a))) * u).astype(h_ref.dtype)

    @pl.when(i >= na_ref[0])
    def _():
        h_ref[...] = jnp.zeros(h_ref.shape, h_ref.dtype)


def _moe_down_kernel(te_ref, na_ref, h_ref, wd_ref, gate_ref, y_ref, wdb):
    i = pl.program_id(1)

    @pl.when(_expert_changed(te_ref, i))
    def _():
        _cast_rows(wd_ref, wdb)

    @pl.when(i < na_ref[0])
    def _():
        y_ref[...] = gate_ref[...] * jnp.dot(h_ref[...], wdb[...], preferred_element_type=F32)

    @pl.when(i >= na_ref[0])
    def _():
        y_ref[...] = jnp.zeros(y_ref.shape, y_ref.dtype)


def moe_grouped(xs, gate_sorted, tile_expert, n_active, w_gate, w_up, w_down, layer, tm, tf=256, tn=1024):
    r, d = xs.shape
    n_tiles = r // tm
    de = w_gate.shape[3]
    tf, tn = min(tf, de), min(tn, d)
    h = pl.pallas_call(
        _moe_up_kernel,
        grid_spec=pltpu.PrefetchScalarGridSpec(
            num_scalar_prefetch=2,
            grid=(de // tf, n_tiles),
            in_specs=[pl.BlockSpec((tm, d), lambda f, i, te, na: (i, 0)),
                      pl.BlockSpec((None, None, d, tf), lambda f, i, te, na: (layer, te[i], 0, f)),
                      pl.BlockSpec((None, None, d, tf), lambda f, i, te, na: (layer, te[i], 0, f))],
            out_specs=pl.BlockSpec((tm, tf), lambda f, i, te, na: (i, f)),
            scratch_shapes=[pltpu.VMEM((d, tf), BF16), pltpu.VMEM((d, tf), BF16)]),
        out_shape=jax.ShapeDtypeStruct((r, de), BF16),
        compiler_params=_params("arbitrary", "arbitrary"),
        name="moe_up",
    )(tile_expert, n_active, xs, w_gate, w_up)
    return pl.pallas_call(
        _moe_down_kernel,
        grid_spec=pltpu.PrefetchScalarGridSpec(
            num_scalar_prefetch=2,
            grid=(d // tn, n_tiles),
            in_specs=[pl.BlockSpec((tm, de), lambda n, i, te, na: (i, 0)),
                      pl.BlockSpec((None, None, de, tn), lambda n, i, te, na: (layer, te[i], 0, n)),
                      pl.BlockSpec((tm, 1), lambda n, i, te, na: (i, 0))],
            out_specs=pl.BlockSpec((tm, tn), lambda n, i, te, na: (i, n)),
            scratch_shapes=[pltpu.VMEM((de, tn), BF16)]),
        out_shape=jax.ShapeDtypeStruct((r, d), F32),
        compiler_params=_params("arbitrary", "arbitrary"),
        name="moe_down",
    )(tile_expert, n_active, h, w_down, gate_sorted)


def _combine_ln_kernel(pos_ref, y_hbm, x_ref, g_ref, b_ref, xo_ref, buf, sem):
    tb = x_ref.shape[0]
    base = pl.program_id(0) * (2 * tb)

    def issue(r, c):
        pltpu.make_async_copy(y_hbm.at[pl.ds(pos_ref[base + r], 1), :], buf.at[pl.ds(r, 1), :], sem).start()
        return c
    lax.fori_loop(0, 2 * tb, issue, 0)
    _wait_rows(y_hbm, buf, sem, 2 * tb)
    f = buf[:tb, :] + buf[tb:, :]
    xo_ref[...] = _layer_norm_rows(DEEPNORM_ALPHA * x_ref[...] + f, g_ref[...], b_ref[...])


def combine_ln(y_sorted, pos_blocks, x1, g, b, tb):
    n, d = x1.shape
    row = pl.BlockSpec((tb, d), lambda i, pos: (i, 0))
    vec = pl.BlockSpec((1, d), lambda i, pos: (0, 0))
    return pl.pallas_call(
        _combine_ln_kernel,
        grid_spec=pltpu.PrefetchScalarGridSpec(
            num_scalar_prefetch=1,
            grid=(n // tb,),
            in_specs=[pl.BlockSpec(memory_space=pl.ANY), row, vec, vec],
            out_specs=row,
            scratch_shapes=[pltpu.VMEM((2 * tb, d), F32), pltpu.SemaphoreType.DMA(())]),
        out_shape=jax.ShapeDtypeStruct((n, d), F32),
        compiler_params=_params("arbitrary"),
        name="moe_combine_ln",
    )(pos_blocks, y_sorted, x1, g.reshape(1, d), b.reshape(1, d))


def moe_block(x, m, ln_g, ln_b, w_router_pad, router_bias_pad, w_gate, w_up, w_down, layer,
              tm=512, tb=128):
    n, d = x.shape
    x1, gate, idx = ln_router(x, m, ln_g[0], ln_b[0], w_router_pad, router_bias_pad)
    e_flat = idx[:, :TOP_K].reshape(-1)
    g_flat = gate[:, :TOP_K].reshape(-1)
    onehot = (e_flat[:, None] == jnp.arange(N_EXPERTS, dtype=I32)[None, :]).astype(I32)
    csum = jnp.cumsum(onehot, axis=0)
    rank = jnp.take_along_axis(csum, e_flat[:, None], axis=1)[:, 0] - 1
    counts = csum[-1]
    padded = (counts + tm - 1) // tm * tm
    ends = jnp.cumsum(padded)
    pos = (ends - padded)[e_flat] + rank
    n_pairs = TOP_K * n
    n_tiles = (n_pairs + N_EXPERTS * (tm - 1)) // tm + 1
    r = n_tiles * tm
    row_token = jnp.zeros((r,), I32).at[pos].set(jnp.arange(n_pairs, dtype=I32) // TOP_K)
    gate_sorted = jnp.zeros((r,), F32).at[pos].set(g_flat).reshape(r, 1)
    n_active = (ends[-1] // tm).astype(I32)
    tile = jnp.arange(n_tiles, dtype=I32)
    tile_expert = jnp.searchsorted(ends, jnp.minimum(tile, n_active - 1) * tm, side='right').astype(I32)
    tile_expert = jnp.minimum(tile_expert, N_EXPERTS - 1)
    n_active = n_active.reshape(1)
    xs = gather_rows_bf16(x1, row_token, n_active, tm)
    y_sorted = moe_grouped(xs, gate_sorted, tile_expert, n_active, w_gate, w_up, w_down, layer, tm)
    pos_blocks = pos.reshape(n // tb, tb, TOP_K).transpose(0, 2, 1).reshape(-1)
    return combine_ln(y_sorted, pos_blocks, x1, ln_g[1], ln_b[1], tb)


def even_layer(xp, xs, cache_a_k, cache_a_v, cache_b_k, cache_b_v, w_in, w_out, rel_bias):
    bp, t, d = xp.shape
    bs, n, _ = xs.shape
    past, lb = cache_a_k.shape[1], cache_b_k.shape[1]
    hd = H_A * HEAD_DIM
    proj_p = matmul(xp.reshape(bp * t, d).astype(BF16), w_in).reshape(bp, t, -1)
    proj_s = matmul(xs.reshape(bs * n, d).astype(BF16), w_in).reshape(bs, n, -1)
    nb = hd // HEAD_DIM
    qa_cb, ka_cb, va_cb, qb_cb, kb_cb, vb_cb = (s * nb for s in range(6))
    seg = lambda p, s: p[..., s * hd:(s + 1) * hd]
    oa_p = stick_breaking_attention(proj_p, qa_cb, [(proj_p, proj_p, ka_cb, va_cb)], H_A, 0, 256, 256)
    ob_p = band_attention_prompt(proj_p, qb_cb, kb_cb, vb_cb, rel_bias)
    mp = matmul_concat(oa_p.reshape(bp * t, hd), ob_p.reshape(bp * t, hd), w_out)
    flat = lambda c: c.reshape(bs, c.shape[1], hd)
    ka_s, va_s, kb_s, vb_s = seg(proj_s, 1), seg(proj_s, 2), seg(proj_s, 4), seg(proj_s, 5)
    oa_s = stick_breaking_attention(
        proj_s, qa_cb, [(flat(cache_a_k), flat(cache_a_v), 0, 0), (proj_s, proj_s, ka_cb, va_cb)],
        H_A, past, n, 256)
    ck = -(-(lb + n) // LANES) * LANES
    q_pos = past + np.arange(n)
    k_pos = np.concatenate([np.arange(past - lb, past), q_pos, np.full((ck - lb - n,), -1)])
    qc, kc = q_pos[:, None] // CHUNK, k_pos[None, :] // CHUNK
    valid = (kc <= qc) & (kc >= qc - N_LEFT_CHUNKS) & (k_pos[None, :] >= 0)
    rel = np.clip(q_pos[:, None] - k_pos[None, :], -REL_CLIP, REL_CLIP) + REL_CLIP
    bias_s = jnp.where(jnp.asarray(valid)[None], rel_bias.astype(F32)[:, rel], NEG)
    ob_s = band_attention_core(
        proj_s, qb_cb, [(flat(cache_b_k), flat(cache_b_v), 0, 0), (proj_s, proj_s, kb_cb, vb_cb)], bias_s)
    ms = matmul_concat(oa_s.reshape(bs * n, hd), ob_s.reshape(bs * n, hd), w_out)
    keep = min(BAND_REACH, t)
    heads = lambda a: a.reshape(a.shape[0], a.shape[1], H_A, HEAD_DIM)
    caches = (heads(seg(proj_p, 1)), heads(seg(proj_p, 2)), heads(ka_s), heads(va_s),
              heads(seg(proj_p, 4)[:, t - keep:]), heads(seg(proj_p, 5)[:, t - keep:]),
              heads(kb_s), heads(vb_s))
    return mp, ms, caches


INT_MIN = np.int32(-2 ** 31)


def _rope_kernel(qc_ref, kc_ref, vc_ref, qi_ref, ki_ref, wi_ref, c128_ref, s128_ref, c64_ref, s64_ref,
                 qc_o, kc_o, kcb_o, vcb_o, qi_o, ki_o, kib_o, wi_o):
    c128, s128, c64, s64 = c128_ref[...], s128_ref[...], c64_ref[...], s64_ref[...]
    lane = lax.broadcasted_iota(I32, c64.shape, 1)
    first_half = (lane % D_IDX) < (D_IDX // 2)

    def rot128(x):
        return x * c128 + pltpu.roll(x, HEAD_DIM // 2, 1) * s128

    def rot64(x):
        partner = jnp.where(first_half, pltpu.roll(x, LANES - D_IDX // 2, 1), pltpu.roll(x, D_IDX // 2, 1))
        return x * c64 + partner * s64

    for h in range(H_C):
        sl = slice(h * HEAD_DIM, (h + 1) * HEAD_DIM)
        qc_o[:, sl] = rot128(qc_ref[:, sl]).astype(BF16)
    for h in range(KV_C):
        sl = slice(h * HEAD_DIM, (h + 1) * HEAD_DIM)
        r = rot128(kc_ref[:, sl])
        kc_o[:, sl] = r
        kcb_o[:, sl] = r.astype(BF16)
    vcb_o[...] = vc_ref[...].astype(BF16)
    for j in range(H_IDX * D_IDX // LANES):
        r = rot64(qi_ref[:, j * LANES:(j + 1) * LANES]).astype(BF16)
        qi_o[2 * j] = r[:, :D_IDX]
        qi_o[2 * j + 1] = r[:, D_IDX:]
    r = rot64(ki_ref[...])
    ki_o[...] = r
    kib_o[...] = r.astype(BF16)
    wi_o[...] = wi_ref[...]


def rope_tables(pos):
    def tab(half, reps):
        inv = ROPE_THETA ** (-jnp.arange(half, dtype=F32) / half)
        ang = pos.astype(F32)[:, None] * inv[None, :]
        c, s = jnp.cos(ang), jnp.sin(ang)
        return jnp.tile(jnp.concatenate([c, c], 1), (1, reps)), jnp.tile(jnp.concatenate([-s, s], 1), (1, reps))
    c128, s128 = tab(HEAD_DIM // 2, 1)
    c64, s64 = tab(D_IDX // 2, 2)
    return c128, s128, c64, s64


def rope_split(proj, pos, cols, tb=256):
    n = proj.shape[0]
    assert n % tb == 0
    tabs = rope_tables(pos)
    dq, dk, di = H_C * HEAD_DIM, KV_C * HEAD_DIM, H_IDX * D_IDX
    spec = lambda w, c0: pl.BlockSpec((tb, w), lambda i: (i, c0 // w))
    row = lambda w: pl.BlockSpec((tb, w), lambda i: (i, 0))
    return pl.pallas_call(
        _rope_kernel,
        grid=(n // tb,),
        in_specs=[spec(dq, cols['qc']), spec(dk, cols['kc']), spec(dk, cols['vc']), spec(di, cols['qi']),
                  spec(LANES, cols['ki']), spec(LANES, cols['wi'])] + [row(LANES)] * 4,
        out_specs=[row(dq), row(dk), row(dk), row(dk),
                   pl.BlockSpec((H_IDX, tb, D_IDX), lambda i: (0, i, 0)), row(LANES), row(LANES), row(LANES)],
        out_shape=[jax.ShapeDtypeStruct((n, dq), BF16), jax.ShapeDtypeStruct((n, dk), F32),
                   jax.ShapeDtypeStruct((n, dk), BF16), jax.ShapeDtypeStruct((n, dk), BF16),
                   jax.ShapeDtypeStruct((H_IDX, n, D_IDX), BF16), jax.ShapeDtypeStruct((n, LANES), F32),
                   jax.ShapeDtypeStruct((n, LANES), BF16), jax.ShapeDtypeStruct((n, LANES), F32)],
        compiler_params=_params("arbitrary"),
        name="rope_split",
    )(proj, proj, proj, proj, proj, proj, *tabs)


def _dsa_kernel(qc_ref, qi_ref, wi_ref, kc_ref, vc_ref, ki_ref, tri_ref, o_ref, key_ref, mask_ref, *,
                bq, bk, q_off, s_real, s_pad, topk, scale):
    qi_blk = pl.program_id(1)
    r0 = qi_blk * bq
    adm_end = ((q_off + r0 + bq - 1) // CHUNK + 1) * CHUNK
    n_kb = jnp.minimum((adm_end + bk - 1) // bk, s_pad // bk)
    q_pos = q_off + r0 + lax.broadcasted_iota(I32, (bq, bk), 0)
    col = lax.broadcasted_iota(I32, (bq, bk), 1)
    kf = float(topk)

    def admissible(start):
        k_pos = col + start
        return ((k_pos // CHUNK) <= (q_pos // CHUNK)) & (k_pos < s_real)

    def lane_fold(x):
        acc = x[:, :LANES]
        for j in range(1, bk // LANES):
            acc = acc + x[:, j * LANES:(j + 1) * LANES]
        return acc

    def count(pred_fn):
        def body(kb, acc):
            start = pl.multiple_of(kb * bk, bk)
            blk = key_ref[:, pl.ds(start, bk)]
            return acc + lane_fold(jnp.where(pred_fn(blk), 1.0, 0.0))
        part = lax.fori_loop(0, n_kb, body, jnp.zeros((bq, LANES), F32))
        return jnp.sum(part, axis=1, keepdims=True)

    wi = wi_ref[0]
    wcols = [wi[:, h:h + 1] for h in range(H_IDX)]

    def score_body(kb, c):
        start = pl.multiple_of(kb * bk, bk)
        ki_blk = ki_ref[0, pl.ds(start, bk), :]
        acc = jnp.zeros((bq, bk), F32)
        for h in range(H_IDX):
            lg = lax.dot_general(qi_ref[h], ki_blk, (((1,), (1,)), ((), ())), preferred_element_type=F32)
            acc = acc + wcols[h] * jnp.maximum(lg, 0.0)
        bits = pltpu.bitcast(acc + 0.0, I32)
        key = jnp.where(bits < 0, bits ^ 0x7FFFFFFF, bits)
        key_ref[:, pl.ds(start, bk)] = jnp.where(admissible(start), key, INT_MIN)
        return c
    lax.fori_loop(0, n_kb, score_body, 0)

    prefix = jnp.where(count(lambda b: b >= 0) >= kf, 0, INT_MIN).astype(I32)

    def bit_body(it, prefix):
        cand = prefix | jnp.left_shift(jnp.int32(1), 30 - it)
        return jnp.where(count(lambda b: b >= cand) >= kf, cand, prefix)
    thr = lax.fori_loop(0, 31, bit_body, prefix)

    need = kf - count(lambda b: b > thr)
    tri = tri_ref[...]

    def mask_body(kb, run):
        start = pl.multiple_of(kb * bk, bk)
        blk = key_ref[:, pl.ds(start, bk)]
        tie = blk == thr
        tie_f = jnp.where(tie, 1.0, 0.0)
        before = jnp.dot(tie_f.astype(BF16), tri, preferred_element_type=F32) + run
        sel = ((blk > thr) | (tie & (before < need))) & admissible(start)
        mask_ref[:, pl.ds(start, bk)] = jnp.where(sel, 0.0, NEG)
        return run + jnp.sum(tie_f, axis=1, keepdims=True)
    lax.fori_loop(0, n_kb, mask_body, jnp.zeros((bq, 1), F32))

    rep = H_C // KV_C
    head = lambda g, r: slice((g * rep + r) * HEAD_DIM, (g * rep + r + 1) * HEAD_DIM)
    pair = 2
    for g0 in range(0, KV_C, pair):
        gs = range(g0, g0 + pair)
        qgs = [jnp.concatenate([qc_ref[0, :, head(g, r)] for r in range(rep)], axis=0) for g in gs]

        def att_body(kb, carry):
            start = pl.multiple_of(kb * bk, bk)
            mk = mask_ref[:, pl.ds(start, bk)]
            mk = jnp.concatenate([mk] * rep, axis=0)
            kv = [slice(g * HEAD_DIM, (g + 1) * HEAD_DIM) for g in gs]
            ss = [lax.dot_general(qg, kc_ref[0, pl.ds(start, bk), sl], (((1,), (1,)), ((), ())),
                                  preferred_element_type=F32) * scale + mk for qg, sl in zip(qgs, kv)]
            ms = [jnp.maximum(c[0], jnp.max(s, axis=1, keepdims=True)) for c, s in zip(carry, ss)]
            ps = [jnp.exp(s - m) for s, m in zip(ss, ms)]
            out = []
            for (m_old, l, acc), m, p, sl in zip(carry, ms, ps, kv):
                alpha = jnp.exp(m_old - m)
                out.append((m, alpha * l + jnp.sum(p, axis=1, keepdims=True),
                            alpha * acc + jnp.dot(p.astype(BF16), vc_ref[0, pl.ds(start, bk), sl],
                                                  preferred_element_type=F32)))
            return tuple(out)
        init = (jnp.full((rep * bq, 1), NEG, F32), jnp.zeros((rep * bq, 1), F32),
                jnp.zeros((rep * bq, HEAD_DIM), F32))
        res = lax.fori_loop(0, n_kb, att_body, (init,) * pair)
        for g, (_, l, acc) in zip(gs, res):
            o = acc / l
            for r in range(rep):
                o_ref[0, :, head(g, r)] = o[r * bq:(r + 1) * bq].astype(o_ref.dtype)


def dsa_attention(qc, qi_hm, wi, kc, vc, ki, q_off, s_real, topk, bq, bk):
    b, tq, _ = qc.shape
    s_pad = kc.shape[1]
    assert tq % bq == 0 and s_pad % bk == 0 and bk % LANES == 0
    nq = tq // bq
    tri = jnp.asarray(np.triu(np.ones((bk, bk), np.float32), 1), BF16)
    kern = functools.partial(_dsa_kernel, bq=bq, bk=bk, q_off=q_off, s_real=s_real, s_pad=s_pad,
                             topk=topk, scale=HEAD_DIM ** -0.5)
    once = dict(pipeline_mode=pl.Buffered(1))
    return pl.pallas_call(
        kern,
        grid=(b, nq),
        in_specs=[pl.BlockSpec((1, bq, H_C * HEAD_DIM), lambda b_, i: (b_, i, 0)),
                  pl.BlockSpec((H_IDX, bq, D_IDX), lambda b_, i: (0, b_ * nq + i, 0)),
                  pl.BlockSpec((1, bq, LANES), lambda b_, i: (b_, i, 0)),
                  pl.BlockSpec((1, s_pad, KV_C * HEAD_DIM), lambda b_, i: (b_, 0, 0), **once),
                  pl.BlockSpec((1, s_pad, KV_C * HEAD_DIM), lambda b_, i: (b_, 0, 0), **once),
                  pl.BlockSpec((1, s_pad, D_IDX), lambda b_, i: (b_, 0, 0), **once),
                  pl.BlockSpec((bk, bk), lambda b_, i: (0, 0))],
        out_specs=pl.BlockSpec((1, bq, H_C * HEAD_DIM), lambda b_, i: (b_, i, 0)),
        out_shape=jax.ShapeDtypeStruct((b, tq, H_C * HEAD_DIM), BF16),
        scratch_shapes=[pltpu.VMEM((bq, s_pad), I32), pltpu.VMEM((bq, s_pad), F32)],
        compiler_params=_params("arbitrary", "arbitrary"),
        name="dsa_attention",
    )(qc, qi_hm, wi, kc, vc, ki, tri)


HALO = 8


def _gdn_prep_kernel(x_ref, prev_ref, halo_ref, cw_ref, a_ref, b_ref, alog_ref, dtb_ref, o_ref, g_ref,
                     beta_ref, *, heads_per_blk, blocks_per_seq):
    j = pl.program_id(1)
    x = x_ref[...]
    tb = x.shape[0]
    halo = jnp.where(pl.program_id(0) % blocks_per_seq == 0, halo_ref[0], prev_ref[...])
    row = lax.broadcasted_iota(I32, halo.shape, 0)
    acc = x * cw_ref[CONV_W - 1:CONV_W, :]
    for k in range(1, CONV_W):
        rolled = pltpu.roll(x, k, 0)
        top = jnp.where(row < k, pltpu.roll(halo, k, 0), rolled[:HALO])
        shifted = jnp.concatenate([top, rolled[HALO:]], axis=0) if tb > HALO else top
        acc = acc + shifted * cw_ref[CONV_W - 1 - k:CONV_W - k, :]
    y = acc * (1.0 / (1.0 + jnp.exp(-acc)))
    seg = j // (H_D // heads_per_blk)
    scale = jnp.where(seg == 0, DK_D ** -0.5, 1.0)
    for h in range(heads_per_blk):
        sl = slice(h * LANES, (h + 1) * LANES)
        t = y[:, sl]
        nrm = t * lax.rsqrt(jnp.sum(t * t, axis=1, keepdims=True) + RMS_EPS) * scale
        o_ref[:, sl] = jnp.where(seg == 2, t, nrm)

    @pl.when(j == 0)
    def _():
        z = a_ref[...] + dtb_ref[...]
        softplus = jnp.maximum(z, 0.0) + jnp.log(1.0 + jnp.exp(-jnp.abs(z)))
        g_ref[...] = -jnp.exp(alog_ref[...]) * softplus
        beta_ref[...] = 1.0 / (1.0 + jnp.exp(-b_ref[...]))


def gdn_prep(proj, n_rows, halo, conv_w, a_log_pad, dt_bias_pad, qkv_col, a_col, b_col, tb):
    cb = 512
    hpb = cb // LANES
    bps = n_rows // tb // halo.shape[0]
    kern = functools.partial(_gdn_prep_kernel, heads_per_blk=hpb, blocks_per_seq=bps)
    vec = pl.BlockSpec((1, LANES), lambda i, j: (0, 0))
    return pl.pallas_call(
        kern,
        grid=(n_rows // tb, QKV_D // cb),
        in_specs=[pl.BlockSpec((tb, cb), lambda i, j: (i, qkv_col // cb + j)),
                  pl.BlockSpec((HALO, cb), lambda i, j: (jnp.maximum(i * (tb // HALO) - 1, 0),
                                                        qkv_col // cb + j)),
                  pl.BlockSpec((1, HALO, cb), lambda i, j: (i // bps, 0, j)),
                  pl.BlockSpec((CONV_W, cb), lambda i, j: (0, j)),
                  pl.BlockSpec((tb, LANES), lambda i, j: (i, a_col // LANES)),
                  pl.BlockSpec((tb, LANES), lambda i, j: (i, b_col // LANES)), vec, vec],
        out_specs=[pl.BlockSpec((tb, cb), lambda i, j: (i, j)),
                   pl.BlockSpec((tb, LANES), lambda i, j: (i, 0)),
                   pl.BlockSpec((tb, LANES), lambda i, j: (i, 0))],
        out_shape=[jax.ShapeDtypeStruct((n_rows, QKV_D), F32), jax.ShapeDtypeStruct((n_rows, LANES), F32),
                   jax.ShapeDtypeStruct((n_rows, LANES), F32)],
        compiler_params=_params("arbitrary", "arbitrary"),
        name="gdn_prep",
    )(proj, proj, halo, conv_w, proj, proj, a_log_pad, dt_bias_pad)


def _split3(x):
    a = x.astype(BF16)
    r = x - a.astype(F32)
    b = r.astype(BF16)
    c = (r - b.astype(F32)).astype(BF16)
    return a, b, c


def _dot_f32(a, b):
    a_hi, a_lo, _ = _split3(a)
    b_hi, b_lo, _ = _split3(b)
    return (jnp.dot(a_hi, b_hi, preferred_element_type=F32) + jnp.dot(a_hi, b_lo, preferred_element_type=F32)
            + jnp.dot(a_lo, b_hi, preferred_element_type=F32))


def _gdn_local_kernel(q_ref, k_ref, v_ref, g_ref, beta_ref, solv_ref, solk_ref, qe_ref, kd_ref, qkd_ref,
                      egl_ref, *, c, hb):
    assert hb == H_D
    ri = lax.broadcasted_iota(I32, (c, c), 0)
    ci = lax.broadcasted_iota(I32, (c, c), 1)
    incl, strict, eye = ri >= ci, ri > ci, ri == ci
    tril = jnp.where(incl, 1.0, 0.0).astype(BF16)
    g_parts = _split3(g_ref[...])
    gcum = sum(jnp.dot(tril, p, preferred_element_type=F32) for p in g_parts)
    gcum_t = sum(lax.dot_general(p, tril, (((0,), (1,)), ((), ())), preferred_element_type=F32)
                 for p in g_parts)
    beta_all = beta_ref[...]
    egl_ref[0] = jnp.exp(gcum[c - 1:c, :])
    ident = jnp.where(eye, 1.0, 0.0)
    heads = range(hb)
    sls = [slice(hh * LANES, (hh + 1) * LANES) for hh in heads]
    nmats, rhs = [], []
    for hh in heads:
        gc = gcum[:, hh:hh + 1]
        bc = beta_all[:, hh:hh + 1]
        decay = jnp.where(incl, jnp.exp(jnp.where(incl, gc - gcum_t[hh:hh + 1, :], 0.0)), 0.0)
        q, k, v = q_ref[:, sls[hh]], k_ref[:, sls[hh]], v_ref[:, sls[hh]]
        qb, kb = q.astype(BF16), k.astype(BF16)
        kk = lax.dot_general(kb, kb, (((1,), (1,)), ((), ())), preferred_element_type=F32)
        qk = lax.dot_general(qb, kb, (((1,), (1,)), ((), ())), preferred_element_type=F32)
        nmats.append(jnp.where(strict, bc * kk * decay, 0.0))
        rhs.append(jnp.concatenate([v * bc, k * (bc * jnp.exp(gc))], axis=1))
        qkd_ref[0, hh] = (qk * decay).astype(BF16)
        qe_ref[:, sls[hh]] = (q * jnp.exp(gc)).astype(BF16)
        kd_ref[:, sls[hh]] = (k * jnp.exp(gc[c - 1:c, :] - gc)).astype(BF16)
    invs = [ident - n for n in nmats]
    pows = nmats
    for _ in range(int(np.log2(c)) - 1):
        pows = [_dot_f32(p, p) for p in pows]
        invs = [t + _dot_f32(t, p) for t, p in zip(invs, pows)]
    for hh in heads:
        y = _dot_f32(invs[hh], rhs[hh])
        solv_ref[:, sls[hh]] = y[:, :DV_D]
        solk_ref[:, sls[hh]] = y[:, DV_D:].astype(BF16)


def gdn_local(qkv, g, beta, c, hb=H_D):
    n = qkv.shape[0]
    nc = n // c
    dh = H_D * DK_D
    kern = functools.partial(_gdn_local_kernel, c=c, hb=hb)
    w = hb * LANES
    seg = lambda s: pl.BlockSpec((c, w), lambda i, j: (i, s * (H_D // hb) + j))
    col = pl.BlockSpec((c, w), lambda i, j: (i, j))
    meta = pl.BlockSpec((c, LANES), lambda i, j: (i, 0))
    return pl.pallas_call(
        kern,
        grid=(nc, H_D // hb),
        in_specs=[seg(0), seg(1), seg(2), meta, meta],
        out_specs=[col, col, col, col,
                   pl.BlockSpec((1, hb, c, c), lambda i, j: (i, j, 0, 0)),
                   pl.BlockSpec((1, 1, LANES), lambda i, j: (i, 0, 0))],
        out_shape=[jax.ShapeDtypeStruct((n, dh), F32), jax.ShapeDtypeStruct((n, dh), BF16),
                   jax.ShapeDtypeStruct((n, dh), BF16), jax.ShapeDtypeStruct((n, dh), BF16),
                   jax.ShapeDtypeStruct((nc, H_D, c, c), BF16),
                   jax.ShapeDtypeStruct((nc, 1, LANES), F32)],
        compiler_params=_params("arbitrary", "arbitrary"),
        name="gdn_local",
    )(qkv, qkv, qkv, g, beta)


def _gdn_scan_kernel(s0_ref, solv_ref, solk_ref, qe_ref, kd_ref, qkd_ref, egl_ref, gate_ref, nw_ref,
                     od_ref, sout_ref, s_ref):
    ci = pl.program_id(1)

    @pl.when(ci == 0)
    def _():
        s_ref[...] = s0_ref[0]

    egl = egl_ref[0]
    nw = nw_ref[...]
    for h in range(H_D):
        sl = slice(h * LANES, (h + 1) * LANES)
        s_old = s_ref[h]
        sb = s_old.astype(BF16)
        u = solv_ref[:, sl] - jnp.dot(solk_ref[:, sl], sb, preferred_element_type=F32)
        ub = u.astype(BF16)
        o = (jnp.dot(qe_ref[:, sl], sb, preferred_element_type=F32)
             + jnp.dot(qkd_ref[0, h], ub, preferred_element_type=F32))
        s_ref[h] = s_old * egl[:, h:h + 1] + lax.dot_general(
            kd_ref[:, sl], ub, (((0,), (0,)), ((), ())), preferred_element_type=F32)
        o = o * lax.rsqrt(jnp.mean(o * o, axis=1, keepdims=True) + RMS_EPS) * nw
        gt = gate_ref[:, sl]
        od_ref[:, sl] = (o * (gt * (1.0 / (1.0 + jnp.exp(-gt))))).astype(od_ref.dtype)

    @pl.when(ci == pl.num_programs(1) - 1)
    def _():
        sout_ref[0] = s_ref[...]


def gdn_scan(s0, solv, solk, qe, kd, qkd, egl, gate_arr, gate_col, norm_w, c):
    b = s0.shape[0]
    n = solv.shape[0]
    cps = n // b // c
    dh = H_D * DV_D
    col = pl.BlockSpec((c, dh), lambda b_, i: (b_ * cps + i, 0))
    st = pl.BlockSpec((1, H_D, DK_D, DV_D), lambda b_, i: (b_, 0, 0, 0))
    return pl.pallas_call(
        _gdn_scan_kernel,
        grid=(b, cps),
        in_specs=[st, col, col, col, col,
                  pl.BlockSpec((1, H_D, c, c), lambda b_, i: (b_ * cps + i, 0, 0, 0)),
                  pl.BlockSpec((1, 1, LANES), lambda b_, i: (b_ * cps + i, 0, 0)),
                  pl.BlockSpec((c, dh), lambda b_, i: (b_ * cps + i, gate_col // dh)),
                  pl.BlockSpec((1, LANES), lambda b_, i: (0, 0))],
        out_specs=[col, st],
        out_shape=[jax.ShapeDtypeStruct((n, dh), BF16), jax.ShapeDtypeStruct(s0.shape, F32)],
        scratch_shapes=[pltpu.VMEM((H_D, DK_D, DV_D), F32)],
        compiler_params=_params("arbitrary", "arbitrary"),
        name="gdn_scan",
    )(s0, solv, solk, qe, kd, qkd, egl, gate_arr, norm_w.reshape(1, LANES))


def gdn_mixer(proj, n_rows, cols, halo, s0, conv_w, a_log, dt_bias, norm_w, c, tb):
    pad = lambda v: jnp.pad(v.astype(F32), (0, LANES - H_D)).reshape(1, LANES)
    qkv, g, beta = gdn_prep(proj, n_rows, halo, conv_w, pad(a_log), pad(dt_bias),
                            cols['qkv'], cols['a'], cols['b'], tb)
    solv, solk, qe, kd, qkd, egl = gdn_local(qkv, g, beta, c)
    return gdn_scan(s0, solv, solk, qe, kd, qkd, egl, proj, cols['g'], norm_w, c)


ODD_COLS = {}
_orig, _new = 0, 0
_order = dict(qc=0, kc=1, vc=2, qi=3, ki=4, wi=5, qkv=6, a=7, b=8, g=9)
_starts = np.concatenate([[0], np.cumsum(ODD_SIZES)])
for _name in ('qc', 'g', 'kc', 'vc', 'qi', 'qkv', 'ki', 'wi', 'a', 'b'):
    _w = ODD_SIZES[_order[_name]]
    ODD_COLS[_name] = (_new, int(_starts[_order[_name]]), _w)
    _new += -(-_w // LANES) * LANES
ODD_WIDTH = _new


def reorder_w_in_odd(w):
    parts = []
    for name in ('qc', 'g', 'kc', 'vc', 'qi', 'qkv', 'ki', 'wi', 'a', 'b'):
        _, o0, wd = ODD_COLS[name]
        parts.append(jnp.pad(w[:, o0:o0 + wd], ((0, 0), (0, (-wd) % LANES))))
    return jnp.concatenate(parts, axis=1).astype(BF16)


def odd_layer(x, n_p, bs, n, cache_c_k, cache_c_v, cache_c_idx_k, state_d_ssm, state_d_conv,
              w_in_r, w_out, conv_w, a_log, dt_bias, norm_w):
    cols = {k: v[0] for k, v in ODD_COLS.items()}
    past = cache_c_k.shape[1]
    assert n >= CONV_W - 1 and n_p % 256 == 0
    proj = matmul(x.astype(BF16), w_in_r, tn=ODD_WIDTH // 10)
    pos = jnp.concatenate([jnp.arange(n_p), jnp.tile(past + jnp.arange(n), bs)])
    qc_b, kc_f, kc_b, vc_b, qi_hm, ki_f, ki_b, wi_f = rope_split(proj, pos, cols)
    dk = KV_C * HEAD_DIM
    topk_p = min(TOPK_MAX, n_p // 4)
    oc_p = dsa_attention(qc_b[:n_p][None], qi_hm[:, :n_p], wi_f[:n_p][None], kc_b[:n_p][None],
                         vc_b[:n_p][None], ki_b[:n_p, :D_IDX][None], 0, n_p, topk_p, 128, 512)
    bk_s = 384
    s_real = past + n
    s_pad = -(-s_real // bk_s) * bk_s
    cat = lambda c, new: jnp.concatenate(
        [c.reshape(bs, past, -1).astype(BF16), new, jnp.zeros((bs, s_pad - s_real, new.shape[-1]), BF16)], 1)
    oc_s = dsa_attention(qc_b[n_p:].reshape(bs, n, -1), qi_hm[:, n_p:], wi_f[n_p:].reshape(bs, n, LANES),
                         cat(cache_c_k, kc_b[n_p:].reshape(bs, n, dk)),
                         cat(cache_c_v, vc_b[n_p:].reshape(bs, n, dk)),
                         cat(cache_c_idx_k, ki_b[n_p:, :D_IDX].reshape(bs, n, D_IDX)),
                         past, s_real, min(TOPK_MAX, s_real // 4), n, bk_s)
    tb_p = 256
    q0 = cols['qkv']
    halo_p = jnp.zeros((1, HALO, QKV_D), F32)
    s0_p = jnp.zeros((1, H_D, DK_D, DV_D), F32)
    od_p, s_p = gdn_mixer(proj, n_p, cols, halo_p, s0_p, conv_w, a_log, dt_bias, norm_w, CHUNK, tb_p)
    proj_s = proj[n_p:]
    halo_s = jnp.concatenate([jnp.zeros((bs, HALO - (CONV_W - 1), QKV_D), F32), state_d_conv], 1)
    od_s, s_s = gdn_mixer(proj_s, bs * n, cols, halo_s, state_d_ssm, conv_w, a_log, dt_bias, norm_w, n, n)
    dh = H_C * HEAD_DIM
    mp = matmul_concat(oc_p.reshape(n_p, dh), od_p, w_out)
    ms = matmul_concat(oc_s.reshape(bs * n, dh), od_s, w_out)
    v0 = cols['vc']
    qkv_s = proj_s[:, q0:q0 + QKV_D].reshape(bs, n, QKV_D)
    caches = (kc_f[:n_p].reshape(1, n_p, KV_C, HEAD_DIM), proj[:n_p, v0:v0 + dk].reshape(1, n_p, KV_C, HEAD_DIM),
              ki_f[:n_p, :D_IDX][None],
              kc_f[n_p:].reshape(bs, n, KV_C, HEAD_DIM), proj_s[:, v0:v0 + dk].reshape(bs, n, KV_C, HEAD_DIM),
              ki_f[n_p:, :D_IDX].reshape(bs, n, D_IDX),
              s_p, proj[n_p - (CONV_W - 1):n_p, q0:q0 + QKV_D][None],
              s_s, qkv_s[:, n - (CONV_W - 1):])
    return mp, ms, caches


def kernel(x_prompt, x_sample, cache_a_k, cache_a_v, cache_b_k, cache_b_v, cache_c_k, cache_c_v,
           cache_c_idx_k, state_d_ssm, state_d_conv, w_in_even, w_out_even, b_rel_bias, w_in_odd,
           w_out_odd, d_conv_w, d_a_log, d_dt_bias, d_norm_w, w_router, router_bias, moe_w_gate,
           moe_w_up, moe_w_down, ln_g, ln_b):
    bp, t, d = x_prompt.shape
    bs, n, _ = x_sample.shape
    assert bp == 1
    n_p = bp * t
    w_in_even = w_in_even.astype(BF16)
    w_out_even = w_out_even.astype(BF16)
    w_in_odd_r = reorder_w_in_odd(w_in_odd)
    w_out_odd = w_out_odd.astype(BF16)
    wr_pad = jnp.pad(w_router.astype(BF16), ((0, 0), (0, LANES - N_EXPERTS)))
    rb_pad = jnp.pad(router_bias.astype(F32), (0, LANES - N_EXPERTS)).reshape(1, LANES)
    x = jnp.concatenate([x_prompt.reshape(n_p, d), x_sample.reshape(bs * n, d)], 0)
    for layer in range(DEPTH):
        if layer % 2 == 0:
            xp, xs = x[:n_p].reshape(bp, t, d), x[n_p:].reshape(bs, n, d)
            mp, ms, (a_k_p, a_v_p, a_k_s, a_v_s, b_k_p, b_v_p, b_k_s, b_v_s) = even_layer(
                xp, xs, cache_a_k, cache_a_v, cache_b_k, cache_b_v, w_in_even, w_out_even, b_rel_bias)
        else:
            mp, ms, (c_k_p, c_v_p, c_i_p, c_k_s, c_v_s, c_i_s, d_s_p, d_c_p, d_s_s, d_c_s) = odd_layer(
                x, n_p, bs, n, cache_c_k, cache_c_v, cache_c_idx_k, state_d_ssm, state_d_conv,
                w_in_odd_r, w_out_odd, d_conv_w, d_a_log, d_dt_bias, d_norm_w)
        m = jnp.concatenate([mp.reshape(n_p, d), ms.reshape(bs * n, d)], 0)
        x = moe_block(x, m, ln_g[layer], ln_b[layer], wr_pad, rb_pad,
                      moe_w_gate, moe_w_up, moe_w_down, layer)
    xp, xs = x[:n_p].reshape(bp, t, d), x[n_p:].reshape(bs, n, d)
    return (xp, xs, a_k_p, a_v_p, a_k_s, a_v_s, b_k_p, b_v_p, b_k_s, b_v_s,
            c_k_p, c_v_p, c_i_p, c_k_s, c_v_s, c_i_s, d_s_p, d_c_p, d_s_s, d_c_s)


def _old_project(x, w_bf16):
    b, t, d = x.shape
    n = w_bf16.shape[1]
    n_pad = (-n) % 1024
    if n_pad:
        w_bf16 = jnp.pad(w_bf16, ((0, 0), (0, n_pad)))
    y = matmul(x.reshape(b * t, d).astype(BF16), w_bf16)
    return y[:, :n].reshape(b, t, n)


def split_cols(h, sizes):
    cuts, acc = [], 0
    for s in sizes[:-1]:
        acc += s
        cuts.append(acc)
    return jnp.split(h, cuts, axis=-1)


def rope(x, pos):
    half = x.shape[-1] // 2
    inv = ROPE_THETA ** (-jnp.arange(half, dtype=F32) / half)
    ang = pos.astype(F32)[:, None] * inv[None, :]
    cos, sin = jnp.cos(ang)[None, :, None, :], jnp.sin(ang)[None, :, None, :]
    xf = x.astype(F32)
    x1, x2 = xf[..., :half], xf[..., half:]
    return jnp.concatenate([x1 * cos - x2 * sin, x2 * cos + x1 * sin], -1).astype(x.dtype)


def dsa_attend(qc, qi, wi, q_pos, kc, vc, ki, k_pos, topk):
    B, Q = qc.shape[:2]
    logits = jnp.einsum('bqhe,bse->bqhs', qi.astype(F32), ki.astype(F32))
    score = jnp.einsum('bqh,bqhs->bqs', wi.astype(F32), jax.nn.relu(logits))
    adm = jnp.broadcast_to(((k_pos[None, :] // CHUNK) <= (q_pos[:, None] // CHUNK))[None], score.shape)
    _, sel = lax.top_k(jnp.where(adm, score, NEG), topk)
    sel_ok = jnp.take_along_axis(adm, sel, axis=-1)
    gather = jax.vmap(lambda rows, i: rows[i])
    ks, vs = gather(kc, sel), gather(vc, sel)
    qg = qc.reshape(B, Q, KV_C, H_C // KV_C, HEAD_DIM).astype(F32)
    s = jnp.einsum('bqgrd,bqkgd->bqgrk', qg, ks.astype(F32)) * (HEAD_DIM ** -0.5)
    s = jnp.where(sel_ok[:, :, None, None, :], s, NEG)
    p = jax.nn.softmax(s, axis=-1)
    o = jnp.einsum('bqgrk,bqkgd->bqgrd', p, vs.astype(F32))
    return o.reshape(B, Q, H_C, HEAD_DIM).astype(qc.dtype)


def causal_conv(raw, buf, conv_w):
    xp = jnp.concatenate([buf, raw], 1)
    out = lax.conv_general_dilated(xp, conv_w[:, None, :], window_strides=(1,), padding='VALID',
                                   dimension_numbers=('NWC', 'WIO', 'NWC'),
                                   feature_group_count=raw.shape[-1])
    return jax.nn.silu(out), xp[:, xp.shape[1] - (CONV_W - 1):]


def l2norm(t):
    return t * lax.rsqrt(jnp.sum(t * t, -1, keepdims=True) + RMS_EPS)


def gdn_inputs(qkv_raw, buf, a_raw, b_raw, conv_w, a_log, dt_bias):
    qkv, new_buf = causal_conv(qkv_raw, buf, conv_w)
    B, T, _ = qkv.shape
    q, k, v = split_cols(qkv, (H_D * DK_D, H_D * DK_D, H_D * DV_D))
    q = l2norm(q.reshape(B, T, H_D, DK_D).astype(F32)) * (DK_D ** -0.5)
    k = l2norm(k.reshape(B, T, H_D, DK_D).astype(F32))
    v = v.reshape(B, T, H_D, DV_D).astype(F32)
    g = -jnp.exp(a_log.astype(F32)) * jax.nn.softplus(a_raw.astype(F32) + dt_bias.astype(F32))
    beta = jax.nn.sigmoid(b_raw.astype(F32))
    return q, k, v, g, beta, new_buf


def gdn_chunk(S, q, k, v, g, beta):
    C = q.shape[1]
    qh, kh, vh = q.transpose(0, 2, 1, 3), k.transpose(0, 2, 1, 3), v.transpose(0, 2, 1, 3)
    G = jnp.cumsum(g, axis=1).transpose(0, 2, 1)
    bt = beta.transpose(0, 2, 1)
    incl = jnp.tril(jnp.ones((C, C), dtype=bool))
    strict = jnp.tril(jnp.ones((C, C), dtype=bool), -1)
    decay = jnp.exp(jnp.where(incl, G[..., :, None] - G[..., None, :], -jnp.inf))
    kk = jnp.einsum('bhtd,bhjd->bhtj', kh, kh)
    a_mat = jnp.where(strict, bt[..., :, None] * kk * decay, 0.0) + jnp.eye(C, dtype=F32)
    rhs = jnp.concatenate([vh * bt[..., None], kh * (bt * jnp.exp(G))[..., None]], -1)
    sol = lax.linalg.triangular_solve(a_mat, rhs, left_side=True, lower=True, unit_diagonal=True)
    u = sol[..., :DV_D] - jnp.einsum('bhtk,bhkv->bhtv', sol[..., DV_D:], S)
    qk = jnp.einsum('bhtd,bhjd->bhtj', qh, kh) * decay
    o = jnp.einsum('bhtk,bhkv->bhtv', qh * jnp.exp(G)[..., None], S) + jnp.einsum('bhtj,bhjv->bhtv', qk, u)
    g_last = G[..., -1]
    S_new = S * jnp.exp(g_last)[..., None, None] + jnp.einsum(
        'bhjk,bhjv->bhkv', kh * jnp.exp(g_last[..., None] - G)[..., None], u)
    return S_new, o.transpose(0, 2, 1, 3)


def gdn_output(o, gate, norm_w):
    B, T = o.shape[:2]
    o = o * lax.rsqrt(jnp.mean(o * o, -1, keepdims=True) + RMS_EPS) * norm_w.astype(F32)
    return (o.reshape(B, T, H_D * DV_D) * jax.nn.silu(gate.astype(F32))).astype(gate.dtype)


def odd_project(x, w_in, pos):
    B, T, _ = x.shape
    qc, kc, vc, qi, ki, wi, qkv_d, a_d, b_d, g_d = split_cols(project(x, w_in), ODD_SIZES)
    qc = rope(qc.reshape(B, T, H_C, HEAD_DIM), pos)
    kc = rope(kc.reshape(B, T, KV_C, HEAD_DIM), pos)
    vc = vc.reshape(B, T, KV_C, HEAD_DIM)
    qi = rope(qi.reshape(B, T, H_IDX, D_IDX), pos)
    ki = rope(ki[:, :, None, :], pos)[:, :, 0]
    return qc, kc, vc, qi, ki, wi, qkv_d, a_d, b_d, g_d


def odd_prompt(x, w_in, w_out, conv_w, a_log, dt_bias, norm_w):
    B, T, _ = x.shape
    pos = jnp.arange(T)
    qc, kc, vc, qi, ki, wi, qkv_d, a_d, b_d, g_d = odd_project(x, w_in, pos)
    nq = T // Q_BLOCK
    topk = min(TOPK_MAX, T // 4)
    blk = lambda t: t.reshape((B, nq, Q_BLOCK) + t.shape[2:]).swapaxes(0, 1)
    oc = lax.map(lambda a: dsa_attend(a[0], a[1], a[2], a[3], kc, vc, ki, pos, topk),
                 (blk(qc), blk(qi), blk(wi), pos.reshape(nq, Q_BLOCK)))
    oc = oc.swapaxes(0, 1).reshape(B, T, H_C * HEAD_DIM)
    buf0 = jnp.zeros((B, CONV_W - 1, QKV_D), qkv_d.dtype)
    q, k, v, g, beta, conv_buf = gdn_inputs(qkv_d, buf0, a_d, b_d, conv_w, a_log, dt_bias)
    nc = T // CHUNK
    chk = lambda t: t.reshape((B, nc, CHUNK) + t.shape[2:]).swapaxes(0, 1)
    S0 = jnp.zeros((B, H_D, DK_D, DV_D), F32)
    S, od = lax.scan(lambda S, c: gdn_chunk(S, *c), S0, (chk(q), chk(k), chk(v), chk(g), chk(beta)))
    od = gdn_output(od.swapaxes(0, 1).reshape(B, T, H_D, DV_D), g_d, norm_w)
    y = project(jnp.concatenate([oc, od], -1), w_out)
    return y, kc, vc, ki, S.astype(x.dtype), conv_buf


def odd_sample(x, cache_c_k, cache_c_v, cache_c_idx_k, state_d_ssm, state_d_conv,
               w_in, w_out, conv_w, a_log, dt_bias, norm_w):
    B, n, _ = x.shape
    past = cache_c_k.shape[1]
    q_pos = past + jnp.arange(n)
    qc, kc, vc, qi, ki, wi, qkv_d, a_d, b_d, g_d = odd_project(x, w_in, q_pos)
    topk = min(TOPK_MAX, (past + n) // 4)
    oc = dsa_attend(qc, qi, wi, q_pos, jnp.concatenate([cache_c_k, kc], 1),
                    jnp.concatenate([cache_c_v, vc], 1), jnp.concatenate([cache_c_idx_k, ki], 1),
                    jnp.arange(past + n), topk).reshape(B, n, H_C * HEAD_DIM)
    q, k, v, g, beta, conv_buf = gdn_inputs(qkv_d, state_d_conv, a_d, b_d, conv_w, a_log, dt_bias)
    S, od = gdn_chunk(state_d_ssm.astype(F32), q, k, v, g, beta)
    od = gdn_output(od, g_d, norm_w)
    y = project(jnp.concatenate([oc, od], -1), w_out)
    return y, kc, vc, ki, S.astype(state_d_ssm.dtype), conv_buf


def _old_kernel(x_prompt, x_sample, cache_a_k, cache_a_v, cache_b_k, cache_b_v, cache_c_k, cache_c_v,
                cache_c_idx_k, state_d_ssm, state_d_conv, w_in_even, w_out_even, b_rel_bias, w_in_odd,
                w_out_odd, d_conv_w, d_a_log, d_dt_bias, d_norm_w, w_router, router_bias, moe_w_gate,
                moe_w_up, moe_w_down, ln_g, ln_b):
    bp, t, d = x_prompt.shape
    bs, n, _ = x_sample.shape
    n_p = bp * t
    w_in_even = w_in_even.astype(BF16)
    w_out_even = w_out_even.astype(BF16)
    w_in_odd = w_in_odd.astype(BF16)
    w_out_odd = w_out_odd.astype(BF16)
    wr_pad = jnp.pad(w_router.astype(BF16), ((0, 0), (0, LANES - N_EXPERTS)))
    rb_pad = jnp.pad(router_bias.astype(F32), (0, LANES - N_EXPERTS)).reshape(1, LANES)
    x = jnp.concatenate([x_prompt.reshape(n_p, d), x_sample.reshape(bs * n, d)], 0)
    for layer in range(DEPTH):
        xp, xs = x[:n_p].reshape(bp, t, d), x[n_p:].reshape(bs, n, d)
        if layer % 2 == 0:
            mp, ms, (a_k_p, a_v_p, a_k_s, a_v_s, b_k_p, b_v_p, b_k_s, b_v_s) = even_layer(
                xp, xs, cache_a_k, cache_a_v, cache_b_k, cache_b_v, w_in_even, w_out_even, b_rel_bias)
        else:
            mp, c_k_p, c_v_p, c_i_p, d_s_p, d_c_p = odd_prompt(xp, w_in_odd, w_out_odd, d_conv_w,
                                                                d_a_log, d_dt_bias, d_norm_w)
            ms, c_k_s, c_v_s, c_i_s, d_s_s, d_c_s = odd_sample(xs, cache_c_k, cache_c_v, cache_c_idx_k,
                                                                state_d_ssm, state_d_conv, w_in_odd,
                                                                w_out_odd, d_conv_w, d_a_log, d_dt_bias,
                                                                d_norm_w)
        m = jnp.concatenate([mp.reshape(n_p, d), ms.reshape(bs * n, d)], 0)
        x = moe_block(x, m, ln_g[layer], ln_b[layer], wr_pad, rb_pad,
                      moe_w_gate, moe_w_up, moe_w_down, layer)
    xp, xs = x[:n_p].reshape(bp, t, d), x[n_p:].reshape(bs, n, d)
    return (xp, xs, a_k_p, a_v_p, a_k_s, a_v_s, b_k_p, b_v_p, b_k_s, b_v_s,
            c_k_p, c_v_p, c_i_p, c_k_s, c_v_s, c_i_s, d_s_p, d_c_p, d_s_s, d_c_s)
```

```python
import functools

import jax
import jax.numpy as jnp
import numpy as np
from jax import lax
from jax.experimental import pallas as pl
from jax.experimental.pallas import tpu as pltpu

D_MODEL = 4096
DEPTH = 2
CHUNK = 64
HEAD_DIM = 128
Q_BLOCK = 128
NEG = -1e30
H_A = 16
H_B = 16
N_LEFT_CHUNKS = 8
BAND_REACH = N_LEFT_CHUNKS * CHUNK
REL_CLIP = 128
EVEN_SIZES = (H_A * HEAD_DIM,) * 3 + (H_B * HEAD_DIM,) * 3
H_C = 16
KV_C = 4
H_IDX = 16
D_IDX = 64
TOPK_MAX = 256
H_D = 16
DK_D = 128
DV_D = 128
CONV_W = 4
QKV_D = H_D * (2 * DK_D + DV_D)
ODD_SIZES = (H_C * HEAD_DIM, KV_C * HEAD_DIM, KV_C * HEAD_DIM, H_IDX * D_IDX, D_IDX, H_IDX,
             QKV_D, H_D, H_D, H_D * DV_D)
N_EXPERTS = 16
N_GROUPS = 4
EXPERTS_PER_GROUP = N_EXPERTS // N_GROUPS
TOP_K = 2
D_EXPERT = 1024
ROPE_THETA = 10000.0
LN_EPS = 1e-5
RMS_EPS = 1e-6
DEEPNORM_ALPHA = (2 * DEPTH) ** 0.25

F32 = jnp.float32
BF16 = jnp.bfloat16
I32 = jnp.int32

F32_EXP_UNDERFLOW = -105.0
LANES = 128
VMEM_LIMIT_BYTES = 56 * 1024 * 1024


def _params(*sem):
    return pltpu.CompilerParams(dimension_semantics=sem, vmem_limit_bytes=VMEM_LIMIT_BYTES)


def _cast_rows(src_ref, dst_ref, dst_off=0, chunk=256):
    rows = src_ref.shape[0]
    assert rows % chunk == 0 and dst_off % chunk == 0

    def body(c, carry):
        r = pl.multiple_of(c * chunk, chunk)
        dst_ref[pl.ds(dst_off + r, chunk), :] = src_ref[pl.ds(r, chunk), :].astype(dst_ref.dtype)
        return carry
    lax.fori_loop(0, rows // chunk, body, 0)


def _matmul_kernel(x_ref, w_ref, o_ref):
    o_ref[...] = jnp.dot(x_ref[...], w_ref[...], preferred_element_type=F32)


def matmul(x, w, tm=512, tn=1024):
    m, k = x.shape
    n = w.shape[1]
    tm = min(tm, m)
    tn = min(tn, n)
    assert m % tm == 0 and n % tn == 0
    return pl.pallas_call(
        _matmul_kernel,
        grid=(n // tn, m // tm),
        in_specs=[pl.BlockSpec((tm, k), lambda j, i: (i, 0)),
                  pl.BlockSpec((k, tn), lambda j, i: (0, j))],
        out_specs=pl.BlockSpec((tm, tn), lambda j, i: (i, j)),
        out_shape=jax.ShapeDtypeStruct((m, n), F32),
        compiler_params=_params("arbitrary", "arbitrary"),
        name="proj_matmul",
    )(x, w)


def _matmul2_kernel(xa_ref, xb_ref, w_ref, o_ref):
    ka = xa_ref.shape[1]
    o_ref[...] = (jnp.dot(xa_ref[...], w_ref[:ka, :], preferred_element_type=F32)
                  + jnp.dot(xb_ref[...], w_ref[ka:, :], preferred_element_type=F32))


def matmul_concat(xa, xb, w, tm=512, tn=1024):
    m, ka = xa.shape
    kb = xb.shape[1]
    n = w.shape[1]
    tm = min(tm, m)
    assert m % tm == 0 and n % tn == 0
    return pl.pallas_call(
        _matmul2_kernel,
        grid=(n // tn, m // tm),
        in_specs=[pl.BlockSpec((tm, ka), lambda j, i: (i, 0)),
                  pl.BlockSpec((tm, kb), lambda j, i: (i, 0)),
                  pl.BlockSpec((ka + kb, tn), lambda j, i: (0, j))],
        out_specs=pl.BlockSpec((tm, tn), lambda j, i: (i, j)),
        out_shape=jax.ShapeDtypeStruct((m, n), F32),
        compiler_params=_params("arbitrary", "arbitrary"),
        name="out_matmul",
    )(xa, xb, w)


def _stage_rows(piece_refs, dst_ref):
    off = 0
    for ref in piece_refs:
        rows = ref.shape[1]
        if rows % 256 == 0 and off % 256 == 0:
            _cast_rows(ref.at[0], dst_ref, dst_off=off)
        else:
            dst_ref[off:off + rows, :] = ref[0].astype(dst_ref.dtype)
        off += rows
    if off < dst_ref.shape[0]:
        dst_ref[off:, :] = jnp.zeros((dst_ref.shape[0] - off, dst_ref.shape[1]), dst_ref.dtype)


def _sb_kernel(q_ref, *refs, n_pieces, bq, bk, q_off, tk, scale):
    k_refs, v_refs = refs[:n_pieces], refs[n_pieces:2 * n_pieces]
    u_ref, o_ref, kb_ref, vb_ref = refs[2 * n_pieces:]
    qi = pl.program_id(2)

    @pl.when(qi == 0)
    def _():
        _stage_rows(k_refs, kb_ref)
        _stage_rows(v_refs, vb_ref)

    q = q_ref[0].astype(BF16)
    r0 = qi * bq
    rows = q_off + r0 + lax.broadcasted_iota(I32, (bq, bk), 0)
    cols = lax.broadcasted_iota(I32, (bq, bk), 1)
    n_kb = jnp.minimum((q_off + r0 + bq + bk - 1) // bk, tk // bk)
    tri = u_ref[...]

    def live(carry):
        it, _, _, max_run = carry
        return (it < n_kb) & (max_run > F32_EXP_UNDERFLOW)

    def body(carry):
        it, acc, run, _ = carry
        start = pl.multiple_of((n_kb - 1 - it) * bk, bk)
        kblk = kb_ref[pl.ds(start, bk), :]
        vblk = vb_ref[pl.ds(start, bk), :]
        z = lax.dot_general(q, kblk, (((1,), (1,)), ((), ())), preferred_element_type=F32) * scale
        softplus = jnp.maximum(z, 0.0) + jnp.log(1.0 + jnp.exp(-jnp.abs(z)))
        earlier = (cols + start) < rows
        log_1mb = jnp.where(earlier, -softplus, 0.0)
        log_b = z - softplus
        hi = log_1mb.astype(BF16)
        lo = (log_1mb - hi.astype(F32)).astype(BF16)
        between = (jnp.dot(hi, tri, preferred_element_type=F32)
                   + jnp.dot(lo, tri, preferred_element_type=F32) + run)
        w = jnp.where(earlier, jnp.exp(log_b + between), 0.0)
        acc = acc + jnp.dot(w.astype(BF16), vblk, preferred_element_type=F32)
        run = run + jnp.sum(log_1mb, axis=1, keepdims=True)
        return it + 1, acc, run, jnp.max(run)

    _, acc, _, _ = lax.while_loop(
        live, body, (jnp.int32(0), jnp.zeros((bq, HEAD_DIM), F32), jnp.zeros((bq, 1), F32), jnp.float32(0.0)))
    o_ref[0] = acc.astype(o_ref.dtype)


def _piece_specs(pieces, which):
    specs = []
    for piece in pieces:
        arr, cb = piece[which], piece[2 + which]
        specs.append(pl.BlockSpec((1, arr.shape[1], HEAD_DIM), lambda b_, h, *_, cb=cb: (b_, 0, cb + h)))
    return specs


def stick_breaking_attention(q_arr, q_cb, kv_pieces, n_heads, q_off, bq, bk):
    b, tq, _ = q_arr.shape
    tk = -(-sum(p[0].shape[1] for p in kv_pieces) // bk) * bk
    assert tq % bq == 0
    tri = jnp.asarray(np.tril(np.ones((bk, bk), np.float32), -1), BF16)
    kern = functools.partial(_sb_kernel, n_pieces=len(kv_pieces), bq=bq, bk=bk, q_off=q_off, tk=tk,
                             scale=HEAD_DIM ** -0.5)
    return pl.pallas_call(
        kern,
        grid=(b, n_heads, tq // bq),
        in_specs=[pl.BlockSpec((1, bq, HEAD_DIM), lambda b_, h, i: (b_, i, q_cb + h))]
        + _piece_specs(kv_pieces, 0) + _piece_specs(kv_pieces, 1)
        + [pl.BlockSpec((bk, bk), lambda b_, h, i: (0, 0))],
        out_specs=pl.BlockSpec((1, bq, HEAD_DIM), lambda b_, h, i: (b_, i, h)),
        out_shape=jax.ShapeDtypeStruct((b, tq, n_heads * HEAD_DIM), BF16),
        scratch_shapes=[pltpu.VMEM((tk, HEAD_DIM), BF16), pltpu.VMEM((tk, HEAD_DIM), BF16)],
        compiler_params=_params("arbitrary", "arbitrary", "arbitrary"),
        name="stick_breaking",
    )(q_arr, *[p[0] for p in kv_pieces], *[p[1] for p in kv_pieces], tri)


def _softmax_pv(s, v):
    m = jnp.max(s, axis=1, keepdims=True)
    e = jnp.exp(s - m)
    p = e / jnp.sum(e, axis=1, keepdims=True)
    return jnp.dot(p.astype(BF16), v, preferred_element_type=F32)


def _band_prompt_kernel(q_ref, k_ref, v_ref, bias_ref, o_ref, kb_ref, vb_ref, *, bq, reach, scale):
    qi = pl.program_id(2)

    @pl.when(qi == 0)
    def _():
        kb_ref[:reach, :] = jnp.zeros((reach, HEAD_DIM), BF16)
        vb_ref[:reach, :] = jnp.zeros((reach, HEAD_DIM), BF16)
        _cast_rows(k_ref.at[0], kb_ref, dst_off=reach)
        _cast_rows(v_ref.at[0], vb_ref, dst_off=reach)

    win = reach + bq
    start = pl.multiple_of(qi * bq, bq)
    q = q_ref[0].astype(BF16)
    kw = kb_ref[pl.ds(start, win), :]
    vw = vb_ref[pl.ds(start, win), :]
    s = lax.dot_general(q, kw, (((1,), (1,)), ((), ())), preferred_element_type=F32) * scale
    k_pos = start - reach + lax.broadcasted_iota(I32, (bq, win), 1)
    s = jnp.where(k_pos >= 0, s + bias_ref[0], NEG)
    o_ref[0] = _softmax_pv(s, vw).astype(o_ref.dtype)


def band_attention_prompt(proj, q_cb, k_cb, v_cb, rel_bias, bq=512):
    b, t, _ = proj.shape
    assert bq % CHUNK == 0 and t % bq == 0
    win = BAND_REACH + bq
    i = np.arange(bq)[:, None]
    j = np.arange(win)[None, :]
    qc, kc = i // CHUNK, j // CHUNK - N_LEFT_CHUNKS
    in_band = (kc <= qc) & (kc >= qc - N_LEFT_CHUNKS)
    period = win + bq
    k = np.arange(period)
    i_minus_j = np.where(k < win, -k, period - k)
    rel = np.clip(BAND_REACH + i_minus_j, -REL_CLIP, REL_CLIP) + REL_CLIP
    vec = rel_bias.astype(F32)[:, rel]
    toep = jnp.tile(vec, (1, bq))[:, :bq * (period - 1)].reshape(-1, bq, period - 1)[:, :, :win]
    bias = jnp.where(jnp.asarray(in_band)[None], toep, NEG)
    kern = functools.partial(_band_prompt_kernel, bq=bq, reach=BAND_REACH, scale=HEAD_DIM ** -0.5)
    return pl.pallas_call(
        kern,
        grid=(b, H_B, t // bq),
        in_specs=[pl.BlockSpec((1, bq, HEAD_DIM), lambda b_, h, i_: (b_, i_, q_cb + h)),
                  pl.BlockSpec((1, t, HEAD_DIM), lambda b_, h, i_: (b_, 0, k_cb + h)),
                  pl.BlockSpec((1, t, HEAD_DIM), lambda b_, h, i_: (b_, 0, v_cb + h)),
                  pl.BlockSpec((1, bq, win), lambda b_, h, i_: (h, 0, 0))],
        out_specs=pl.BlockSpec((1, bq, HEAD_DIM), lambda b_, h, i_: (b_, i_, h)),
        out_shape=jax.ShapeDtypeStruct((b, t, H_B * HEAD_DIM), BF16),
        scratch_shapes=[pltpu.VMEM((BAND_REACH + t, HEAD_DIM), BF16),
                        pltpu.VMEM((BAND_REACH + t, HEAD_DIM), BF16)],
        compiler_params=_params("arbitrary", "arbitrary", "arbitrary"),
        name="band_prompt",
    )(proj, proj, proj, bias)


def _band_core_kernel(q_ref, *refs, n_pieces, scale):
    k_refs, v_refs = refs[:n_pieces], refs[n_pieces:2 * n_pieces]
    bias_ref, o_ref, kb_ref, vb_ref = refs[2 * n_pieces:]
    _stage_rows(k_refs, kb_ref)
    _stage_rows(v_refs, vb_ref)
    q = q_ref[0].astype(BF16)
    s = lax.dot_general(q, kb_ref[...], (((1,), (1,)), ((), ())), preferred_element_type=F32) * scale
    o_ref[0] = _softmax_pv(s + bias_ref[0], vb_ref[...]).astype(o_ref.dtype)


def band_attention_core(q_arr, q_cb, kv_pieces, bias):
    b, cq, _ = q_arr.shape
    h_n, _, ck = bias.shape
    kern = functools.partial(_band_core_kernel, n_pieces=len(kv_pieces), scale=HEAD_DIM ** -0.5)
    return pl.pallas_call(
        kern,
        grid=(b, h_n),
        in_specs=[pl.BlockSpec((1, cq, HEAD_DIM), lambda b_, h: (b_, 0, q_cb + h))]
        + _piece_specs(kv_pieces, 0) + _piece_specs(kv_pieces, 1)
        + [pl.BlockSpec((1, cq, ck), lambda b_, h: (h, 0, 0))],
        out_specs=pl.BlockSpec((1, cq, HEAD_DIM), lambda b_, h: (b_, 0, h)),
        out_shape=jax.ShapeDtypeStruct((b, cq, h_n * HEAD_DIM), BF16),
        scratch_shapes=[pltpu.VMEM((ck, HEAD_DIM), BF16), pltpu.VMEM((ck, HEAD_DIM), BF16)],
        compiler_params=_params("arbitrary", "arbitrary"),
        name="band_core",
    )(q_arr, *[p[0] for p in kv_pieces], *[p[1] for p in kv_pieces], bias)


def _layer_norm_rows(h, g, b):
    mu = jnp.mean(h, axis=-1, keepdims=True)
    d = h - mu
    var = jnp.mean(d * d, axis=-1, keepdims=True)
    return d * lax.rsqrt(var + LN_EPS) * g + b


def _ln_router_kernel(x_ref, m_ref, g_ref, b_ref, wr_ref, rb_ref, x1_ref, gate_ref, idx_ref):
    y = _layer_norm_rows(DEEPNORM_ALPHA * x_ref[...] + m_ref[...], g_ref[...], b_ref[...])
    x1_ref[...] = y
    tb = y.shape[0]
    logits = jnp.dot(y.astype(BF16), wr_ref[...], preferred_element_type=F32)
    lane = lax.broadcasted_iota(I32, (tb, LANES), 1)
    lane_f = lane.astype(F32)
    real = lane < N_EXPERTS
    lg = jnp.where(real, logits, -jnp.inf)
    e = jnp.exp(lg - jnp.max(lg, axis=1, keepdims=True))
    aff = e / jnp.sum(e, axis=1, keepdims=True)
    sel = jnp.where(real, aff + rb_ref[...], -jnp.inf)
    group = lane // EXPERTS_PER_GROUP
    best = jnp.zeros((tb, 1), I32)
    best_v = jnp.max(jnp.where(group == 0, sel, -jnp.inf), axis=1, keepdims=True)
    for gi in range(1, N_GROUPS):
        gv = jnp.max(jnp.where(group == gi, sel, -jnp.inf), axis=1, keepdims=True)
        upd = gv > best_v
        best = jnp.where(upd, gi, best)
        best_v = jnp.where(upd, gv, best_v)
    cand = jnp.where((group == best) & real, sel, NEG)
    m1 = jnp.max(cand, axis=1, keepdims=True)
    i1 = jnp.min(jnp.where(cand == m1, lane_f, float(LANES)), axis=1, keepdims=True)
    cand2 = jnp.where(lane_f == i1, -jnp.inf, cand)
    m2 = jnp.max(cand2, axis=1, keepdims=True)
    i2 = jnp.min(jnp.where(cand2 == m2, lane_f, float(LANES)), axis=1, keepdims=True)
    g1 = jnp.sum(jnp.where(lane_f == i1, aff, 0.0), axis=1, keepdims=True)
    g2 = jnp.sum(jnp.where(lane_f == i2, aff, 0.0), axis=1, keepdims=True)
    tot = g1 + g2
    gate_ref[...] = jnp.where(lane == 0, g1 / tot, jnp.where(lane == 1, g2 / tot, 0.0))
    idx_ref[...] = jnp.where(lane == 0, i1, jnp.where(lane == 1, i2, 0.0)).astype(I32)


def ln_router(x, m, g, b, w_router_pad, router_bias_pad, tb=256):
    n, d = x.shape
    assert n % tb == 0
    row = pl.BlockSpec((tb, d), lambda i: (i, 0))
    vec = pl.BlockSpec((1, d), lambda i: (0, 0))
    meta = pl.BlockSpec((tb, LANES), lambda i: (i, 0))
    return pl.pallas_call(
        _ln_router_kernel,
        grid=(n // tb,),
        in_specs=[row, row, vec, vec,
                  pl.BlockSpec((d, LANES), lambda i: (0, 0)),
                  pl.BlockSpec((1, LANES), lambda i: (0, 0))],
        out_specs=[row, meta, meta],
        out_shape=[jax.ShapeDtypeStruct((n, d), F32),
                   jax.ShapeDtypeStruct((n, LANES), F32),
                   jax.ShapeDtypeStruct((n, LANES), I32)],
        compiler_params=_params("arbitrary"),
        name="ln_router",
    )(x, m, g.reshape(1, d), b.reshape(1, d), w_router_pad, router_bias_pad)


def _wait_rows(src_hbm, dst, sem, n):
    def body(r, c):
        pltpu.make_async_copy(src_hbm.at[pl.ds(0, 1), :], dst.at[pl.ds(0, 1), :], sem).wait()
        return c
    lax.fori_loop(0, n, body, 0)


def _gather_cast_kernel(tok_ref, na_ref, x_hbm, o_ref, buf, sem):
    tm = buf.shape[0]
    i = pl.program_id(0)
    base = i * tm

    @pl.when(i < na_ref[0])
    def _():
        def issue(r, c):
            pltpu.make_async_copy(x_hbm.at[pl.ds(tok_ref[base + r], 1), :], buf.at[pl.ds(r, 1), :], sem).start()
            return c
        lax.fori_loop(0, tm, issue, 0, unroll=8)
        _wait_rows(x_hbm, buf, sem, tm)
        _cast_rows(buf, o_ref)

    @pl.when(i >= na_ref[0])
    def _():
        o_ref[...] = jnp.zeros(o_ref.shape, o_ref.dtype)


def gather_rows_bf16(x, row_token, n_active, tm):
    n, d = x.shape
    r = row_token.shape[0]
    return pl.pallas_call(
        _gather_cast_kernel,
        grid_spec=pltpu.PrefetchScalarGridSpec(
            num_scalar_prefetch=2,
            grid=(r // tm,),
            in_specs=[pl.BlockSpec(memory_space=pl.ANY)],
            out_specs=pl.BlockSpec((tm, d), lambda i, tok, na: (i, 0)),
            scratch_shapes=[pltpu.VMEM((tm, d), F32), pltpu.SemaphoreType.DMA(())]),
        out_shape=jax.ShapeDtypeStruct((r, d), BF16),
        compiler_params=_params("arbitrary"),
        name="moe_gather",
    )(row_token, n_active, x)


def _expert_changed(te_ref, i):
    return (i == 0) | (te_ref[i] != te_ref[jnp.maximum(i - 1, 0)])


def _moe_up_kernel(te_ref, na_ref, xs_ref, wg_ref, wu_ref, h_ref, wgb, wub):
    i = pl.program_id(1)

    @pl.when(_expert_changed(te_ref, i))
    def _():
        _cast_rows(wg_ref, wgb)
        _cast_rows(wu_ref, wub)

    @pl.when(i < na_ref[0])
    def _():
        x = xs_ref[...]
        a = jnp.dot(x, wgb[...], preferred_element_type=F32)
        u = jnp.dot(x, wub[...], preferred_element_type=F32)
        h_ref[...] = (a * (1.0 / (1.0 + jnp.exp(-a))) * u).astype(h_ref.dtype)

    @pl.when(i >= na_ref[0])
    def _():
        h_ref[...] = jnp.zeros(h_ref.shape, h_ref.dtype)


def _moe_down_kernel(te_ref, na_ref, h_ref, wd_ref, gate_ref, y_ref, wdb):
    i = pl.program_id(1)

    @pl.when(_expert_changed(te_ref, i))
    def _():
        _cast_rows(wd_ref, wdb)

    @pl.when(i < na_ref[0])
    def _():
        y_ref[...] = gate_ref[...] * jnp.dot(h_ref[...], wdb[...], preferred_element_type=F32)

    @pl.when(i >= na_ref[0])
    def _():
        y_ref[...] = jnp.zeros(y_ref.shape, y_ref.dtype)


def moe_grouped(xs, gate_sorted, tile_expert, n_active, w_gate, w_up, w_down, layer, tm, tf=512, tn=1024):
    r, d = xs.shape
    n_tiles = r // tm
    de = w_gate.shape[3]
    tf, tn = min(tf, de), min(tn, d)
    h = pl.pallas_call(
        _moe_up_kernel,
        grid_spec=pltpu.PrefetchScalarGridSpec(
            num_scalar_prefetch=2,
            grid=(de // tf, n_tiles),
            in_specs=[pl.BlockSpec((tm, d), lambda f, i, te, na: (i, 0)),
                      pl.BlockSpec((None, None, d, tf), lambda f, i, te, na: (layer, te[i], 0, f)),
                      pl.BlockSpec((None, None, d, tf), lambda f, i, te, na: (layer, te[i], 0, f))],
            out_specs=pl.BlockSpec((tm, tf), lambda f, i, te, na: (i, f)),
            scratch_shapes=[pltpu.VMEM((d, tf), BF16), pltpu.VMEM((d, tf), BF16)]),
        out_shape=jax.ShapeDtypeStruct((r, de), BF16),
        compiler_params=_params("arbitrary", "arbitrary"),
        name="moe_up",
    )(tile_expert, n_active, xs, w_gate, w_up)
    return pl.pallas_call(
        _moe_down_kernel,
        grid_spec=pltpu.PrefetchScalarGridSpec(
            num_scalar_prefetch=2,
            grid=(d // tn, n_tiles),
            in_specs=[pl.BlockSpec((tm, de), lambda n, i, te, na: (i, 0)),
                      pl.BlockSpec((None, None, de, tn), lambda n, i, te, na: (layer, te[i], 0, n)),
                      pl.BlockSpec((tm, 1), lambda n, i, te, na: (i, 0))],
            out_specs=pl.BlockSpec((tm, tn), lambda n, i, te, na: (i, n)),
            scratch_shapes=[pltpu.VMEM((de, tn), BF16)]),
        out_shape=jax.ShapeDtypeStruct((r, d), F32),
        compiler_params=_params("arbitrary", "arbitrary"),
        name="moe_down",
    )(tile_expert, n_active, h, w_down, gate_sorted)


def _combine_ln_kernel(pos_ref, y_hbm, x_ref, g_ref, b_ref, xo_ref, buf, sem):
    tb = x_ref.shape[0]
    base = pl.program_id(0) * (2 * tb)

    def issue(r, c):
        pltpu.make_async_copy(y_hbm.at[pl.ds(pos_ref[base + r], 1), :], buf.at[pl.ds(r, 1), :], sem).start()
        return c
    lax.fori_loop(0, 2 * tb, issue, 0)
    _wait_rows(y_hbm, buf, sem, 2 * tb)
    f = buf[:tb, :] + buf[tb:, :]
    xo_ref[...] = _layer_norm_rows(DEEPNORM_ALPHA * x_ref[...] + f, g_ref[...], b_ref[...])


def combine_ln(y_sorted, pos_blocks, x1, g, b, tb):
    n, d = x1.shape
    row = pl.BlockSpec((tb, d), lambda i, pos: (i, 0))
    vec = pl.BlockSpec((1, d), lambda i, pos: (0, 0))
    return pl.pallas_call(
        _combine_ln_kernel,
        grid_spec=pltpu.PrefetchScalarGridSpec(
            num_scalar_prefetch=1,
            grid=(n // tb,),
            in_specs=[pl.BlockSpec(memory_space=pl.ANY), row, vec, vec],
            out_specs=row,
            scratch_shapes=[pltpu.VMEM((2 * tb, d), F32), pltpu.SemaphoreType.DMA(())]),
        out_shape=jax.ShapeDtypeStruct((n, d), F32),
        compiler_params=_params("arbitrary"),
        name="moe_combine_ln",
    )(pos_blocks, y_sorted, x1, g.reshape(1, d), b.reshape(1, d))


def moe_block(x, m, ln_g, ln_b, w_router_pad, router_bias_pad, w_gate, w_up, w_down, layer,
              tm=512, tb=128):
    n, d = x.shape
    x1, gate, idx = ln_router(x, m, ln_g[0], ln_b[0], w_router_pad, router_bias_pad)
    e_flat = idx[:, :TOP_K].reshape(-1)
    g_flat = gate[:, :TOP_K].reshape(-1)
    onehot = (e_flat[:, None] == jnp.arange(N_EXPERTS, dtype=I32)[None, :]).astype(I32)
    csum = jnp.cumsum(onehot, axis=0)
    rank = jnp.take_along_axis(csum, e_flat[:, None], axis=1)[:, 0] - 1
    counts = csum[-1]
    padded = (counts + tm - 1) // tm * tm
    ends = jnp.cumsum(padded)
    pos = (ends - padded)[e_flat] + rank
    n_pairs = TOP_K * n
    n_tiles = (n_pairs + N_EXPERTS * (tm - 1)) // tm + 1
    r = n_tiles * tm
    row_token = jnp.zeros((r,), I32).at[pos].set(jnp.arange(n_pairs, dtype=I32) // TOP_K)
    gate_sorted = jnp.zeros((r,), F32).at[pos].set(g_flat).reshape(r, 1)
    n_active = (ends[-1] // tm).astype(I32)
    tile = jnp.arange(n_tiles, dtype=I32)
    tile_expert = jnp.searchsorted(ends, jnp.minimum(tile, n_active - 1) * tm, side='right').astype(I32)
    tile_expert = jnp.minimum(tile_expert, N_EXPERTS - 1)
    n_active = n_active.reshape(1)
    xs = gather_rows_bf16(x1, row_token, n_active, tm)
    y_sorted = moe_grouped(xs, gate_sorted, tile_expert, n_active, w_gate, w_up, w_down, layer, tm)
    pos_blocks = pos.reshape(n // tb, tb, TOP_K).transpose(0, 2, 1).reshape(-1)
    return combine_ln(y_sorted, pos_blocks, x1, ln_g[1], ln_b[1], tb)


def even_layer(xp, xs, cache_a_k, cache_a_v, cache_b_k, cache_b_v, w_in, w_out, rel_bias):
    bp, t, d = xp.shape
    bs, n, _ = xs.shape
    past, lb = cache_a_k.shape[1], cache_b_k.shape[1]
    hd = H_A * HEAD_DIM
    proj_p = matmul(xp.reshape(bp * t, d).astype(BF16), w_in).reshape(bp, t, -1)
    proj_s = matmul(xs.reshape(bs * n, d).astype(BF16), w_in).reshape(bs, n, -1)
    nb = hd // HEAD_DIM
    qa_cb, ka_cb, va_cb, qb_cb, kb_cb, vb_cb = (s * nb for s in range(6))
    seg = lambda p, s: p[..., s * hd:(s + 1) * hd]
    oa_p = stick_breaking_attention(proj_p, qa_cb, [(proj_p, proj_p, ka_cb, va_cb)], H_A, 0, 256, 256)
    ob_p = band_attention_prompt(proj_p, qb_cb, kb_cb, vb_cb, rel_bias)
    mp = matmul_concat(oa_p.reshape(bp * t, hd), ob_p.reshape(bp * t, hd), w_out)
    flat = lambda c: c.reshape(bs, c.shape[1], hd)
    ka_s, va_s, kb_s, vb_s = seg(proj_s, 1), seg(proj_s, 2), seg(proj_s, 4), seg(proj_s, 5)
    oa_s = stick_breaking_attention(
        proj_s, qa_cb, [(flat(cache_a_k), flat(cache_a_v), 0, 0), (proj_s, proj_s, ka_cb, va_cb)],
        H_A, past, n, 256)
    ck = -(-(lb + n) // LANES) * LANES
    q_pos = past + np.arange(n)
    k_pos = np.concatenate([np.arange(past - lb, past), q_pos, np.full((ck - lb - n,), -1)])
    qc, kc = q_pos[:, None] // CHUNK, k_pos[None, :] // CHUNK
    valid = (kc <= qc) & (kc >= qc - N_LEFT_CHUNKS) & (k_pos[None, :] >= 0)
    rel = np.clip(q_pos[:, None] - k_pos[None, :], -REL_CLIP, REL_CLIP) + REL_CLIP
    bias_s = jnp.where(jnp.asarray(valid)[None], rel_bias.astype(F32)[:, rel], NEG)
    ob_s = band_attention_core(
        proj_s, qb_cb, [(flat(cache_b_k), flat(cache_b_v), 0, 0), (proj_s, proj_s, kb_cb, vb_cb)], bias_s)
    ms = matmul_concat(oa_s.reshape(bs * n, hd), ob_s.reshape(bs * n, hd), w_out)
    keep = min(BAND_REACH, t)
    heads = lambda a: a.reshape(a.shape[0], a.shape[1], H_A, HEAD_DIM)
    caches = (heads(seg(proj_p, 1)), heads(seg(proj_p, 2)), heads(ka_s), heads(va_s),
              heads(seg(proj_p, 4)[:, t - keep:]), heads(seg(proj_p, 5)[:, t - keep:]),
              heads(kb_s), heads(vb_s))
    return mp, ms, caches


INT_MIN = np.int32(-2 ** 31)


def _rope_kernel(qc_ref, kc_ref, vc_ref, qi_ref, ki_ref, wi_ref, c128_ref, s128_ref, c64_ref, s64_ref,
                 qc_o, kc_o, kcb_o, vcb_o, qi_o, ki_o, kib_o, wi_o):
    c128, s128, c64, s64 = c128_ref[...], s128_ref[...], c64_ref[...], s64_ref[...]
    lane = lax.broadcasted_iota(I32, c64.shape, 1)
    first_half = (lane % D_IDX) < (D_IDX // 2)

    def rot128(x):
        return x * c128 + pltpu.roll(x, HEAD_DIM // 2, 1) * s128

    def rot64(x):
        partner = jnp.where(first_half, pltpu.roll(x, LANES - D_IDX // 2, 1), pltpu.roll(x, D_IDX // 2, 1))
        return x * c64 + partner * s64

    for h in range(H_C):
        sl = slice(h * HEAD_DIM, (h + 1) * HEAD_DIM)
        qc_o[:, sl] = rot128(qc_ref[:, sl]).astype(BF16)
    for h in range(KV_C):
        sl = slice(h * HEAD_DIM, (h + 1) * HEAD_DIM)
        r = rot128(kc_ref[:, sl])
        kc_o[:, sl] = r
        kcb_o[:, sl] = r.astype(BF16)
    vcb_o[...] = vc_ref[...].astype(BF16)
    for j in range(H_IDX * D_IDX // LANES):
        r = rot64(qi_ref[:, j * LANES:(j + 1) * LANES]).astype(BF16)
        qi_o[2 * j] = r[:, :D_IDX]
        qi_o[2 * j + 1] = r[:, D_IDX:]
    r = rot64(ki_ref[...])
    ki_o[...] = r
    kib_o[...] = r.astype(BF16)
    wi_o[...] = wi_ref[...]


def rope_tables(pos):
    def tab(half, reps):
        inv = ROPE_THETA ** (-jnp.arange(half, dtype=F32) / half)
        ang = pos.astype(F32)[:, None] * inv[None, :]
        c, s = jnp.cos(ang), jnp.sin(ang)
        return jnp.tile(jnp.concatenate([c, c], 1), (1, reps)), jnp.tile(jnp.concatenate([-s, s], 1), (1, reps))
    c128, s128 = tab(HEAD_DIM // 2, 1)
    c64, s64 = tab(D_IDX // 2, 2)
    return c128, s128, c64, s64


def rope_split(proj, pos, cols, tb=256):
    n = proj.shape[0]
    assert n % tb == 0
    tabs = rope_tables(pos)
    dq, dk, di = H_C * HEAD_DIM, KV_C * HEAD_DIM, H_IDX * D_IDX
    spec = lambda w, c0: pl.BlockSpec((tb, w), lambda i: (i, c0 // w))
    row = lambda w: pl.BlockSpec((tb, w), lambda i: (i, 0))
    return pl.pallas_call(
        _rope_kernel,
        grid=(n // tb,),
        in_specs=[spec(dq, cols['qc']), spec(dk, cols['kc']), spec(dk, cols['vc']), spec(di, cols['qi']),
                  spec(LANES, cols['ki']), spec(LANES, cols['wi'])] + [row(LANES)] * 4,
        out_specs=[row(dq), row(dk), row(dk), row(dk),
                   pl.BlockSpec((H_IDX, tb, D_IDX), lambda i: (0, i, 0)), row(LANES), row(LANES), row(LANES)],
        out_shape=[jax.ShapeDtypeStruct((n, dq), BF16), jax.ShapeDtypeStruct((n, dk), F32),
                   jax.ShapeDtypeStruct((n, dk), BF16), jax.ShapeDtypeStruct((n, dk), BF16),
                   jax.ShapeDtypeStruct((H_IDX, n, D_IDX), BF16), jax.ShapeDtypeStruct((n, LANES), F32),
                   jax.ShapeDtypeStruct((n, LANES), BF16), jax.ShapeDtypeStruct((n, LANES), F32)],
        compiler_params=_params("arbitrary"),
        name="rope_split",
    )(proj, proj, proj, proj, proj, proj, *tabs)


def _dsa_kernel(qc_ref, qi_ref, wi_ref, kc_ref, vc_ref, ki_ref, tri_ref, o_ref, key_ref, mask_ref, *,
                bq, bk, q_off, s_real, s_pad, topk, scale):
    qi_blk = pl.program_id(1)
    r0 = qi_blk * bq
    adm_end = ((q_off + r0 + bq - 1) // CHUNK + 1) * CHUNK
    n_kb = jnp.minimum((adm_end + bk - 1) // bk, s_pad // bk)
    q_pos = q_off + r0 + lax.broadcasted_iota(I32, (bq, bk), 0)
    col = lax.broadcasted_iota(I32, (bq, bk), 1)
    kf = float(topk)

    def admissible(start):
        k_pos = col + start
        return ((k_pos // CHUNK) <= (q_pos // CHUNK)) & (k_pos < s_real)

    def lane_fold(x):
        acc = x[:, :LANES]
        for j in range(1, bk // LANES):
            acc = acc + x[:, j * LANES:(j + 1) * LANES]
        return acc

    def count(pred_fn):
        def body(kb, acc):
            start = pl.multiple_of(kb * bk, bk)
            blk = key_ref[:, pl.ds(start, bk)]
            return acc + lane_fold(jnp.where(pred_fn(blk), 1.0, 0.0))
        part = lax.fori_loop(0, n_kb, body, jnp.zeros((bq, LANES), F32))
        return jnp.sum(part, axis=1, keepdims=True)

    wi = wi_ref[0]
    wcols = [wi[:, h:h + 1] for h in range(H_IDX)]

    def score_body(kb, c):
        start = pl.multiple_of(kb * bk, bk)
        ki_blk = ki_ref[0, pl.ds(start, bk), :]
        acc = jnp.zeros((bq, bk), F32)
        for h in range(H_IDX):
            lg = lax.dot_general(qi_ref[h], ki_blk, (((1,), (1,)), ((), ())), preferred_element_type=F32)
            acc = acc + wcols[h] * jnp.maximum(lg, 0.0)
        bits = pltpu.bitcast(acc + 0.0, I32)
        key = jnp.where(bits < 0, bits ^ 0x7FFFFFFF, bits)
        key_ref[:, pl.ds(start, bk)] = jnp.where(admissible(start), key, INT_MIN)
        return c
    lax.fori_loop(0, n_kb, score_body, 0)

    prefix = jnp.where(count(lambda b: b >= 0) >= kf, 0, INT_MIN).astype(I32)

    def bit_body(it, prefix):
        cand = prefix | jnp.left_shift(jnp.int32(1), 30 - it)
        return jnp.where(count(lambda b: b >= cand) >= kf, cand, prefix)
    thr = lax.fori_loop(0, 31, bit_body, prefix)

    need = kf - count(lambda b: b > thr)
    tri = tri_ref[...]

    def mask_body(kb, run):
        start = pl.multiple_of(kb * bk, bk)
        blk = key_ref[:, pl.ds(start, bk)]
        tie = blk == thr
        tie_f = jnp.where(tie, 1.0, 0.0)
        before = jnp.dot(tie_f.astype(BF16), tri, preferred_element_type=F32) + run
        sel = ((blk > thr) | (tie & (before < need))) & admissible(start)
        mask_ref[:, pl.ds(start, bk)] = jnp.where(sel, 0.0, NEG)
        return run + jnp.sum(tie_f, axis=1, keepdims=True)
    lax.fori_loop(0, n_kb, mask_body, jnp.zeros((bq, 1), F32))

    rep = H_C // KV_C
    head = lambda g, r: slice((g * rep + r) * HEAD_DIM, (g * rep + r + 1) * HEAD_DIM)
    pair = 2
    for g0 in range(0, KV_C, pair):
        gs = range(g0, g0 + pair)
        qgs = [jnp.concatenate([qc_ref[0, :, head(g, r)] for r in range(rep)], axis=0) for g in gs]

        def att_body(kb, carry):
            start = pl.multiple_of(kb * bk, bk)
            mk = mask_ref[:, pl.ds(start, bk)]
            mk = jnp.concatenate([mk] * rep, axis=0)
            kv = [slice(g * HEAD_DIM, (g + 1) * HEAD_DIM) for g in gs]
            ss = [lax.dot_general(qg, kc_ref[0, pl.ds(start, bk), sl], (((1,), (1,)), ((), ())),
                                  preferred_element_type=F32) * scale + mk for qg, sl in zip(qgs, kv)]
            ms = [jnp.maximum(c[0], jnp.max(s, axis=1, keepdims=True)) for c, s in zip(carry, ss)]
            ps = [jnp.exp(s - m) for s, m in zip(ss, ms)]
            out = []
            for (m_old, l, acc), m, p, sl in zip(carry, ms, ps, kv):
                alpha = jnp.exp(m_old - m)
                out.append((m, alpha * l + jnp.sum(p, axis=1, keepdims=True),
                            alpha * acc + jnp.dot(p.astype(BF16), vc_ref[0, pl.ds(start, bk), sl],
                                                  preferred_element_type=F32)))
            return tuple(out)
        init = (jnp.full((rep * bq, 1), NEG, F32), jnp.zeros((rep * bq, 1), F32),
                jnp.zeros((rep * bq, HEAD_DIM), F32))
        res = lax.fori_loop(0, n_kb, att_body, (init,) * pair)
        for g, (_, l, acc) in zip(gs, res):
            o = acc / l
            for r in range(rep):
                o_ref[0, :, head(g, r)] = o[r * bq:(r + 1) * bq].astype(o_ref.dtype)


def dsa_attention(qc, qi_hm, wi, kc, vc, ki, q_off, s_real, topk, bq, bk):
    b, tq, _ = qc.shape
    s_pad = kc.shape[1]
    assert tq % bq == 0 and s_pad % bk == 0 and bk % LANES == 0
    nq = tq // bq
    tri = jnp.asarray(np.triu(np.ones((bk, bk), np.float32), 1), BF16)
    kern = functools.partial(_dsa_kernel, bq=bq, bk=bk, q_off=q_off, s_real=s_real, s_pad=s_pad,
                             topk=topk, scale=HEAD_DIM ** -0.5)
    once = dict(pipeline_mode=pl.Buffered(1))
    return pl.pallas_call(
        kern,
        grid=(b, nq),
        in_specs=[pl.BlockSpec((1, bq, H_C * HEAD_DIM), lambda b_, i: (b_, i, 0)),
                  pl.BlockSpec((H_IDX, bq, D_IDX), lambda b_, i: (0, b_ * nq + i, 0)),
                  pl.BlockSpec((1, bq, LANES), lambda b_, i: (b_, i, 0)),
                  pl.BlockSpec((1, s_pad, KV_C * HEAD_DIM), lambda b_, i: (b_, 0, 0), **once),
                  pl.BlockSpec((1, s_pad, KV_C * HEAD_DIM), lambda b_, i: (b_, 0, 0), **once),
                  pl.BlockSpec((1, s_pad, D_IDX), lambda b_, i: (b_, 0, 0), **once),
                  pl.BlockSpec((bk, bk), lambda b_, i: (0, 0))],
        out_specs=pl.BlockSpec((1, bq, H_C * HEAD_DIM), lambda b_, i: (b_, i, 0)),
        out_shape=jax.ShapeDtypeStruct((b, tq, H_C * HEAD_DIM), BF16),
        scratch_shapes=[pltpu.VMEM((bq, s_pad), I32), pltpu.VMEM((bq, s_pad), F32)],
        compiler_params=_params("arbitrary", "arbitrary"),
        name="dsa_attention",
    )(qc, qi_hm, wi, kc, vc, ki, tri)


HALO = 8


def _gdn_prep_kernel(x_ref, prev_ref, halo_ref, cw_ref, a_ref, b_ref, alog_ref, dtb_ref, o_ref, g_ref,
                     beta_ref, *, heads_per_blk, blocks_per_seq):
    j = pl.program_id(1)
    x = x_ref[...]
    tb = x.shape[0]
    halo = jnp.where(pl.program_id(0) % blocks_per_seq == 0, halo_ref[0], prev_ref[...])
    row = lax.broadcasted_iota(I32, halo.shape, 0)
    acc = x * cw_ref[CONV_W - 1:CONV_W, :]
    for k in range(1, CONV_W):
        rolled = pltpu.roll(x, k, 0)
        top = jnp.where(row < k, pltpu.roll(halo, k, 0), rolled[:HALO])
        shifted = jnp.concatenate([top, rolled[HALO:]], axis=0) if tb > HALO else top
        acc = acc + shifted * cw_ref[CONV_W - 1 - k:CONV_W - k, :]
    y = acc * (1.0 / (1.0 + jnp.exp(-acc)))
    seg = j // (H_D // heads_per_blk)
    scale = jnp.where(seg == 0, DK_D ** -0.5, 1.0)
    for h in range(heads_per_blk):
        sl = slice(h * LANES, (h + 1) * LANES)
        t = y[:, sl]
        nrm = t * lax.rsqrt(jnp.sum(t * t, axis=1, keepdims=True) + RMS_EPS) * scale
        o_ref[:, sl] = jnp.where(seg == 2, t, nrm)

    @pl.when(j == 0)
    def _():
        z = a_ref[...] + dtb_ref[...]
        softplus = jnp.maximum(z, 0.0) + jnp.log(1.0 + jnp.exp(-jnp.abs(z)))
        g_ref[...] = -jnp.exp(alog_ref[...]) * softplus
        beta_ref[...] = 1.0 / (1.0 + jnp.exp(-b_ref[...]))


def gdn_prep(proj, n_rows, halo, conv_w, a_log_pad, dt_bias_pad, qkv_col, a_col, b_col, tb):
    cb = 512
    hpb = cb // LANES
    bps = n_rows // tb // halo.shape[0]
    kern = functools.partial(_gdn_prep_kernel, heads_per_blk=hpb, blocks_per_seq=bps)
    vec = pl.BlockSpec((1, LANES), lambda i, j: (0, 0))
    return pl.pallas_call(
        kern,
        grid=(n_rows // tb, QKV_D // cb),
        in_specs=[pl.BlockSpec((tb, cb), lambda i, j: (i, qkv_col // cb + j)),
                  pl.BlockSpec((HALO, cb), lambda i, j: (jnp.maximum(i * (tb // HALO) - 1, 0),
                                                        qkv_col // cb + j)),
                  pl.BlockSpec((1, HALO, cb), lambda i, j: (i // bps, 0, j)),
                  pl.BlockSpec((CONV_W, cb), lambda i, j: (0, j)),
                  pl.BlockSpec((tb, LANES), lambda i, j: (i, a_col // LANES)),
                  pl.BlockSpec((tb, LANES), lambda i, j: (i, b_col // LANES)), vec, vec],
        out_specs=[pl.BlockSpec((tb, cb), lambda i, j: (i, j)),
                   pl.BlockSpec((tb, LANES), lambda i, j: (i, 0)),
                   pl.BlockSpec((tb, LANES), lambda i, j: (i, 0))],
        out_shape=[jax.ShapeDtypeStruct((n_rows, QKV_D), F32), jax.ShapeDtypeStruct((n_rows, LANES), F32),
                   jax.ShapeDtypeStruct((n_rows, LANES), F32)],
        compiler_params=_params("arbitrary", "arbitrary"),
        name="gdn_prep",
    )(proj, proj, halo, conv_w, proj, proj, a_log_pad, dt_bias_pad)


def _split3(x):
    a = x.astype(BF16)
    r = x - a.astype(F32)
    b = r.astype(BF16)
    c = (r - b.astype(F32)).astype(BF16)
    return a, b, c


def _dot_f32(a, b):
    a_hi, a_lo, _ = _split3(a)
    b_hi, b_lo, _ = _split3(b)
    return (jnp.dot(a_hi, b_hi, preferred_element_type=F32) + jnp.dot(a_hi, b_lo, preferred_element_type=F32)
            + jnp.dot(a_lo, b_hi, preferred_element_type=F32))


def _gdn_local_kernel(q_ref, k_ref, v_ref, g_ref, beta_ref, solv_ref, solk_ref, qe_ref, kd_ref, qkd_ref,
                      egl_ref, *, c, hb):
    assert hb == H_D
    ri = lax.broadcasted_iota(I32, (c, c), 0)
    ci = lax.broadcasted_iota(I32, (c, c), 1)
    incl, strict, eye = ri >= ci, ri > ci, ri == ci
    tril = jnp.where(incl, 1.0, 0.0).astype(BF16)
    g_parts = _split3(g_ref[...])
    gcum = sum(jnp.dot(tril, p, preferred_element_type=F32) for p in g_parts)
    gcum_t = sum(lax.dot_general(p, tril, (((0,), (1,)), ((), ())), preferred_element_type=F32)
                 for p in g_parts)
    beta_all = beta_ref[...]
    egl_ref[0] = jnp.exp(gcum[c - 1:c, :])
    ident = jnp.where(eye, 1.0, 0.0)
    heads = range(hb)
    sls = [slice(hh * LANES, (hh + 1) * LANES) for hh in heads]
    nmats, rhs = [], []
    for hh in heads:
        gc = gcum[:, hh:hh + 1]
        bc = beta_all[:, hh:hh + 1]
        decay = jnp.where(incl, jnp.exp(jnp.where(incl, gc - gcum_t[hh:hh + 1, :], 0.0)), 0.0)
        q, k, v = q_ref[:, sls[hh]], k_ref[:, sls[hh]], v_ref[:, sls[hh]]
        qb, kb = q.astype(BF16), k.astype(BF16)
        kk = lax.dot_general(kb, kb, (((1,), (1,)), ((), ())), preferred_element_type=F32)
        qk = lax.dot_general(qb, kb, (((1,), (1,)), ((), ())), preferred_element_type=F32)
        nmats.append(jnp.where(strict, bc * kk * decay, 0.0))
        rhs.append(jnp.concatenate([v * bc, k * (bc * jnp.exp(gc))], axis=1))
        qkd_ref[0, hh] = (qk * decay).astype(BF16)
        qe_ref[:, sls[hh]] = (q * jnp.exp(gc)).astype(BF16)
        kd_ref[:, sls[hh]] = (k * jnp.exp(gc[c - 1:c, :] - gc)).astype(BF16)
    invs = [ident - n for n in nmats]
    pows = nmats
    for _ in range(int(np.log2(c)) - 1):
        pows = [_dot_f32(p, p) for p in pows]
        invs = [t + _dot_f32(t, p) for t, p in zip(invs, pows)]
    for hh in heads:
        y = _dot_f32(invs[hh], rhs[hh])
        solv_ref[:, sls[hh]] = y[:, :DV_D]
        solk_ref[:, sls[hh]] = y[:, DV_D:].astype(BF16)


def gdn_local(qkv, g, beta, c, hb=H_D):
    n = qkv.shape[0]
    nc = n // c
    dh = H_D * DK_D
    kern = functools.partial(_gdn_local_kernel, c=c, hb=hb)
    w = hb * LANES
    seg = lambda s: pl.BlockSpec((c, w), lambda i, j: (i, s * (H_D // hb) + j))
    col = pl.BlockSpec((c, w), lambda i, j: (i, j))
    meta = pl.BlockSpec((c, LANES), lambda i, j: (i, 0))
    return pl.pallas_call(
        kern,
        grid=(nc, H_D // hb),
        in_specs=[seg(0), seg(1), seg(2), meta, meta],
        out_specs=[col, col, col, col,
                   pl.BlockSpec((1, hb, c, c), lambda i, j: (i, j, 0, 0)),
                   pl.BlockSpec((1, 1, LANES), lambda i, j: (i, 0, 0))],
        out_shape=[jax.ShapeDtypeStruct((n, dh), F32), jax.ShapeDtypeStruct((n, dh), BF16),
                   jax.ShapeDtypeStruct((n, dh), BF16), jax.ShapeDtypeStruct((n, dh), BF16),
                   jax.ShapeDtypeStruct((nc, H_D, c, c), BF16),
                   jax.ShapeDtypeStruct((nc, 1, LANES), F32)],
        compiler_params=_params("arbitrary", "arbitrary"),
        name="gdn_local",
    )(qkv, qkv, qkv, g, beta)


def _gdn_scan_kernel(s0_ref, solv_ref, solk_ref, qe_ref, kd_ref, qkd_ref, egl_ref, gate_ref, nw_ref,
                     od_ref, sout_ref, s_ref):
    ci = pl.program_id(1)

    @pl.when(ci == 0)
    def _():
        s_ref[...] = s0_ref[0]

    egl = egl_ref[0]
    nw = nw_ref[...]
    for h in range(H_D):
        sl = slice(h * LANES, (h + 1) * LANES)
        s_old = s_ref[h]
        sb = s_old.astype(BF16)
        u = solv_ref[:, sl] - jnp.dot(solk_ref[:, sl], sb, preferred_element_type=F32)
        ub = u.astype(BF16)
        o = (jnp.dot(qe_ref[:, sl], sb, preferred_element_type=F32)
             + jnp.dot(qkd_ref[0, h], ub, preferred_element_type=F32))
        s_ref[h] = s_old * egl[:, h:h + 1] + lax.dot_general(
            kd_ref[:, sl], ub, (((0,), (0,)), ((), ())), preferred_element_type=F32)
        o = o * lax.rsqrt(jnp.mean(o * o, axis=1, keepdims=True) + RMS_EPS) * nw
        gt = gate_ref[:, sl]
        od_ref[:, sl] = (o * (gt * (1.0 / (1.0 + jnp.exp(-gt))))).astype(od_ref.dtype)

    @pl.when(ci == pl.num_programs(1) - 1)
    def _():
        sout_ref[0] = s_ref[...]


def gdn_scan(s0, solv, solk, qe, kd, qkd, egl, gate_arr, gate_col, norm_w, c):
    b = s0.shape[0]
    n = solv.shape[0]
    cps = n // b // c
    dh = H_D * DV_D
    col = pl.BlockSpec((c, dh), lambda b_, i: (b_ * cps + i, 0))
    st = pl.BlockSpec((1, H_D, DK_D, DV_D), lambda b_, i: (b_, 0, 0, 0))
    return pl.pallas_call(
        _gdn_scan_kernel,
        grid=(b, cps),
        in_specs=[st, col, col, col, col,
                  pl.BlockSpec((1, H_D, c, c), lambda b_, i: (b_ * cps + i, 0, 0, 0)),
                  pl.BlockSpec((1, 1, LANES), lambda b_, i: (b_ * cps + i, 0, 0)),
                  pl.BlockSpec((c, dh), lambda b_, i: (b_ * cps + i, gate_col // dh)),
                  pl.BlockSpec((1, LANES), lambda b_, i: (0, 0))],
        out_specs=[col, st],
        out_shape=[jax.ShapeDtypeStruct((n, dh), BF16), jax.ShapeDtypeStruct(s0.shape, F32)],
        scratch_shapes=[pltpu.VMEM((H_D, DK_D, DV_D), F32)],
        compiler_params=_params("arbitrary", "arbitrary"),
        name="gdn_scan",
    )(s0, solv, solk, qe, kd, qkd, egl, gate_arr, norm_w.reshape(1, LANES))


def gdn_mixer(proj, n_rows, cols, halo, s0, conv_w, a_log, dt_bias, norm_w, c, tb):
    pad = lambda v: jnp.pad(v.astype(F32), (0, LANES - H_D)).reshape(1, LANES)
    qkv, g, beta = gdn_prep(proj, n_rows, halo, conv_w, pad(a_log), pad(dt_bias),
                            cols['qkv'], cols['a'], cols['b'], tb)
    solv, solk, qe, kd, qkd, egl = gdn_local(qkv, g, beta, c)
    return gdn_scan(s0, solv, solk, qe, kd, qkd, egl, proj, cols['g'], norm_w, c)


ODD_COLS = {}
_orig, _new = 0, 0
_order = dict(qc=0, kc=1, vc=2, qi=3, ki=4, wi=5, qkv=6, a=7, b=8, g=9)
_starts = np.concatenate([[0], np.cumsum(ODD_SIZES)])
for _name in ('qc', 'g', 'kc', 'vc', 'qi', 'qkv', 'ki', 'wi', 'a', 'b'):
    _w = ODD_SIZES[_order[_name]]
    ODD_COLS[_name] = (_new, int(_starts[_order[_name]]), _w)
    _new += -(-_w // LANES) * LANES
ODD_WIDTH = _new


def reorder_w_in_odd(w):
    parts = []
    for name in ('qc', 'g', 'kc', 'vc', 'qi', 'qkv', 'ki', 'wi', 'a', 'b'):
        _, o0, wd = ODD_COLS[name]
        parts.append(jnp.pad(w[:, o0:o0 + wd], ((0, 0), (0, (-wd) % LANES))))
    return jnp.concatenate(parts, axis=1).astype(BF16)


def odd_layer(x, n_p, bs, n, cache_c_k, cache_c_v, cache_c_idx_k, state_d_ssm, state_d_conv,
              w_in_r, w_out, conv_w, a_log, dt_bias, norm_w):
    cols = {k: v[0] for k, v in ODD_COLS.items()}
    past = cache_c_k.shape[1]
    assert n >= CONV_W - 1 and n_p % 256 == 0
    proj = matmul(x.astype(BF16), w_in_r, tn=ODD_WIDTH // 10)
    pos = jnp.concatenate([jnp.arange(n_p), jnp.tile(past + jnp.arange(n), bs)])
    qc_b, kc_f, kc_b, vc_b, qi_hm, ki_f, ki_b, wi_f = rope_split(proj, pos, cols)
    dk = KV_C * HEAD_DIM
    topk_p = min(TOPK_MAX, n_p // 4)
    oc_p = dsa_attention(qc_b[:n_p][None], qi_hm[:, :n_p], wi_f[:n_p][None], kc_b[:n_p][None],
                         vc_b[:n_p][None], ki_b[:n_p, :D_IDX][None], 0, n_p, topk_p, 128, 512)
    bk_s = 384
    s_real = past + n
    s_pad = -(-s_real // bk_s) * bk_s
    cat = lambda c, new: jnp.concatenate(
        [c.reshape(bs, past, -1).astype(BF16), new, jnp.zeros((bs, s_pad - s_real, new.shape[-1]), BF16)], 1)
    oc_s = dsa_attention(qc_b[n_p:].reshape(bs, n, -1), qi_hm[:, n_p:], wi_f[n_p:].reshape(bs, n, LANES),
                         cat(cache_c_k, kc_b[n_p:].reshape(bs, n, dk)),
                         cat(cache_c_v, vc_b[n_p:].reshape(bs, n, dk)),
                         cat(cache_c_idx_k, ki_b[n_p:, :D_IDX].reshape(bs, n, D_IDX)),
                         past, s_real, min(TOPK_MAX, s_real // 4), n, bk_s)
    tb_p = 256
    q0 = cols['qkv']
    halo_p = jnp.zeros((1, HALO, QKV_D), F32)
    s0_p = jnp.zeros((1, H_D, DK_D, DV_D), F32)
    od_p, s_p = gdn_mixer(proj, n_p, cols, halo_p, s0_p, conv_w, a_log, dt_bias, norm_w, CHUNK, tb_p)
    proj_s = proj[n_p:]
    halo_s = jnp.concatenate([jnp.zeros((bs, HALO - (CONV_W - 1), QKV_D), F32), state_d_conv], 1)
    od_s, s_s = gdn_mixer(proj_s, bs * n, cols, halo_s, state_d_ssm, conv_w, a_log, dt_bias, norm_w, n, n)
    dh = H_C * HEAD_DIM
    mp = matmul_concat(oc_p.reshape(n_p, dh), od_p, w_out)
    ms = matmul_concat(oc_s.reshape(bs * n, dh), od_s, w_out)
    v0 = cols['vc']
    qkv_s = proj_s[:, q0:q0 + QKV_D].reshape(bs, n, QKV_D)
    caches = (kc_f[:n_p].reshape(1, n_p, KV_C, HEAD_DIM), proj[:n_p, v0:v0 + dk].reshape(1, n_p, KV_C, HEAD_DIM),
              ki_f[:n_p, :D_IDX][None],
              kc_f[n_p:].reshape(bs, n, KV_C, HEAD_DIM), proj_s[:, v0:v0 + dk].reshape(bs, n, KV_C, HEAD_DIM),
              ki_f[n_p:, :D_IDX].reshape(bs, n, D_IDX),
              s_p, proj[n_p - (CONV_W - 1):n_p, q0:q0 + QKV_D][None],
              s_s, qkv_s[:, n - (CONV_W - 1):])
    return mp, ms, caches


def kernel(x_prompt, x_sample, cache_a_k, cache_a_v, cache_b_k, cache_b_v, cache_c_k, cache_c_v,
           cache_c_idx_k, state_d_ssm, state_d_conv, w_in_even, w_out_even, b_rel_bias, w_in_odd,
           w_out_odd, d_conv_w, d_a_log, d_dt_bias, d_norm_w, w_router, router_bias, moe_w_gate,
           moe_w_up, moe_w_down, ln_g, ln_b):
    bp, t, d = x_prompt.shape
    bs, n, _ = x_sample.shape
    assert bp == 1
    n_p = bp * t
    w_in_even = w_in_even.astype(BF16)
    w_out_even = w_out_even.astype(BF16)
    w_in_odd_r = reorder_w_in_odd(w_in_odd)
    w_out_odd = w_out_odd.astype(BF16)
    wr_pad = jnp.pad(w_router.astype(BF16), ((0, 0), (0, LANES - N_EXPERTS)))
    rb_pad = jnp.pad(router_bias.astype(F32), (0, LANES - N_EXPERTS)).reshape(1, LANES)
    x = jnp.concatenate([x_prompt.reshape(n_p, d), x_sample.reshape(bs * n, d)], 0)
    for layer in range(DEPTH):
        if layer % 2 == 0:
            xp, xs = x[:n_p].reshape(bp, t, d), x[n_p:].reshape(bs, n, d)
            mp, ms, (a_k_p, a_v_p, a_k_s, a_v_s, b_k_p, b_v_p, b_k_s, b_v_s) = even_layer(
                xp, xs, cache_a_k, cache_a_v, cache_b_k, cache_b_v, w_in_even, w_out_even, b_rel_bias)
        else:
            mp, ms, (c_k_p, c_v_p, c_i_p, c_k_s, c_v_s, c_i_s, d_s_p, d_c_p, d_s_s, d_c_s) = odd_layer(
                x, n_p, bs, n, cache_c_k, cache_c_v, cache_c_idx_k, state_d_ssm, state_d_conv,
                w_in_odd_r, w_out_odd, d_conv_w, d_a_log, d_dt_bias, d_norm_w)
        m = jnp.concatenate([mp.reshape(n_p, d), ms.reshape(bs * n, d)], 0)
        x = moe_block(x, m, ln_g[layer], ln_b[layer], wr_pad, rb_pad,
                      moe_w_gate, moe_w_up, moe_w_down, layer)
    xp, xs = x[:n_p].reshape(bp, t, d), x[n_p:].reshape(bs, n, d)
    return (xp, xs, a_k_p, a_v_p, a_k_s, a_v_s, b_k_p, b_v_p, b_k_s, b_v_s,
            c_k_p, c_v_p, c_i_p, c_k_s, c_v_s, c_i_s, d_s_p, d_c_p, d_s_s, d_c_s)
```

```python
import functools

import jax
import jax.numpy as jnp
import numpy as np
from jax import lax
from jax.experimental import pallas as pl
from jax.experimental.pallas import tpu as pltpu

D_MODEL = 4096
DEPTH = 2
CHUNK = 64
HEAD_DIM = 128
Q_BLOCK = 128
NEG = -1e30
H_A = 16
H_B = 16
N_LEFT_CHUNKS = 8
BAND_REACH = N_LEFT_CHUNKS * CHUNK
REL_CLIP = 128
EVEN_SIZES = (H_A * HEAD_DIM,) * 3 + (H_B * HEAD_DIM,) * 3
H_C = 16
KV_C = 4
H_IDX = 16
D_IDX = 64
TOPK_MAX = 256
H_D = 16
DK_D = 128
DV_D = 128
CONV_W = 4
QKV_D = H_D * (2 * DK_D + DV_D)
ODD_SIZES = (H_C * HEAD_DIM, KV_C * HEAD_DIM, KV_C * HEAD_DIM, H_IDX * D_IDX, D_IDX, H_IDX,
             QKV_D, H_D, H_D, H_D * DV_D)
N_EXPERTS = 16
N_GROUPS = 4
EXPERTS_PER_GROUP = N_EXPERTS // N_GROUPS
TOP_K = 2
D_EXPERT = 1024
ROPE_THETA = 10000.0
LN_EPS = 1e-5
RMS_EPS = 1e-6
DEEPNORM_ALPHA = (2 * DEPTH) ** 0.25

F32 = jnp.float32
BF16 = jnp.bfloat16
I32 = jnp.int32

F32_EXP_UNDERFLOW = -105.0
LANES = 128
VMEM_LIMIT_BYTES = 56 * 1024 * 1024


def _params(*sem):
    return pltpu.CompilerParams(dimension_semantics=sem, vmem_limit_bytes=VMEM_LIMIT_BYTES)


def _cast_rows(src_ref, dst_ref, dst_off=0, chunk=256):
    rows = src_ref.shape[0]
    assert rows % chunk == 0 and dst_off % chunk == 0

    def body(c, carry):
        r = pl.multiple_of(c * chunk, chunk)
        dst_ref[pl.ds(dst_off + r, chunk), :] = src_ref[pl.ds(r, chunk), :].astype(dst_ref.dtype)
        return carry
    lax.fori_loop(0, rows // chunk, body, 0)


def _matmul_kernel(x_ref, w_ref, o_ref):
    o_ref[...] = jnp.dot(x_ref[...], w_ref[...], preferred_element_type=F32)


def matmul(x, w, tm=512, tn=1024):
    m, k = x.shape
    n = w.shape[1]
    tm = min(tm, m)
    tn = min(tn, n)
    assert m % tm == 0 and n % tn == 0
    return pl.pallas_call(
        _matmul_kernel,
        grid=(n // tn, m // tm),
        in_specs=[pl.BlockSpec((tm, k), lambda j, i: (i, 0)),
                  pl.BlockSpec((k, tn), lambda j, i: (0, j))],
        out_specs=pl.BlockSpec((tm, tn), lambda j, i: (i, j)),
        out_shape=jax.ShapeDtypeStruct((m, n), F32),
        compiler_params=_params("arbitrary", "arbitrary"),
        name="proj_matmul",
    )(x, w)


def _matmul2_kernel(xa_ref, xb_ref, w_ref, o_ref):
    ka = xa_ref.shape[1]
    o_ref[...] = (jnp.dot(xa_ref[...], w_ref[:ka, :], preferred_element_type=F32)
                  + jnp.dot(xb_ref[...], w_ref[ka:, :], preferred_element_type=F32))


def matmul_concat(xa, xb, w, tm=512, tn=1024):
    m, ka = xa.shape
    kb = xb.shape[1]
    n = w.shape[1]
    tm = min(tm, m)
    assert m % tm == 0 and n % tn == 0
    return pl.pallas_call(
        _matmul2_kernel,
        grid=(n // tn, m // tm),
        in_specs=[pl.BlockSpec((tm, ka), lambda j, i: (i, 0)),
                  pl.BlockSpec((tm, kb), lambda j, i: (i, 0)),
                  pl.BlockSpec((ka + kb, tn), lambda j, i: (0, j))],
        out_specs=pl.BlockSpec((tm, tn), lambda j, i: (i, j)),
        out_shape=jax.ShapeDtypeStruct((m, n), F32),
        compiler_params=_params("arbitrary", "arbitrary"),
        name="out_matmul",
    )(xa, xb, w)


def _stage_rows(piece_refs, dst_ref):
    off = 0
    for ref in piece_refs:
        rows = ref.shape[1]
        if rows % 256 == 0 and off % 256 == 0:
            _cast_rows(ref.at[0], dst_ref, dst_off=off)
        else:
            dst_ref[off:off + rows, :] = ref[0].astype(dst_ref.dtype)
        off += rows
    if off < dst_ref.shape[0]:
        dst_ref[off:, :] = jnp.zeros((dst_ref.shape[0] - off, dst_ref.shape[1]), dst_ref.dtype)


def _sb_kernel(q_ref, *refs, n_pieces, bq, bk, q_off, tk, scale):
    k_refs, v_refs = refs[:n_pieces], refs[n_pieces:2 * n_pieces]
    u_ref, o_ref, kb_ref, vb_ref = refs[2 * n_pieces:]
    qi = pl.program_id(2)

    @pl.when(qi == 0)
    def _():
        _stage_rows(k_refs, kb_ref)
        _stage_rows(v_refs, vb_ref)

    q = q_ref[0].astype(BF16)
    r0 = qi * bq
    rows = q_off + r0 + lax.broadcasted_iota(I32, (bq, bk), 0)
    cols = lax.broadcasted_iota(I32, (bq, bk), 1)
    n_kb = jnp.minimum((q_off + r0 + bq + bk - 1) // bk, tk // bk)
    tri = u_ref[...]

    def live(carry):
        it, _, _, max_run = carry
        return (it < n_kb) & (max_run > F32_EXP_UNDERFLOW)

    def body(carry):
        it, acc, run, _ = carry
        start = pl.multiple_of((n_kb - 1 - it) * bk, bk)
        kblk = kb_ref[pl.ds(start, bk), :]
        vblk = vb_ref[pl.ds(start, bk), :]
        z = lax.dot_general(q, kblk, (((1,), (1,)), ((), ())), preferred_element_type=F32) * scale
        softplus = jnp.maximum(z, 0.0) + jnp.log(1.0 + jnp.exp(-jnp.abs(z)))
        earlier = (cols + start) < rows
        log_1mb = jnp.where(earlier, -softplus, 0.0)
        log_b = z - softplus
        hi = log_1mb.astype(BF16)
        lo = (log_1mb - hi.astype(F32)).astype(BF16)
        between = (jnp.dot(hi, tri, preferred_element_type=F32)
                   + jnp.dot(lo, tri, preferred_element_type=F32) + run)
        w = jnp.where(earlier, jnp.exp(log_b + between), 0.0)
        acc = acc + jnp.dot(w.astype(BF16), vblk, preferred_element_type=F32)
        run = run + jnp.sum(log_1mb, axis=1, keepdims=True)
        return it + 1, acc, run, jnp.max(run)

    _, acc, _, _ = lax.while_loop(
        live, body, (jnp.int32(0), jnp.zeros((bq, HEAD_DIM), F32), jnp.zeros((bq, 1), F32), jnp.float32(0.0)))
    o_ref[0] = acc.astype(o_ref.dtype)


def _piece_specs(pieces, which):
    specs = []
    for piece in pieces:
        arr, cb = piece[which], piece[2 + which]
        specs.append(pl.BlockSpec((1, arr.shape[1], HEAD_DIM), lambda b_, h, *_, cb=cb: (b_, 0, cb + h)))
    return specs


def stick_breaking_attention(q_arr, q_cb, kv_pieces, n_heads, q_off, bq, bk):
    b, tq, _ = q_arr.shape
    tk = -(-sum(p[0].shape[1] for p in kv_pieces) // bk) * bk
    assert tq % bq == 0
    tri = jnp.asarray(np.tril(np.ones((bk, bk), np.float32), -1), BF16)
    kern = functools.partial(_sb_kernel, n_pieces=len(kv_pieces), bq=bq, bk=bk, q_off=q_off, tk=tk,
                             scale=HEAD_DIM ** -0.5)
    return pl.pallas_call(
        kern,
        grid=(b, n_heads, tq // bq),
        in_specs=[pl.BlockSpec((1, bq, HEAD_DIM), lambda b_, h, i: (b_, i, q_cb + h))]
        + _piece_specs(kv_pieces, 0) + _piece_specs(kv_pieces, 1)
        + [pl.BlockSpec((bk, bk), lambda b_, h, i: (0, 0))],
        out_specs=pl.BlockSpec((1, bq, HEAD_DIM), lambda b_, h, i: (b_, i, h)),
        out_shape=jax.ShapeDtypeStruct((b, tq, n_heads * HEAD_DIM), BF16),
        scratch_shapes=[pltpu.VMEM((tk, HEAD_DIM), BF16), pltpu.VMEM((tk, HEAD_DIM), BF16)],
        compiler_params=_params("arbitrary", "arbitrary", "arbitrary"),
        name="stick_breaking",
    )(q_arr, *[p[0] for p in kv_pieces], *[p[1] for p in kv_pieces], tri)


def _softmax_pv(s, v):
    m = jnp.max(s, axis=1, keepdims=True)
    e = jnp.exp(s - m)
    p = e / jnp.sum(e, axis=1, keepdims=True)
    return jnp.dot(p.astype(BF16), v, preferred_element_type=F32)


def _band_prompt_kernel(q_ref, k_ref, v_ref, bias_ref, o_ref, kb_ref, vb_ref, *, bq, reach, scale):
    qi = pl.program_id(2)

    @pl.when(qi == 0)
    def _():
        kb_ref[:reach, :] = jnp.zeros((reach, HEAD_DIM), BF16)
        vb_ref[:reach, :] = jnp.zeros((reach, HEAD_DIM), BF16)
        _cast_rows(k_ref.at[0], kb_ref, dst_off=reach)
        _cast_rows(v_ref.at[0], vb_ref, dst_off=reach)

    win = reach + bq
    start = pl.multiple_of(qi * bq, bq)
    q = q_ref[0].astype(BF16)
    kw = kb_ref[pl.ds(start, win), :]
    vw = vb_ref[pl.ds(start, win), :]
    s = lax.dot_general(q, kw, (((1,), (1,)), ((), ())), preferred_element_type=F32) * scale
    k_pos = start - reach + lax.broadcasted_iota(I32, (bq, win), 1)
    s = jnp.where(k_pos >= 0, s + bias_ref[0], NEG)
    o_ref[0] = _softmax_pv(s, vw).astype(o_ref.dtype)


def band_attention_prompt(proj, q_cb, k_cb, v_cb, rel_bias, bq=512):
    b, t, _ = proj.shape
    assert bq % CHUNK == 0 and t % bq == 0
    win = BAND_REACH + bq
    i = np.arange(bq)[:, None]
    j = np.arange(win)[None, :]
    qc, kc = i // CHUNK, j // CHUNK - N_LEFT_CHUNKS
    in_band = (kc <= qc) & (kc >= qc - N_LEFT_CHUNKS)
    period = win + bq
    k = np.arange(period)
    i_minus_j = np.where(k < win, -k, period - k)
    rel = np.clip(BAND_REACH + i_minus_j, -REL_CLIP, REL_CLIP) + REL_CLIP
    vec = rel_bias.astype(F32)[:, rel]
    toep = jnp.tile(vec, (1, bq))[:, :bq * (period - 1)].reshape(-1, bq, period - 1)[:, :, :win]
    bias = jnp.where(jnp.asarray(in_band)[None], toep, NEG)
    kern = functools.partial(_band_prompt_kernel, bq=bq, reach=BAND_REACH, scale=HEAD_DIM ** -0.5)
    return pl.pallas_call(
        kern,
        grid=(b, H_B, t // bq),
        in_specs=[pl.BlockSpec((1, bq, HEAD_DIM), lambda b_, h, i_: (b_, i_, q_cb + h)),
                  pl.BlockSpec((1, t, HEAD_DIM), lambda b_, h, i_: (b_, 0, k_cb + h)),
                  pl.BlockSpec((1, t, HEAD_DIM), lambda b_, h, i_: (b_, 0, v_cb + h)),
                  pl.BlockSpec((1, bq, win), lambda b_, h, i_: (h, 0, 0))],
        out_specs=pl.BlockSpec((1, bq, HEAD_DIM), lambda b_, h, i_: (b_, i_, h)),
        out_shape=jax.ShapeDtypeStruct((b, t, H_B * HEAD_DIM), BF16),
        scratch_shapes=[pltpu.VMEM((BAND_REACH + t, HEAD_DIM), BF16),
                        pltpu.VMEM((BAND_REACH + t, HEAD_DIM), BF16)],
        compiler_params=_params("arbitrary", "arbitrary", "arbitrary"),
        name="band_prompt",
    )(proj, proj, proj, bias)


def _band_core_kernel(q_ref, *refs, n_pieces, scale):
    k_refs, v_refs = refs[:n_pieces], refs[n_pieces:2 * n_pieces]
    bias_ref, o_ref, kb_ref, vb_ref = refs[2 * n_pieces:]
    _stage_rows(k_refs, kb_ref)
    _stage_rows(v_refs, vb_ref)
    q = q_ref[0].astype(BF16)
    s = lax.dot_general(q, kb_ref[...], (((1,), (1,)), ((), ())), preferred_element_type=F32) * scale
    o_ref[0] = _softmax_pv(s + bias_ref[0], vb_ref[...]).astype(o_ref.dtype)


def band_attention_core(q_arr, q_cb, kv_pieces, bias):
    b, cq, _ = q_arr.shape
    h_n, _, ck = bias.shape
    kern = functools.partial(_band_core_kernel, n_pieces=len(kv_pieces), scale=HEAD_DIM ** -0.5)
    return pl.pallas_call(
        kern,
        grid=(b, h_n),
        in_specs=[pl.BlockSpec((1, cq, HEAD_DIM), lambda b_, h: (b_, 0, q_cb + h))]
        + _piece_specs(kv_pieces, 0) + _piece_specs(kv_pieces, 1)
        + [pl.BlockSpec((1, cq, ck), lambda b_, h: (h, 0, 0))],
        out_specs=pl.BlockSpec((1, cq, HEAD_DIM), lambda b_, h: (b_, 0, h)),
        out_shape=jax.ShapeDtypeStruct((b, cq, h_n * HEAD_DIM), BF16),
        scratch_shapes=[pltpu.VMEM((ck, HEAD_DIM), BF16), pltpu.VMEM((ck, HEAD_DIM), BF16)],
        compiler_params=_params("arbitrary", "arbitrary"),
        name="band_core",
    )(q_arr, *[p[0] for p in kv_pieces], *[p[1] for p in kv_pieces], bias)


def _layer_norm_rows(h, g, b):
    mu = jnp.mean(h, axis=-1, keepdims=True)
    d = h - mu
    var = jnp.mean(d * d, axis=-1, keepdims=True)
    return d * lax.rsqrt(var + LN_EPS) * g + b


def _ln_router_kernel(x_ref, m_ref, g_ref, b_ref, wr_ref, rb_ref, x1_ref, gate_ref, idx_ref):
    y = _layer_norm_rows(DEEPNORM_ALPHA * x_ref[...] + m_ref[...], g_ref[...], b_ref[...])
    x1_ref[...] = y
    tb = y.shape[0]
    logits = jnp.dot(y.astype(BF16), wr_ref[...], preferred_element_type=F32)
    lane = lax.broadcasted_iota(I32, (tb, LANES), 1)
    lane_f = lane.astype(F32)
    real = lane < N_EXPERTS
    lg = jnp.where(real, logits, -jnp.inf)
    e = jnp.exp(lg - jnp.max(lg, axis=1, keepdims=True))
    aff = e / jnp.sum(e, axis=1, keepdims=True)
    sel = jnp.where(real, aff + rb_ref[...], -jnp.inf)
    group = lane // EXPERTS_PER_GROUP
    best = jnp.zeros((tb, 1), I32)
    best_v = jnp.max(jnp.where(group == 0, sel, -jnp.inf), axis=1, keepdims=True)
    for gi in range(1, N_GROUPS):
        gv = jnp.max(jnp.where(group == gi, sel, -jnp.inf), axis=1, keepdims=True)
        upd = gv > best_v
        best = jnp.where(upd, gi, best)
        best_v = jnp.where(upd, gv, best_v)
    cand = jnp.where((group == best) & real, sel, NEG)
    m1 = jnp.max(cand, axis=1, keepdims=True)
    i1 = jnp.min(jnp.where(cand == m1, lane_f, float(LANES)), axis=1, keepdims=True)
    cand2 = jnp.where(lane_f == i1, -jnp.inf, cand)
    m2 = jnp.max(cand2, axis=1, keepdims=True)
    i2 = jnp.min(jnp.where(cand2 == m2, lane_f, float(LANES)), axis=1, keepdims=True)
    g1 = jnp.sum(jnp.where(lane_f == i1, aff, 0.0), axis=1, keepdims=True)
    g2 = jnp.sum(jnp.where(lane_f == i2, aff, 0.0), axis=1, keepdims=True)
    tot = g1 + g2
    gate_ref[...] = jnp.where(lane == 0, g1 / tot, jnp.where(lane == 1, g2 / tot, 0.0))
    idx_ref[...] = jnp.where(lane == 0, i1, jnp.where(lane == 1, i2, 0.0)).astype(I32)


def ln_router(x, m, g, b, w_router_pad, router_bias_pad, tb=256):
    n, d = x.shape
    assert n % tb == 0
    row = pl.BlockSpec((tb, d), lambda i: (i, 0))
    vec = pl.BlockSpec((1, d), lambda i: (0, 0))
    meta = pl.BlockSpec((tb, LANES), lambda i: (i, 0))
    return pl.pallas_call(
        _ln_router_kernel,
        grid=(n // tb,),
        in_specs=[row, row, vec, vec,
                  pl.BlockSpec((d, LANES), lambda i: (0, 0)),
                  pl.BlockSpec((1, LANES), lambda i: (0, 0))],
        out_specs=[row, meta, meta],
        out_shape=[jax.ShapeDtypeStruct((n, d), F32),
                   jax.ShapeDtypeStruct((n, LANES), F32),
                   jax.ShapeDtypeStruct((n, LANES), I32)],
        compiler_params=_params("arbitrary"),
        name="ln_router",
    )(x, m, g.reshape(1, d), b.reshape(1, d), w_router_pad, router_bias_pad)


def _wait_rows(src_hbm, dst, sem, n):
    def body(r, c):
        pltpu.make_async_copy(src_hbm.at[pl.ds(0, 1), :], dst.at[pl.ds(0, 1), :], sem).wait()
        return c
    lax.fori_loop(0, n, body, 0)


def _gather_cast_kernel(tok_ref, na_ref, x_hbm, o_ref, buf, sem):
    tm = buf.shape[0]
    i = pl.program_id(0)
    base = i * tm

    @pl.when(i < na_ref[0])
    def _():
        def issue(r, c):
            pltpu.make_async_copy(x_hbm.at[pl.ds(tok_ref[base + r], 1), :], buf.at[pl.ds(r, 1), :], sem).start()
            return c
        lax.fori_loop(0, tm, issue, 0, unroll=8)
        _wait_rows(x_hbm, buf, sem, tm)
        _cast_rows(buf, o_ref)

    @pl.when(i >= na_ref[0])
    def _():
        o_ref[...] = jnp.zeros(o_ref.shape, o_ref.dtype)


def gather_rows_bf16(x, row_token, n_active, tm):
    n, d = x.shape
    r = row_token.shape[0]
    return pl.pallas_call(
        _gather_cast_kernel,
        grid_spec=pltpu.PrefetchScalarGridSpec(
            num_scalar_prefetch=2,
            grid=(r // tm,),
            in_specs=[pl.BlockSpec(memory_space=pl.ANY)],
            out_specs=pl.BlockSpec((tm, d), lambda i, tok, na: (i, 0)),
            scratch_shapes=[pltpu.VMEM((tm, d), F32), pltpu.SemaphoreType.DMA(())]),
        out_shape=jax.ShapeDtypeStruct((r, d), BF16),
        compiler_params=_params("arbitrary"),
        name="moe_gather",
    )(row_token, n_active, x)


def _expert_changed(te_ref, i):
    return (i == 0) | (te_ref[i] != te_ref[jnp.maximum(i - 1, 0)])


def _moe_up_kernel(te_ref, na_ref, xs_ref, wg_ref, wu_ref, h_ref, wgb, wub):
    i = pl.program_id(1)

    @pl.when(_expert_changed(te_ref, i))
    def _():
        _cast_rows(wg_ref, wgb)
        _cast_rows(wu_ref, wub)

    @pl.when(i < na_ref[0])
    def _():
        x = xs_ref[...]
        a = jnp.dot(x, wgb[...], preferred_element_type=F32)
        u = jnp.dot(x, wub[...], preferred_element_type=F32)
        h_ref[...] = (a * (1.0 / (1.0 + jnp.exp(-a))) * u).astype(h_ref.dtype)

    @pl.when(i >= na_ref[0])
    def _():
        h_ref[...] = jnp.zeros(h_ref.shape, h_ref.dtype)


def _moe_down_kernel(te_ref, na_ref, h_ref, wd_ref, gate_ref, y_ref, wdb):
    i = pl.program_id(1)

    @pl.when(_expert_changed(te_ref, i))
    def _():
        _cast_rows(wd_ref, wdb)

    @pl.when(i < na_ref[0])
    def _():
        y_ref[...] = gate_ref[...] * jnp.dot(h_ref[...], wdb[...], preferred_element_type=F32)

    @pl.when(i >= na_ref[0])
    def _():
        y_ref[...] = jnp.zeros(y_ref.shape, y_ref.dtype)


def moe_grouped(xs, gate_sorted, tile_expert, n_active, w_gate, w_up, w_down, layer, tm, tf=512, tn=1024):
    r, d = xs.shape
    n_tiles = r // tm
    de = w_gate.shape[3]
    tf, tn = min(tf, de), min(tn, d)
    h = pl.pallas_call(
        _moe_up_kernel,
        grid_spec=pltpu.PrefetchScalarGridSpec(
            num_scalar_prefetch=2,
            grid=(de // tf, n_tiles),
            in_specs=[pl.BlockSpec((tm, d), lambda f, i, te, na: (i, 0)),
                      pl.BlockSpec((None, None, d, tf), lambda f, i, te, na: (layer, te[i], 0, f)),
                      pl.BlockSpec((None, None, d, tf), lambda f, i, te, na: (layer, te[i], 0, f))],
            out_specs=pl.BlockSpec((tm, tf), lambda f, i, te, na: (i, f)),
            scratch_shapes=[pltpu.VMEM((d, tf), BF16), pltpu.VMEM((d, tf), BF16)]),
        out_shape=jax.ShapeDtypeStruct((r, de), BF16),
        compiler_params=_params("arbitrary", "arbitrary"),
        name="moe_up",
    )(tile_expert, n_active, xs, w_gate, w_up)
    return pl.pallas_call(
        _moe_down_kernel,
        grid_spec=pltpu.PrefetchScalarGridSpec(
            num_scalar_prefetch=2,
            grid=(d // tn, n_tiles),
            in_specs=[pl.BlockSpec((tm, de), lambda n, i, te, na: (i, 0)),
                      pl.BlockSpec((None, None, de, tn), lambda n, i, te, na: (layer, te[i], 0, n)),
                      pl.BlockSpec((tm, 1), lambda n, i, te, na: (i, 0))],
            out_specs=pl.BlockSpec((tm, tn), lambda n, i, te, na: (i, n)),
            scratch_shapes=[pltpu.VMEM((de, tn), BF16)]),
        out_shape=jax.ShapeDtypeStruct((r, d), F32),
        compiler_params=_params("arbitrary", "arbitrary"),
        name="moe_down",
    )(tile_expert, n_active, h, w_down, gate_sorted)


def _combine_ln_kernel(pos_ref, y_hbm, x_ref, g_ref, b_ref, xo_ref, buf, sem):
    tb = x_ref.shape[0]
    base = pl.program_id(0) * (2 * tb)

    def issue(r, c):
        pltpu.make_async_copy(y_hbm.at[pl.ds(pos_ref[base + r], 1), :], buf.at[pl.ds(r, 1), :], sem).start()
        return c
    lax.fori_loop(0, 2 * tb, issue, 0)
    _wait_rows(y_hbm, buf, sem, 2 * tb)
    f = buf[:tb, :] + buf[tb:, :]
    xo_ref[...] = _layer_norm_rows(DEEPNORM_ALPHA * x_ref[...] + f, g_ref[...], b_ref[...])


def combine_ln(y_sorted, pos_blocks, x1, g, b, tb):
    n, d = x1.shape
    row = pl.BlockSpec((tb, d), lambda i, pos: (i, 0))
    vec = pl.BlockSpec((1, d), lambda i, pos: (0, 0))
    return pl.pallas_call(
        _combine_ln_kernel,
        grid_spec=pltpu.PrefetchScalarGridSpec(
            num_scalar_prefetch=1,
            grid=(n // tb,),
            in_specs=[pl.BlockSpec(memory_space=pl.ANY), row, vec, vec],
            out_specs=row,
            scratch_shapes=[pltpu.VMEM((2 * tb, d), F32), pltpu.SemaphoreType.DMA(())]),
        out_shape=jax.ShapeDtypeStruct((n, d), F32),
        compiler_params=_params("arbitrary"),
        name="moe_combine_ln",
    )(pos_blocks, y_sorted, x1, g.reshape(1, d), b.reshape(1, d))


def moe_block(x, m, ln_g, ln_b, w_router_pad, router_bias_pad, w_gate, w_up, w_down, layer,
              tm=512, tb=128):
    n, d = x.shape
    x1, gate, idx = ln_router(x, m, ln_g[0], ln_b[0], w_router_pad, router_bias_pad)
    e_flat = idx[:, :TOP_K].reshape(-1)
    g_flat = gate[:, :TOP_K].reshape(-1)
    onehot = (e_flat[:, None] == jnp.arange(N_EXPERTS, dtype=I32)[None, :]).astype(I32)
    csum = jnp.cumsum(onehot, axis=0)
    rank = jnp.take_along_axis(csum, e_flat[:, None], axis=1)[:, 0] - 1
    counts = csum[-1]
    padded = (counts + tm - 1) // tm * tm
    ends = jnp.cumsum(padded)
    pos = (ends - padded)[e_flat] + rank
    n_pairs = TOP_K * n
    n_tiles = (n_pairs + N_EXPERTS * (tm - 1)) // tm + 1
    r = n_tiles * tm
    row_token = jnp.zeros((r,), I32).at[pos].set(jnp.arange(n_pairs, dtype=I32) // TOP_K)
    gate_sorted = jnp.zeros((r,), F32).at[pos].set(g_flat).reshape(r, 1)
    n_active = (ends[-1] // tm).astype(I32)
    tile = jnp.arange(n_tiles, dtype=I32)
    tile_expert = jnp.searchsorted(ends, jnp.minimum(tile, n_active - 1) * tm, side='right').astype(I32)
    tile_expert = jnp.minimum(tile_expert, N_EXPERTS - 1)
    n_active = n_active.reshape(1)
    xs = gather_rows_bf16(x1, row_token, n_active, tm)
    y_sorted = moe_grouped(xs, gate_sorted, tile_expert, n_active, w_gate, w_up, w_down, layer, tm)
    pos_blocks = pos.reshape(n // tb, tb, TOP_K).transpose(0, 2, 1).reshape(-1)
    return combine_ln(y_sorted, pos_blocks, x1, ln_g[1], ln_b[1], tb)


def even_layer(xp, xs, cache_a_k, cache_a_v, cache_b_k, cache_b_v, w_in, w_out, rel_bias):
    bp, t, d = xp.shape
    bs, n, _ = xs.shape
    past, lb = cache_a_k.shape[1], cache_b_k.shape[1]
    hd = H_A * HEAD_DIM
    proj_p = matmul(xp.reshape(bp * t, d).astype(BF16), w_in).reshape(bp, t, -1)
    proj_s = matmul(xs.reshape(bs * n, d).astype(BF16), w_in).reshape(bs, n, -1)
    nb = hd // HEAD_DIM
    qa_cb, ka_cb, va_cb, qb_cb, kb_cb, vb_cb = (s * nb for s in range(6))
    seg = lambda p, s: p[..., s * hd:(s + 1) * hd]
    oa_p = stick_breaking_attention(proj_p, qa_cb, [(proj_p, proj_p, ka_cb, va_cb)], H_A, 0, 256, 256)
    ob_p = band_attention_prompt(proj_p, qb_cb, kb_cb, vb_cb, rel_bias)
    flat = lambda c: c.reshape(bs, c.shape[1], hd)
    ka_s, va_s, kb_s, vb_s = seg(proj_s, 1), seg(proj_s, 2), seg(proj_s, 4), seg(proj_s, 5)
    oa_s = stick_breaking_attention(
        proj_s, qa_cb, [(flat(cache_a_k), flat(cache_a_v), 0, 0), (proj_s, proj_s, ka_cb, va_cb)],
        H_A, past, n, 256)
    ck = -(-(lb + n) // LANES) * LANES
    q_pos = past + np.arange(n)
    k_pos = np.concatenate([np.arange(past - lb, past), q_pos, np.full((ck - lb - n,), -1)])
    qc, kc = q_pos[:, None] // CHUNK, k_pos[None, :] // CHUNK
    valid = (kc <= qc) & (kc >= qc - N_LEFT_CHUNKS) & (k_pos[None, :] >= 0)
    rel = np.clip(q_pos[:, None] - k_pos[None, :], -REL_CLIP, REL_CLIP) + REL_CLIP
    bias_s = jnp.where(jnp.asarray(valid)[None], rel_bias.astype(F32)[:, rel], NEG)
    ob_s = band_attention_core(
        proj_s, qb_cb, [(flat(cache_b_k), flat(cache_b_v), 0, 0), (proj_s, proj_s, kb_cb, vb_cb)], bias_s)
    rows = lambda p, s: jnp.concatenate([p.reshape(bp * t, hd), s.reshape(bs * n, hd)], 0)
    m = matmul_concat(rows(oa_p, oa_s), rows(ob_p, ob_s), w_out)
    keep = min(BAND_REACH, t)
    heads = lambda a: a.reshape(a.shape[0], a.shape[1], H_A, HEAD_DIM)
    caches = (heads(seg(proj_p, 1)), heads(seg(proj_p, 2)), heads(ka_s), heads(va_s),
              heads(seg(proj_p, 4)[:, t - keep:]), heads(seg(proj_p, 5)[:, t - keep:]),
              heads(kb_s), heads(vb_s))
    return m, caches


INT_MIN = np.int32(-2 ** 31)


def _rope_kernel(qc_ref, kc_ref, vc_ref, qi_ref, ki_ref, wi_ref, c128_ref, s128_ref, c64_ref, s64_ref,
                 qc_o, kc_o, kcb_o, vcb_o, qi_o, ki_o, kib_o, wi_o):
    c128, s128, c64, s64 = c128_ref[...], s128_ref[...], c64_ref[...], s64_ref[...]
    lane = lax.broadcasted_iota(I32, c64.shape, 1)
    first_half = (lane % D_IDX) < (D_IDX // 2)

    def rot128(x):
        return x * c128 + pltpu.roll(x, HEAD_DIM // 2, 1) * s128

    def rot64(x):
        partner = jnp.where(first_half, pltpu.roll(x, LANES - D_IDX // 2, 1), pltpu.roll(x, D_IDX // 2, 1))
        return x * c64 + partner * s64

    for h in range(H_C):
        sl = slice(h * HEAD_DIM, (h + 1) * HEAD_DIM)
        qc_o[:, sl] = rot128(qc_ref[:, sl]).astype(BF16)
    for h in range(KV_C):
        sl = slice(h * HEAD_DIM, (h + 1) * HEAD_DIM)
        r = rot128(kc_ref[:, sl])
        kc_o[:, sl] = r
        kcb_o[:, sl] = r.astype(BF16)
    vcb_o[...] = vc_ref[...].astype(BF16)
    for j in range(H_IDX * D_IDX // LANES):
        r = rot64(qi_ref[:, j * LANES:(j + 1) * LANES]).astype(BF16)
        qi_o[2 * j] = r[:, :D_IDX]
        qi_o[2 * j + 1] = r[:, D_IDX:]
    r = rot64(ki_ref[...])
    ki_o[...] = r
    kib_o[...] = r.astype(BF16)
    wi_o[...] = wi_ref[...]


def rope_tables(pos):
    def tab(half, reps):
        inv = ROPE_THETA ** (-jnp.arange(half, dtype=F32) / half)
        ang = pos.astype(F32)[:, None] * inv[None, :]
        c, s = jnp.cos(ang), jnp.sin(ang)
        return jnp.tile(jnp.concatenate([c, c], 1), (1, reps)), jnp.tile(jnp.concatenate([-s, s], 1), (1, reps))
    c128, s128 = tab(HEAD_DIM // 2, 1)
    c64, s64 = tab(D_IDX // 2, 2)
    return c128, s128, c64, s64


def rope_split(proj, pos, cols, tb=256):
    n = proj.shape[0]
    assert n % tb == 0
    tabs = rope_tables(pos)
    dq, dk, di = H_C * HEAD_DIM, KV_C * HEAD_DIM, H_IDX * D_IDX
    spec = lambda w, c0: pl.BlockSpec((tb, w), lambda i: (i, c0 // w))
    row = lambda w: pl.BlockSpec((tb, w), lambda i: (i, 0))
    return pl.pallas_call(
        _rope_kernel,
        grid=(n // tb,),
        in_specs=[spec(dq, cols['qc']), spec(dk, cols['kc']), spec(dk, cols['vc']), spec(di, cols['qi']),
                  spec(LANES, cols['ki']), spec(LANES, cols['wi'])] + [row(LANES)] * 4,
        out_specs=[row(dq), row(dk), row(dk), row(dk),
                   pl.BlockSpec((H_IDX, tb, D_IDX), lambda i: (0, i, 0)), row(LANES), row(LANES), row(LANES)],
        out_shape=[jax.ShapeDtypeStruct((n, dq), BF16), jax.ShapeDtypeStruct((n, dk), F32),
                   jax.ShapeDtypeStruct((n, dk), BF16), jax.ShapeDtypeStruct((n, dk), BF16),
                   jax.ShapeDtypeStruct((H_IDX, n, D_IDX), BF16), jax.ShapeDtypeStruct((n, LANES), F32),
                   jax.ShapeDtypeStruct((n, LANES), BF16), jax.ShapeDtypeStruct((n, LANES), F32)],
        compiler_params=_params("arbitrary"),
        name="rope_split",
    )(proj, proj, proj, proj, proj, proj, *tabs)


def _dsa_kernel(qc_ref, qi_ref, wi_ref, kc_ref, vc_ref, ki_ref, tri_ref, o_ref, key_ref, mask_ref, *,
                bq, bk, q_off, s_real, s_pad, topk, scale):
    qi_blk = pl.program_id(1)
    r0 = qi_blk * bq
    adm_end = ((q_off + r0 + bq - 1) // CHUNK + 1) * CHUNK
    n_kb = jnp.minimum((adm_end + bk - 1) // bk, s_pad // bk)
    q_pos = q_off + r0 + lax.broadcasted_iota(I32, (bq, bk), 0)
    col = lax.broadcasted_iota(I32, (bq, bk), 1)
    kf = float(topk)

    def admissible(start):
        k_pos = col + start
        return ((k_pos // CHUNK) <= (q_pos // CHUNK)) & (k_pos < s_real)

    def lane_fold(x):
        acc = x[:, :LANES]
        for j in range(1, bk // LANES):
            acc = acc + x[:, j * LANES:(j + 1) * LANES]
        return acc

    def count(pred_fn):
        def body(kb, acc):
            start = pl.multiple_of(kb * bk, bk)
            blk = key_ref[:, pl.ds(start, bk)]
            return acc + lane_fold(jnp.where(pred_fn(blk), 1.0, 0.0))
        part = lax.fori_loop(0, n_kb, body, jnp.zeros((bq, LANES), F32))
        return jnp.sum(part, axis=1, keepdims=True)

    wi = wi_ref[0]
    wcols = [wi[:, h:h + 1] for h in range(H_IDX)]

    def score_body(kb, c):
        start = pl.multiple_of(kb * bk, bk)
        ki_blk = ki_ref[0, pl.ds(start, bk), :]
        acc = jnp.zeros((bq, bk), F32)
        for h in range(H_IDX):
            lg = lax.dot_general(qi_ref[h], ki_blk, (((1,), (1,)), ((), ())), preferred_element_type=F32)
            acc = acc + wcols[h] * jnp.maximum(lg, 0.0)
        bits = pltpu.bitcast(acc + 0.0, I32)
        key = jnp.where(bits < 0, bits ^ 0x7FFFFFFF, bits)
        key_ref[:, pl.ds(start, bk)] = jnp.where(admissible(start), key, INT_MIN)
        return c
    lax.fori_loop(0, n_kb, score_body, 0)

    prefix = jnp.where(count(lambda b: b >= 0) >= kf, 0, INT_MIN).astype(I32)

    def bit_body(it, prefix):
        cand = prefix | jnp.left_shift(jnp.int32(1), 30 - it)
        return jnp.where(count(lambda b: b >= cand) >= kf, cand, prefix)
    thr = lax.fori_loop(0, 31, bit_body, prefix)

    need = kf - count(lambda b: b > thr)
    tri = tri_ref[...]

    def mask_body(kb, run):
        start = pl.multiple_of(kb * bk, bk)
        blk = key_ref[:, pl.ds(start, bk)]
        tie = blk == thr
        tie_f = jnp.where(tie, 1.0, 0.0)
        before = jnp.dot(tie_f.astype(BF16), tri, preferred_element_type=F32) + run
        sel = ((blk > thr) | (tie & (before < need))) & admissible(start)
        mask_ref[:, pl.ds(start, bk)] = jnp.where(sel, 0.0, NEG)
        return run + jnp.sum(tie_f, axis=1, keepdims=True)
    lax.fori_loop(0, n_kb, mask_body, jnp.zeros((bq, 1), F32))

    rep = H_C // KV_C
    head = lambda g, r: slice((g * rep + r) * HEAD_DIM, (g * rep + r + 1) * HEAD_DIM)
    pair = 2
    for g0 in range(0, KV_C, pair):
        gs = range(g0, g0 + pair)
        qgs = [jnp.concatenate([qc_ref[0, :, head(g, r)] for r in range(rep)], axis=0) for g in gs]

        def att_body(kb, carry):
            start = pl.multiple_of(kb * bk, bk)
            mk = mask_ref[:, pl.ds(start, bk)]
            mk = jnp.concatenate([mk] * rep, axis=0)
            kv = [slice(g * HEAD_DIM, (g + 1) * HEAD_DIM) for g in gs]
            ss = [lax.dot_general(qg, kc_ref[0, pl.ds(start, bk), sl], (((1,), (1,)), ((), ())),
                                  preferred_element_type=F32) * scale + mk for qg, sl in zip(qgs, kv)]
            ms = [jnp.maximum(c[0], jnp.max(s, axis=1, keepdims=True)) for c, s in zip(carry, ss)]
            ps = [jnp.exp(s - m) for s, m in zip(ss, ms)]
            out = []
            for (m_old, l, acc), m, p, sl in zip(carry, ms, ps, kv):
                alpha = jnp.exp(m_old - m)
                out.append((m, alpha * l + jnp.sum(p, axis=1, keepdims=True),
                            alpha * acc + jnp.dot(p.astype(BF16), vc_ref[0, pl.ds(start, bk), sl],
                                                  preferred_element_type=F32)))
            return tuple(out)
        init = (jnp.full((rep * bq, 1), NEG, F32), jnp.zeros((rep * bq, 1), F32),
                jnp.zeros((rep * bq, HEAD_DIM), F32))
        res = lax.fori_loop(0, n_kb, att_body, (init,) * pair)
        for g, (_, l, acc) in zip(gs, res):
            o = acc / l
            for r in range(rep):
                o_ref[0, :, head(g, r)] = o[r * bq:(r + 1) * bq].astype(o_ref.dtype)


def dsa_attention(qc, qi_hm, wi, kc, vc, ki, q_off, s_real, topk, bq, bk):
    b, tq, _ = qc.shape
    s_pad = kc.shape[1]
    assert tq % bq == 0 and s_pad % bk == 0 and bk % LANES == 0
    nq = tq // bq
    tri = jnp.asarray(np.triu(np.ones((bk, bk), np.float32), 1), BF16)
    kern = functools.partial(_dsa_kernel, bq=bq, bk=bk, q_off=q_off, s_real=s_real, s_pad=s_pad,
                             topk=topk, scale=HEAD_DIM ** -0.5)
    once = dict(pipeline_mode=pl.Buffered(1))
    return pl.pallas_call(
        kern,
        grid=(b, nq),
        in_specs=[pl.BlockSpec((1, bq, H_C * HEAD_DIM), lambda b_, i: (b_, i, 0)),
                  pl.BlockSpec((H_IDX, bq, D_IDX), lambda b_, i: (0, b_ * nq + i, 0)),
                  pl.BlockSpec((1, bq, LANES), lambda b_, i: (b_, i, 0)),
                  pl.BlockSpec((1, s_pad, KV_C * HEAD_DIM), lambda b_, i: (b_, 0, 0), **once),
                  pl.BlockSpec((1, s_pad, KV_C * HEAD_DIM), lambda b_, i: (b_, 0, 0), **once),
                  pl.BlockSpec((1, s_pad, D_IDX), lambda b_, i: (b_, 0, 0), **once),
                  pl.BlockSpec((bk, bk), lambda b_, i: (0, 0))],
        out_specs=pl.BlockSpec((1, bq, H_C * HEAD_DIM), lambda b_, i: (b_, i, 0)),
        out_shape=jax.ShapeDtypeStruct((b, tq, H_C * HEAD_DIM), BF16),
        scratch_shapes=[pltpu.VMEM((bq, s_pad), I32), pltpu.VMEM((bq, s_pad), F32)],
        compiler_params=_params("arbitrary", "arbitrary"),
        name="dsa_attention",
    )(qc, qi_hm, wi, kc, vc, ki, tri)


HALO = 8


def _gdn_prep_kernel(x_ref, prev_ref, halo_ref, cw_ref, a_ref, b_ref, alog_ref, dtb_ref, o_ref, g_ref,
                     beta_ref, *, heads_per_blk, blocks_per_seq):
    j = pl.program_id(1)
    x = x_ref[...]
    tb = x.shape[0]
    halo = jnp.where(pl.program_id(0) % blocks_per_seq == 0, halo_ref[0], prev_ref[...])
    row = lax.broadcasted_iota(I32, halo.shape, 0)
    acc = x * cw_ref[CONV_W - 1:CONV_W, :]
    for k in range(1, CONV_W):
        rolled = pltpu.roll(x, k, 0)
        top = jnp.where(row < k, pltpu.roll(halo, k, 0), rolled[:HALO])
        shifted = jnp.concatenate([top, rolled[HALO:]], axis=0) if tb > HALO else top
        acc = acc + shifted * cw_ref[CONV_W - 1 - k:CONV_W - k, :]
    y = acc * (1.0 / (1.0 + jnp.exp(-acc)))
    seg = j // (H_D // heads_per_blk)
    scale = jnp.where(seg == 0, DK_D ** -0.5, 1.0)
    for h in range(heads_per_blk):
        sl = slice(h * LANES, (h + 1) * LANES)
        t = y[:, sl]
        nrm = t * lax.rsqrt(jnp.sum(t * t, axis=1, keepdims=True) + RMS_EPS) * scale
        o_ref[:, sl] = jnp.where(seg == 2, t, nrm)

    @pl.when(j == 0)
    def _():
        z = a_ref[...] + dtb_ref[...]
        softplus = jnp.maximum(z, 0.0) + jnp.log(1.0 + jnp.exp(-jnp.abs(z)))
        g_ref[...] = -jnp.exp(alog_ref[...]) * softplus
        beta_ref[...] = 1.0 / (1.0 + jnp.exp(-b_ref[...]))


def gdn_prep(proj, n_rows, halo, conv_w, a_log_pad, dt_bias_pad, qkv_col, a_col, b_col, tb):
    cb = 512
    hpb = cb // LANES
    bps = n_rows // tb // halo.shape[0]
    kern = functools.partial(_gdn_prep_kernel, heads_per_blk=hpb, blocks_per_seq=bps)
    vec = pl.BlockSpec((1, LANES), lambda i, j: (0, 0))
    return pl.pallas_call(
        kern,
        grid=(n_rows // tb, QKV_D // cb),
        in_specs=[pl.BlockSpec((tb, cb), lambda i, j: (i, qkv_col // cb + j)),
                  pl.BlockSpec((HALO, cb), lambda i, j: (jnp.maximum(i * (tb // HALO) - 1, 0),
                                                        qkv_col // cb + j)),
                  pl.BlockSpec((1, HALO, cb), lambda i, j: (i // bps, 0, j)),
                  pl.BlockSpec((CONV_W, cb), lambda i, j: (0, j)),
                  pl.BlockSpec((tb, LANES), lambda i, j: (i, a_col // LANES)),
                  pl.BlockSpec((tb, LANES), lambda i, j: (i, b_col // LANES)), vec, vec],
        out_specs=[pl.BlockSpec((tb, cb), lambda i, j: (i, j)),
                   pl.BlockSpec((tb, LANES), lambda i, j: (i, 0)),
                   pl.BlockSpec((tb, LANES), lambda i, j: (i, 0))],
        out_shape=[jax.ShapeDtypeStruct((n_rows, QKV_D), F32), jax.ShapeDtypeStruct((n_rows, LANES), F32),
                   jax.ShapeDtypeStruct((n_rows, LANES), F32)],
        compiler_params=_params("arbitrary", "arbitrary"),
        name="gdn_prep",
    )(proj, proj, halo, conv_w, proj, proj, a_log_pad, dt_bias_pad)


def _split3(x):
    a = x.astype(BF16)
    r = x - a.astype(F32)
    b = r.astype(BF16)
    c = (r - b.astype(F32)).astype(BF16)
    return a, b, c


def _dot_f32(a, b):
    a_hi, a_lo, _ = _split3(a)
    b_hi, b_lo, _ = _split3(b)
    return (jnp.dot(a_hi, b_hi, preferred_element_type=F32) + jnp.dot(a_hi, b_lo, preferred_element_type=F32)
            + jnp.dot(a_lo, b_hi, preferred_element_type=F32))


def _gdn_local_kernel(q_ref, k_ref, v_ref, g_ref, beta_ref, solv_ref, solk_ref, qe_ref, kd_ref, qkd_ref,
                      egl_ref, *, c, hb):
    assert hb == H_D
    ri = lax.broadcasted_iota(I32, (c, c), 0)
    ci = lax.broadcasted_iota(I32, (c, c), 1)
    incl, strict, eye = ri >= ci, ri > ci, ri == ci
    tril = jnp.where(incl, 1.0, 0.0).astype(BF16)
    g_parts = _split3(g_ref[...])
    gcum = sum(jnp.dot(tril, p, preferred_element_type=F32) for p in g_parts)
    gcum_t = sum(lax.dot_general(p, tril, (((0,), (1,)), ((), ())), preferred_element_type=F32)
                 for p in g_parts)
    beta_all = beta_ref[...]
    egl_ref[0] = jnp.exp(gcum[c - 1:c, :])
    ident = jnp.where(eye, 1.0, 0.0)
    heads = range(hb)
    sls = [slice(hh * LANES, (hh + 1) * LANES) for hh in heads]
    nmats, rhs = [], []
    for hh in heads:
        gc = gcum[:, hh:hh + 1]
        bc = beta_all[:, hh:hh + 1]
        decay = jnp.where(incl, jnp.exp(jnp.where(incl, gc - gcum_t[hh:hh + 1, :], 0.0)), 0.0)
        q, k, v = q_ref[:, sls[hh]], k_ref[:, sls[hh]], v_ref[:, sls[hh]]
        qb, kb = q.astype(BF16), k.astype(BF16)
        kk = lax.dot_general(kb, kb, (((1,), (1,)), ((), ())), preferred_element_type=F32)
        qk = lax.dot_general(qb, kb, (((1,), (1,)), ((), ())), preferred_element_type=F32)
        nmats.append(jnp.where(strict, bc * kk * decay, 0.0))
        rhs.append(jnp.concatenate([v * bc, k * (bc * jnp.exp(gc))], axis=1))
        qkd_ref[0, hh] = (qk * decay).astype(BF16)
        qe_ref[:, sls[hh]] = (q * jnp.exp(gc)).astype(BF16)
        kd_ref[:, sls[hh]] = (k * jnp.exp(gc[c - 1:c, :] - gc)).astype(BF16)
    invs = [ident - n for n in nmats]
    pows = nmats
    for _ in range(int(np.log2(c)) - 1):
        pows = [_dot_f32(p, p) for p in pows]
        invs = [t + _dot_f32(t, p) for t, p in zip(invs, pows)]
    for hh in heads:
        y = _dot_f32(invs[hh], rhs[hh])
        solv_ref[:, sls[hh]] = y[:, :DV_D]
        solk_ref[:, sls[hh]] = y[:, DV_D:].astype(BF16)


def gdn_local(qkv, g, beta, c, hb=H_D):
    n = qkv.shape[0]
    nc = n // c
    dh = H_D * DK_D
    kern = functools.partial(_gdn_local_kernel, c=c, hb=hb)
    w = hb * LANES
    seg = lambda s: pl.BlockSpec((c, w), lambda i, j: (i, s * (H_D // hb) + j))
    col = pl.BlockSpec((c, w), lambda i, j: (i, j))
    meta = pl.BlockSpec((c, LANES), lambda i, j: (i, 0))
    return pl.pallas_call(
        kern,
        grid=(nc, H_D // hb),
        in_specs=[seg(0), seg(1), seg(2), meta, meta],
        out_specs=[col, col, col, col,
                   pl.BlockSpec((1, hb, c, c), lambda i, j: (i, j, 0, 0)),
                   pl.BlockSpec((1, 1, LANES), lambda i, j: (i, 0, 0))],
        out_shape=[jax.ShapeDtypeStruct((n, dh), F32), jax.ShapeDtypeStruct((n, dh), BF16),
                   jax.ShapeDtypeStruct((n, dh), BF16), jax.ShapeDtypeStruct((n, dh), BF16),
                   jax.ShapeDtypeStruct((nc, H_D, c, c), BF16),
                   jax.ShapeDtypeStruct((nc, 1, LANES), F32)],
        compiler_params=_params("arbitrary", "arbitrary"),
        name="gdn_local",
    )(qkv, qkv, qkv, g, beta)


def _gdn_scan_kernel(s0_ref, solv_ref, solk_ref, qe_ref, kd_ref, qkd_ref, egl_ref, gate_ref, nw_ref,
                     od_ref, sout_ref, s_ref):
    ci = pl.program_id(1)

    @pl.when(ci == 0)
    def _():
        s_ref[...] = s0_ref[0]

    egl = egl_ref[0]
    nw = nw_ref[...]
    for h in range(H_D):
        sl = slice(h * LANES, (h + 1) * LANES)
        s_old = s_ref[h]
        sb = s_old.astype(BF16)
        u = solv_ref[:, sl] - jnp.dot(solk_ref[:, sl], sb, preferred_element_type=F32)
        ub = u.astype(BF16)
        o = (jnp.dot(qe_ref[:, sl], sb, preferred_element_type=F32)
             + jnp.dot(qkd_ref[0, h], ub, preferred_element_type=F32))
        s_ref[h] = s_old * egl[:, h:h + 1] + lax.dot_general(
            kd_ref[:, sl], ub, (((0,), (0,)), ((), ())), preferred_element_type=F32)
        o = o * lax.rsqrt(jnp.mean(o * o, axis=1, keepdims=True) + RMS_EPS) * nw
        gt = gate_ref[:, sl]
        od_ref[:, sl] = (o * (gt * (1.0 / (1.0 + jnp.exp(-gt))))).astype(od_ref.dtype)

    @pl.when(ci == pl.num_programs(1) - 1)
    def _():
        sout_ref[0] = s_ref[...]


def gdn_scan(s0, solv, solk, qe, kd, qkd, egl, gate_arr, gate_col, norm_w, c):
    b = s0.shape[0]
    n = solv.shape[0]
    cps = n // b // c
    dh = H_D * DV_D
    col = pl.BlockSpec((c, dh), lambda b_, i: (b_ * cps + i, 0))
    st = pl.BlockSpec((1, H_D, DK_D, DV_D), lambda b_, i: (b_, 0, 0, 0))
    return pl.pallas_call(
        _gdn_scan_kernel,
        grid=(b, cps),
        in_specs=[st, col, col, col, col,
                  pl.BlockSpec((1, H_D, c, c), lambda b_, i: (b_ * cps + i, 0, 0, 0)),
                  pl.BlockSpec((1, 1, LANES), lambda b_, i: (b_ * cps + i, 0, 0)),
                  pl.BlockSpec((c, dh), lambda b_, i: (b_ * cps + i, gate_col // dh)),
                  pl.BlockSpec((1, LANES), lambda b_, i: (0, 0))],
        out_specs=[col, st],
        out_shape=[jax.ShapeDtypeStruct((n, dh), BF16), jax.ShapeDtypeStruct(s0.shape, F32)],
        scratch_shapes=[pltpu.VMEM((H_D, DK_D, DV_D), F32)],
        compiler_params=_params("arbitrary", "arbitrary"),
        name="gdn_scan",
    )(s0, solv, solk, qe, kd, qkd, egl, gate_arr, norm_w.reshape(1, LANES))


def gdn_mixer(proj, n_rows, cols, halo, s0, conv_w, a_log, dt_bias, norm_w, c, tb):
    pad = lambda v: jnp.pad(v.astype(F32), (0, LANES - H_D)).reshape(1, LANES)
    qkv, g, beta = gdn_prep(proj, n_rows, halo, conv_w, pad(a_log), pad(dt_bias),
                            cols['qkv'], cols['a'], cols['b'], tb)
    solv, solk, qe, kd, qkd, egl = gdn_local(qkv, g, beta, c)
    return gdn_scan(s0, solv, solk, qe, kd, qkd, egl, proj, cols['g'], norm_w, c)


ODD_COLS = {}
_orig, _new = 0, 0
_order = dict(qc=0, kc=1, vc=2, qi=3, ki=4, wi=5, qkv=6, a=7, b=8, g=9)
_starts = np.concatenate([[0], np.cumsum(ODD_SIZES)])
for _name in ('qc', 'g', 'kc', 'vc', 'qi', 'qkv', 'ki', 'wi', 'a', 'b'):
    _w = ODD_SIZES[_order[_name]]
    ODD_COLS[_name] = (_new, int(_starts[_order[_name]]), _w)
    _new += -(-_w // LANES) * LANES
ODD_WIDTH = _new


def reorder_w_in_odd(w):
    parts = []
    for name in ('qc', 'g', 'kc', 'vc', 'qi', 'qkv', 'ki', 'wi', 'a', 'b'):
        _, o0, wd = ODD_COLS[name]
        parts.append(jnp.pad(w[:, o0:o0 + wd], ((0, 0), (0, (-wd) % LANES))))
    return jnp.concatenate(parts, axis=1).astype(BF16)


def odd_layer(x, n_p, bs, n, cache_c_k, cache_c_v, cache_c_idx_k, state_d_ssm, state_d_conv,
              w_in_r, w_out, conv_w, a_log, dt_bias, norm_w):
    cols = {k: v[0] for k, v in ODD_COLS.items()}
    past = cache_c_k.shape[1]
    assert n >= CONV_W - 1 and n_p % 256 == 0
    proj = matmul(x.astype(BF16), w_in_r, tn=ODD_WIDTH // 10)
    pos = jnp.concatenate([jnp.arange(n_p), jnp.tile(past + jnp.arange(n), bs)])
    qc_b, kc_f, kc_b, vc_b, qi_hm, ki_f, ki_b, wi_f = rope_split(proj, pos, cols)
    dk = KV_C * HEAD_DIM
    topk_p = min(TOPK_MAX, n_p // 4)
    oc_p = dsa_attention(qc_b[:n_p][None], qi_hm[:, :n_p], wi_f[:n_p][None], kc_b[:n_p][None],
                         vc_b[:n_p][None], ki_b[:n_p, :D_IDX][None], 0, n_p, topk_p, 128, 512)
    bk_s = 384
    s_real = past + n
    s_pad = -(-s_real // bk_s) * bk_s
    cat = lambda c, new: jnp.concatenate(
        [c.reshape(bs, past, -1).astype(BF16), new, jnp.zeros((bs, s_pad - s_real, new.shape[-1]), BF16)], 1)
    oc_s = dsa_attention(qc_b[n_p:].reshape(bs, n, -1), qi_hm[:, n_p:], wi_f[n_p:].reshape(bs, n, LANES),
                         cat(cache_c_k, kc_b[n_p:].reshape(bs, n, dk)),
                         cat(cache_c_v, vc_b[n_p:].reshape(bs, n, dk)),
                         cat(cache_c_idx_k, ki_b[n_p:, :D_IDX].reshape(bs, n, D_IDX)),
                         past, s_real, min(TOPK_MAX, s_real // 4), n, bk_s)
    tb_p = 256
    q0 = cols['qkv']
    halo_p = jnp.zeros((1, HALO, QKV_D), F32)
    s0_p = jnp.zeros((1, H_D, DK_D, DV_D), F32)
    od_p, s_p = gdn_mixer(proj, n_p, cols, halo_p, s0_p, conv_w, a_log, dt_bias, norm_w, CHUNK, tb_p)
    proj_s = proj[n_p:]
    halo_s = jnp.concatenate([jnp.zeros((bs, HALO - (CONV_W - 1), QKV_D), F32), state_d_conv], 1)
    od_s, s_s = gdn_mixer(proj_s, bs * n, cols, halo_s, state_d_ssm, conv_w, a_log, dt_bias, norm_w, n, n)
    dh = H_C * HEAD_DIM
    m = matmul_concat(jnp.concatenate([oc_p.reshape(n_p, dh), oc_s.reshape(bs * n, dh)], 0),
                      jnp.concatenate([od_p, od_s], 0), w_out)
    v0 = cols['vc']
    qkv_s = proj_s[:, q0:q0 + QKV_D].reshape(bs, n, QKV_D)
    caches = (kc_f[:n_p].reshape(1, n_p, KV_C, HEAD_DIM), proj[:n_p, v0:v0 + dk].reshape(1, n_p, KV_C, HEAD_DIM),
              ki_f[:n_p, :D_IDX][None],
              kc_f[n_p:].reshape(bs, n, KV_C, HEAD_DIM), proj_s[:, v0:v0 + dk].reshape(bs, n, KV_C, HEAD_DIM),
              ki_f[n_p:, :D_IDX].reshape(bs, n, D_IDX),
              s_p, proj[n_p - (CONV_W - 1):n_p, q0:q0 + QKV_D][None],
              s_s, qkv_s[:, n - (CONV_W - 1):])
    return m, caches


def kernel(x_prompt, x_sample, cache_a_k, cache_a_v, cache_b_k, cache_b_v, cache_c_k, cache_c_v,
           cache_c_idx_k, state_d_ssm, state_d_conv, w_in_even, w_out_even, b_rel_bias, w_in_odd,
           w_out_odd, d_conv_w, d_a_log, d_dt_bias, d_norm_w, w_router, router_bias, moe_w_gate,
           moe_w_up, moe_w_down, ln_g, ln_b):
    bp, t, d = x_prompt.shape
    bs, n, _ = x_sample.shape
    assert bp == 1
    n_p = bp * t
    w_in_even = w_in_even.astype(BF16)
    w_out_even = w_out_even.astype(BF16)
    w_in_odd_r = reorder_w_in_odd(w_in_odd)
    w_out_odd = w_out_odd.astype(BF16)
    wr_pad = jnp.pad(w_router.astype(BF16), ((0, 0), (0, LANES - N_EXPERTS)))
    rb_pad = jnp.pad(router_bias.astype(F32), (0, LANES - N_EXPERTS)).reshape(1, LANES)
    x = jnp.concatenate([x_prompt.reshape(n_p, d), x_sample.reshape(bs * n, d)], 0)
    for layer in range(DEPTH):
        if layer % 2 == 0:
            xp, xs = x[:n_p].reshape(bp, t, d), x[n_p:].reshape(bs, n, d)
            m, (a_k_p, a_v_p, a_k_s, a_v_s, b_k_p, b_v_p, b_k_s, b_v_s) = even_layer(
                xp, xs, cache_a_k, cache_a_v, cache_b_k, cache_b_v, w_in_even, w_out_even, b_rel_bias)
        else:
            m, (c_k_p, c_v_p, c_i_p, c_k_s, c_v_s, c_i_s, d_s_p, d_c_p, d_s_s, d_c_s) = odd_layer(
                x, n_p, bs, n, cache_c_k, cache_c_v, cache_c_idx_k, state_d_ssm, state_d_conv,
                w_in_odd_r, w_out_odd, d_conv_w, d_a_log, d_dt_bias, d_norm_w)
        x = moe_block(x, m, ln_g[layer], ln_b[layer], wr_pad, rb_pad,
                      moe_w_gate, moe_w_up, moe_w_down, layer)
    xp, xs = x[:n_p].reshape(bp, t, d), x[n_p:].reshape(bs, n, d)
    return (xp, xs, a_k_p, a_v_p, a_k_s, a_v_s, b_k_p, b_v_p, b_k_s, b_v_s,
            c_k_p, c_v_p, c_i_p, c_k_s, c_v_s, c_i_s, d_s_p, d_c_p, d_s_s, d_c_s)
```
